```python
import jax, jax.numpy as jnp
from jax import lax
import numpy as np

D_MODEL = 1024
BATCH = 8
SEQ = 2048
DEPTH = 1

PLE_DIM = 256
MIX_WIDTH = D_MODEL
GDN_WIDTH = D_MODEL // 2
GDN_HEAD_DIM = 128
GDN_HEADS = GDN_WIDTH // GDN_HEAD_DIM
FOX_WIDTH = MIX_WIDTH - GDN_WIDTH
FOX_HEAD_DIM = 64
FOX_HEADS = FOX_WIDTH // FOX_HEAD_DIM
CONV_WIDTH = 4
CHUNK = 64
Q_BLOCK = 128
D_FF = -(-8 * D_MODEL // (3 * 256)) * 256
EPS = 1e-6

IN_SIZES = [3 * GDN_WIDTH, GDN_WIDTH, GDN_HEADS, GDN_HEADS, 3 * FOX_WIDTH, FOX_HEADS]
D_IN = sum(IN_SIZES)
IN_SPLITS = [sum(IN_SIZES[:i + 1]) for i in range(len(IN_SIZES) - 1)]

kernel_name = 'hymba_style_gdn_fox_hybrid'


def rmsnorm(x, w):
    xf = x.astype(jnp.float32)
    y = xf * lax.rsqrt(jnp.mean(xf * xf, axis=-1, keepdims=True) + EPS)
    return (y * w.astype(jnp.float32)).astype(x.dtype)


def l2norm(x):
    return x * lax.rsqrt(jnp.sum(x * x, axis=-1, keepdims=True) + EPS)


def causal_depthwise_conv(x, w):
    c = x.shape[-1]
    return lax.conv_general_dilated(
        x, w.astype(x.dtype)[:, None, :], window_strides=(1,),
        padding=[(CONV_WIDTH - 1, 0)], dimension_numbers=('NWC', 'WIO', 'NWC'),
        feature_group_count=c)


def gated_delta_rule(q, k, v, g, beta):
    b, s, h, dk = q.shape
    dv = v.shape[-1]
    n = s // CHUNK

    def chunk(t):
        return jnp.moveaxis(t.reshape((b, n, CHUNK, h) + t.shape[3:]), 3, 2)

    q = chunk(q * (dk ** -0.5))
    k, v, g, beta = chunk(k), chunk(v), chunk(g), chunk(beta)
    gc = jnp.cumsum(g, axis=-1)
    lower = jnp.tril(jnp.ones((CHUNK, CHUNK), bool))
    strict = jnp.tril(jnp.ones((CHUNK, CHUNK), bool), -1)
    diff = gc[..., :, None] - gc[..., None, :]
    decay = jnp.where(lower, jnp.exp(jnp.where(lower, diff, 0.0)), 0.0)
    kb = k * beta[..., None]
    vb = v * beta[..., None]
    kk = jnp.einsum('bnhid,bnhjd->bnhij', kb, k) * decay
    a_mat = jnp.where(strict, kk, 0.0) + jnp.eye(CHUNK, dtype=kk.dtype)
    rhs = jnp.concatenate([vb, kb * jnp.exp(gc)[..., None]], axis=-1)
    sol = lax.linalg.triangular_solve(a_mat, rhs, left_side=True, lower=True, unit_diagonal=True)
    u, w = sol[..., :dv], sol[..., dv:]
    qk = jnp.where(lower, jnp.einsum('bnhid,bnhjd->bnhij', q, k) * decay, 0.0)

    def step(state, xs):
        q_c, k_c, u_c, w_c, gc_c, qk_c = xs
        v_new = u_c - jnp.einsum('bhcd,bhde->bhce', w_c, state)
        o = (jnp.einsum('bhcd,bhde->bhce', q_c * jnp.exp(gc_c)[..., None], state)
             + jnp.einsum('bhij,bhje->bhie', qk_c, v_new))
        g_last = gc_c[..., -1]
        k_dec = k_c * jnp.exp(g_last[..., None] - gc_c)[..., None]
        state = state * jnp.exp(g_last)[..., None, None] + jnp.einsum('bhcd,bhce->bhde', k_dec, v_new)
        return state, o

    xs = tuple(jnp.moveaxis(t, 1, 0) for t in (q, k, u, w, gc, qk))
    state0 = jnp.zeros((b, h, dk, dv), jnp.float32)
    _, o = lax.scan(step, state0, xs)
    o = jnp.moveaxis(jnp.moveaxis(o, 0, 1), 2, 3)
    return o.reshape(b, s, h, dv)


def forgetting_attention(q, k, v, log_f):
    b, s, h, d = q.shape
    scale = d ** -0.5
    c = jnp.cumsum(log_f.astype(jnp.float32), axis=1).transpose(0, 2, 1)
    q, k, v = (t.transpose(0, 2, 1, 3) for t in (q, k, v))
    outs = []
    for blk in range(s // Q_BLOCK):
        q0, q1 = blk * Q_BLOCK, (blk + 1) * Q_BLOCK
        logits = jnp.einsum('bhqd,bhkd->bhqk', q[:, :, q0:q1], k[:, :, :q1],
                            preferred_element_type=jnp.float32) * scale
        logits = logits + c[:, :, q0:q1, None] - c[:, :, None, :q1]
        causal = jnp.arange(q0, q1)[:, None] >= jnp.arange(q1)[None, :]
        logits = jnp.where(causal, logits, -jnp.inf)
        prob = jax.nn.softmax(logits, axis=-1)
        outs.append(jnp.einsum('bhqk,bhkd->bhqd', prob.astype(v.dtype), v[:, :, :q1]))
    o = jnp.concatenate(outs, axis=2)
    return o.transpose(0, 2, 1, 3)


def setup_inputs(seed: int = 0) -> dict:
    key = jax.random.key(seed)
    ks = jax.random.split(key, 20)
    f32 = jnp.float32
    nrm = lambda k, shape, s: jax.random.normal(k, shape, f32) * s
    gain = lambda k, shape: 1.0 + 0.1 * jax.random.normal(k, shape, f32)
    dt = jnp.exp(jax.random.uniform(ks[5], (DEPTH, GDN_HEADS), f32) * (np.log(0.1) - np.log(0.001)) + np.log(0.001))
    dt_bias = dt + jnp.log(-jnp.expm1(-dt))
    return {
        'x': nrm(ks[0], (BATCH, SEQ, D_MODEL), 1.0),
        'p': nrm(ks[1], (DEPTH, BATCH, SEQ, PLE_DIM), 1.0),
        'attn_norm_w': gain(ks[2], (DEPTH, D_MODEL)),
        'w_in': nrm(ks[3], (DEPTH, D_MODEL, D_IN), D_MODEL ** -0.5),
        'conv_w': nrm(ks[4], (DEPTH, CONV_WIDTH, 3 * GDN_WIDTH), CONV_WIDTH ** -0.5),
        'a_log': jnp.log(jax.random.uniform(ks[6], (DEPTH, GDN_HEADS), f32, 1.0, 16.0)),
        'dt_bias': dt_bias,
        'gdn_norm_w': gain(ks[7], (DEPTH, GDN_HEAD_DIM)),
        'fox_f_bias': 3.0 + 0.5 * jax.random.normal(ks[8], (DEPTH, FOX_HEADS), f32),
        'w_out': nrm(ks[9], (DEPTH, MIX_WIDTH, D_MODEL), MIX_WIDTH ** -0.5),
        'ffn_norm_w': gain(ks[10], (DEPTH, D_MODEL)),
        'w_gate_up': nrm(ks[11], (DEPTH, D_MODEL, 2 * D_FF), D_MODEL ** -0.5),
        'w_down': nrm(ks[12], (DEPTH, D_FF, D_MODEL), D_FF ** -0.5),
        'ple_norm_w': gain(ks[13], (DEPTH, D_MODEL)),
        'w_ple_gate': nrm(ks[14], (DEPTH, D_MODEL, D_MODEL), D_MODEL ** -0.5),
        'w_ple_proj': nrm(ks[15], (DEPTH, PLE_DIM, D_MODEL), PLE_DIM ** -0.5),
        'final_norm_w': gain(ks[16], (D_MODEL,)),
    }


def reference(x, p, attn_norm_w, w_in, conv_w, a_log, dt_bias, gdn_norm_w, fox_f_bias, w_out,
              ffn_norm_w, w_gate_up, w_down, ple_norm_w, w_ple_gate, w_ple_proj, final_norm_w):
    b, s, _ = x.shape
    f32 = jnp.float32
    h = x
    for i in range(DEPTH):
        xn = rmsnorm(h, attn_norm_w[i])
        proj = xn @ w_in[i]
        g_qkv, g_z, g_a, g_b, f_qkv, f_f = jnp.split(proj, IN_SPLITS, axis=-1)

        g_qkv = jax.nn.silu(causal_depthwise_conv(g_qkv, conv_w[i])).astype(f32)
        gq, gk, gv = jnp.split(g_qkv.reshape(b, s, 3 * GDN_HEADS, GDN_HEAD_DIM), 3, axis=2)
        gq, gk = l2norm(gq), l2norm(gk)
        beta = jax.nn.sigmoid(g_b.astype(f32))
        g_decay = -jnp.exp(a_log[i].astype(f32)) * jax.nn.softplus(g_a.astype(f32) + dt_bias[i].astype(f32))
        o_g = gated_delta_rule(gq, gk, gv, g_decay, beta)
        o_g = o_g * lax.rsqrt(jnp.mean(o_g * o_g, axis=-1, keepdims=True) + EPS) * gdn_norm_w[i].astype(f32)
        o_g = o_g * jax.nn.silu(g_z.astype(f32)).reshape(b, s, GDN_HEADS, GDN_HEAD_DIM)
        o_g = o_g.reshape(b, s, GDN_WIDTH).astype(h.dtype)

        fq, fk, fv = jnp.split(f_qkv.reshape(b, s, 3 * FOX_HEADS, FOX_HEAD_DIM), 3, axis=2)
        log_f = jax.nn.log_sigmoid(f_f.astype(f32) + fox_f_bias[i].astype(f32))
        o_f = forgetting_attention(fq, fk, fv, log_f).reshape(b, s, FOX_WIDTH).astype(h.dtype)

        h = h + jnp.concatenate([o_g, o_f], axis=-1) @ w_out[i]

        hn = rmsnorm(h, ffn_norm_w[i])
        gate, up = jnp.split(hn @ w_gate_up[i], 2, axis=-1)
        h = h + (jax.nn.silu(gate) * up) @ w_down[i]

        ple_gate = jax.nn.sigmoid(rmsnorm(h, ple_norm_w[i]) @ w_ple_gate[i])
        h = h + ple_gate * (p[i] @ w_ple_proj[i])
    return rmsnorm(h, final_norm_w)
```

```python
import functools

import jax
import jax.numpy as jnp
from jax import lax
from jax.experimental import pallas as pl
from jax.experimental.pallas import tpu as pltpu

F32 = jnp.float32
BF16 = jnp.bfloat16

EPS = 1e-6
GDN_HEADS = 4
GDN_HEAD_DIM = 128
GDN_WIDTH = GDN_HEADS * GDN_HEAD_DIM
FOX_HEADS = 8
FOX_HEAD_DIM = 64
FOX_WIDTH = FOX_HEADS * FOX_HEAD_DIM
CONV_WIDTH = 4
CHUNK = 64
LANES = 128
SUBLANES = 8
NEG_BIG = -1e30

GC_LANE = 0
BETA_LANE = 4
CF_LANE = 8
GATE_ROWS = 16

VMEM_LIMIT = 56 * 1024 * 1024

TM_IN = 256
TG_GDN = 256
TQ_FOX = 256
TM_POST = 512
TF_POST = 256


def _dot(a, b):
    return jnp.dot(a, b, preferred_element_type=F32)


def _dot_nt(a, b):
    return lax.dot_general(a, b, (((1,), (1,)), ((), ())), preferred_element_type=F32)


def _dot_tn(a, b):
    return lax.dot_general(a, b, (((0,), (0,)), ((), ())), preferred_element_type=F32)


def _dot_f32(a, b):
    return jnp.dot(a, b, preferred_element_type=F32, precision=lax.Precision.HIGHEST)


def _sigmoid(x):
    return 1.0 / (1.0 + jnp.exp(-x))


def _softplus(x):
    return jnp.maximum(x, 0.0) + jnp.log1p(jnp.exp(-jnp.abs(x)))


def _rms_scale(x):
    return lax.rsqrt(jnp.mean(x * x, axis=-1, keepdims=True) + EPS)


def _const_spec(shape):
    nd = len(shape)
    return pl.BlockSpec(shape, lambda *_: (0,) * nd, pipeline_mode=pl.Buffered(1))


def _inproj_kernel(x_ref, nw_ref, wg_ref, wz_ref, wf_ref, ws_ref, conv_ref, prm_ref,
                   gq_ref, gk_ref, gv_ref, gz_ref, fq_ref, fk_ref, fv_ref, gates_ref, gates_t_ref,
                   buf_ref, carry_ref, *, tm, tiles_per_seq):
    i = pl.program_id(0)
    seq_start = (i % tiles_per_seq) == 0
    halo = SUBLANES

    x = x_ref[...]
    xn = (x * _rms_scale(x) * nw_ref[...]).astype(BF16)

    @pl.when(seq_start)
    def _():
        buf_ref[0:halo, :] = jnp.zeros((halo, buf_ref.shape[1]), F32)
        carry_ref[...] = jnp.zeros_like(carry_ref)

    @pl.when(jnp.logical_not(seq_start))
    def _():
        buf_ref[0:halo, :] = buf_ref[tm:tm + halo, :]

    buf_ref[halo:halo + tm, :] = _dot(xn, wg_ref[...])

    out_refs = (gq_ref, gk_ref, gv_ref)
    for j in range(3 * GDN_HEADS):
        cols = slice(j * LANES, (j + 1) * LANES)
        acc = buf_ref[halo:halo + tm, cols] * conv_ref[CONV_WIDTH - 1:CONV_WIDTH, cols]
        for t in range(CONV_WIDTH - 1):
            off = halo - (CONV_WIDTH - 1) + t
            acc = acc + buf_ref[off:off + tm, cols] * conv_ref[t:t + 1, cols]
        y = acc * _sigmoid(acc)
        kind, head = divmod(j, GDN_HEADS)
        if kind < 2:
            y = y * lax.rsqrt(jnp.sum(y * y, axis=-1, keepdims=True) + EPS)
        if kind == 0:
            y = y * (GDN_HEAD_DIM ** -0.5)
        out_refs[kind][:, head * LANES:(head + 1) * LANES] = y

    gz_ref[...] = _dot(xn, wz_ref[...])

    f_qkv = _dot(xn, wf_ref[...])
    fq_ref[...] = (f_qkv[:, 0:FOX_WIDTH] * (FOX_HEAD_DIM ** -0.5)).astype(BF16)
    fk_ref[...] = f_qkv[:, FOX_WIDTH:2 * FOX_WIDTH].astype(BF16)
    fv_ref[...] = f_qkv[:, 2 * FOX_WIDTH:3 * FOX_WIDTH].astype(BF16)

    z = _dot(xn, ws_ref[...]) + prm_ref[0:1, :]
    lane = lax.broadcasted_iota(jnp.int32, (tm, LANES), 1)
    row = lax.broadcasted_iota(jnp.int32, (tm, LANES), 0)
    is_gc = lane < BETA_LANE
    is_cf = (lane >= CF_LANE) & (lane < CF_LANE + FOX_HEADS)
    soft = jnp.log1p(jnp.exp(-jnp.abs(z)))
    g_decay = -jnp.exp(prm_ref[1:2, :]) * (jnp.maximum(z, 0.0) + soft)
    log_f = -(jnp.maximum(-z, 0.0) + soft)
    val = jnp.where(is_gc, g_decay, jnp.where(lane < CF_LANE, _sigmoid(z), jnp.where(is_cf, log_f, 0.0)))

    pos = jnp.where(is_gc, row % CHUNK, row)
    summed = is_gc | is_cf
    d = 1
    while d < tm:
        shifted = pltpu.roll(val, d, axis=0)
        val = val + jnp.where(summed & (pos >= d), shifted, 0.0)
        d *= 2
    val = val + jnp.where(is_cf, carry_ref[...], 0.0)
    carry_ref[...] = val[tm - 1:tm, :]

    gates_ref[...] = val
    gates_t_ref[0] = val.T[0:GATE_ROWS, :]


def _inproj(x2, nw, wg, wz, wf, ws, conv_w, prm, *, batch, seq):
    m, d = x2.shape
    tm = TM_IN
    tiles_per_seq = seq // tm
    grid = (m // tm,)
    tok = lambda w: pl.BlockSpec((tm, w), lambda i: (i, 0))
    out_shape = (
        jax.ShapeDtypeStruct((m, GDN_WIDTH), F32),
        jax.ShapeDtypeStruct((m, GDN_WIDTH), F32),
        jax.ShapeDtypeStruct((m, GDN_WIDTH), F32),
        jax.ShapeDtypeStruct((m, GDN_WIDTH), F32),
        jax.ShapeDtypeStruct((m, FOX_WIDTH), BF16),
        jax.ShapeDtypeStruct((m, FOX_WIDTH), BF16),
        jax.ShapeDtypeStruct((m, FOX_WIDTH), BF16),
        jax.ShapeDtypeStruct((m, LANES), F32),
        jax.ShapeDtypeStruct((batch, GATE_ROWS, seq), F32),
    )
    out_specs = (
        tok(GDN_WIDTH), tok(GDN_WIDTH), tok(GDN_WIDTH), tok(GDN_WIDTH),
        tok(FOX_WIDTH), tok(FOX_WIDTH), tok(FOX_WIDTH), tok(LANES),
        pl.BlockSpec((1, GATE_ROWS, tm), lambda i: (i // tiles_per_seq, 0, i % tiles_per_seq)),
    )
    in_specs = [
        tok(d), _const_spec(nw.shape), _const_spec(wg.shape), _const_spec(wz.shape),
        _const_spec(wf.shape), _const_spec(ws.shape), _const_spec(conv_w.shape), _const_spec(prm.shape),
    ]
    return pl.pallas_call(
        functools.partial(_inproj_kernel, tm=tm, tiles_per_seq=tiles_per_seq),
        grid=grid, in_specs=in_specs, out_specs=out_specs, out_shape=out_shape,
        scratch_shapes=[pltpu.VMEM((tm + SUBLANES, 3 * GDN_WIDTH), F32), pltpu.VMEM((1, LANES), F32)],
        compiler_params=pltpu.CompilerParams(dimension_semantics=("arbitrary",), vmem_limit_bytes=VMEM_LIMIT),
        name="inproj",
    )(x2, nw, wg, wz, wf, ws, conv_w, prm)


def _gdn_kernel(q_ref, k_ref, v_ref, z_ref, gates_ref, gates_t_ref, nw_ref, o_ref, state_ref, *, tg):
    @pl.when(pl.program_id(1) == 0)
    def _():
        state_ref[...] = jnp.zeros_like(state_ref)

    ri = lax.broadcasted_iota(jnp.int32, (CHUNK, CHUNK), 0)
    ci = lax.broadcasted_iota(jnp.int32, (CHUNK, CHUNK), 1)
    lower = ri >= ci
    strict = ri > ci
    eye = (ri == ci).astype(F32)
    nw = nw_ref[...]

    states = [state_ref[h] for h in range(GDN_HEADS)]
    for c in range(tg // CHUNK):
        rows = slice(c * CHUNK, (c + 1) * CHUNK)
        gt = gates_ref[rows, :]
        gt_t = gates_t_ref[0, :, rows]
        for h in range(GDN_HEADS):
            cols = slice(h * GDN_HEAD_DIM, (h + 1) * GDN_HEAD_DIM)
            q = q_ref[rows, cols]
            k = k_ref[rows, cols]
            v = v_ref[rows, cols]
            gc = gt[:, GC_LANE + h:GC_LANE + h + 1]
            beta = gt[:, BETA_LANE + h:BETA_LANE + h + 1]
            gc_row = gt_t[GC_LANE + h:GC_LANE + h + 1, :]
            g_last = gc[CHUNK - 1:CHUNK, :]
            decay = jnp.where(lower, jnp.exp(jnp.where(lower, gc - gc_row, 0.0)), 0.0)
            e_gc = jnp.exp(gc)
            kb = k * beta
            vb = v * beta
            k16 = k.astype(BF16)
            a_mat = jnp.where(strict, _dot_nt(kb.astype(BF16), k16) * decay, 0.0)
            inv = eye - a_mat
            a_pow = a_mat
            for _ in range(5):
                a_pow = _dot_f32(a_pow, a_pow)
                inv = _dot_f32(inv, eye + a_pow)
            inv16 = inv.astype(BF16)
            u = _dot(inv16, vb.astype(BF16))
            w = _dot(inv16, (kb * e_gc).astype(BF16))
            qk = jnp.where(lower, _dot_nt(q.astype(BF16), k16) * decay, 0.0)

            s = states[h]
            s16 = s.astype(BF16)
            v_new = u - _dot(w.astype(BF16), s16)
            v_new16 = v_new.astype(BF16)
            o = _dot((q * e_gc).astype(BF16), s16) + _dot(qk.astype(BF16), v_new16)
            k_dec = k * jnp.exp(g_last - gc)
            states[h] = s * jnp.exp(g_last) + _dot_tn(k_dec.astype(BF16), v_new16)

            o = o * lax.rsqrt(jnp.mean(o * o, axis=-1, keepdims=True) + EPS) * nw
            zz = z_ref[rows, cols]
            o_ref[rows, cols] = (o * (zz * _sigmoid(zz))).astype(o_ref.dtype)

    for h in range(GDN_HEADS):
        state_ref[h] = states[h]


def _gdn(gq, gk, gv, gz, gates, gates_t, nw, *, batch, seq):
    m = gq.shape[0]
    tg = TG_GDN
    steps = seq // tg
    tok = lambda w: pl.BlockSpec((tg, w), lambda b, j: (b * steps + j, 0))
    in_specs = [
        tok(GDN_WIDTH), tok(GDN_WIDTH), tok(GDN_WIDTH), tok(GDN_WIDTH), tok(LANES),
        pl.BlockSpec((1, GATE_ROWS, tg), lambda b, j: (b, 0, j)),
        _const_spec(nw.shape),
    ]
    return pl.pallas_call(
        functools.partial(_gdn_kernel, tg=tg),
        grid=(batch, steps), in_specs=in_specs, out_specs=tok(GDN_WIDTH),
        out_shape=jax.ShapeDtypeStruct((m, GDN_WIDTH), BF16),
        scratch_shapes=[pltpu.VMEM((GDN_HEADS, GDN_HEAD_DIM, GDN_HEAD_DIM), F32)],
        compiler_params=pltpu.CompilerParams(
            dimension_semantics=("arbitrary", "arbitrary"), vmem_limit_bytes=VMEM_LIMIT),
        name="gdn",
    )(gq, gk, gv, gz, gates, gates_t, nw)


def _fox_kernel(q_ref, k_ref, v_ref, gates_ref, gates_t_ref, o_ref, *, tq):
    pair = pl.program_id(1)
    i = pl.program_id(2)
    heads_per_block = LANES // FOX_HEAD_DIM

    q = q_ref[...]
    gts = gates_ref[...]
    lane = lax.broadcasted_iota(jnp.int32, (tq, LANES), 1)
    qs, cqs, crow_idx = [], [], []
    for hh in range(heads_per_block):
        head = pair * heads_per_block + hh
        in_head = (lane >= hh * FOX_HEAD_DIM) & (lane < (hh + 1) * FOX_HEAD_DIM)
        qs.append(jnp.where(in_head, q, jnp.zeros_like(q)))
        cqs.append(jnp.sum(jnp.where(lane == CF_LANE + head, gts, 0.0), axis=-1, keepdims=True))
        crow_idx.append(CF_LANE + head)

    def tile(j, carry, masked):
        start = pl.multiple_of(j * tq, tq)
        kt = k_ref[pl.ds(start, tq), :]
        vt = v_ref[pl.ds(start, tq), :]
        new = []
        for hh in range(heads_per_block):
            m, l, acc = carry[hh]
            c_row = gates_t_ref[0, pl.ds(crow_idx[hh], 1), pl.ds(start, tq)]
            s = _dot_nt(qs[hh], kt) + (cqs[hh] - c_row)
            if masked:
                ri = lax.broadcasted_iota(jnp.int32, (tq, tq), 0)
                ci = lax.broadcasted_iota(jnp.int32, (tq, tq), 1)
                s = jnp.where(ri >= ci, s, NEG_BIG)
            m_new = jnp.maximum(m, jnp.max(s, axis=-1, keepdims=True))
            p = jnp.exp(s - m_new)
            alpha = jnp.exp(m - m_new)
            l = alpha * l + jnp.sum(p, axis=-1, keepdims=True)
            acc = alpha * acc + _dot(p.astype(BF16), vt)
            new.append((m_new, l, acc))
        return tuple(new)

    init = tuple((jnp.full((tq, 1), NEG_BIG, F32), jnp.zeros((tq, 1), F32), jnp.zeros((tq, LANES), F32))
                 for _ in range(heads_per_block))
    carry = lax.fori_loop(0, i, lambda j, c: tile(j, c, False), init)
    carry = tile(i, carry, True)

    out = jnp.zeros((tq, LANES), F32)
    for hh in range(heads_per_block):
        _, l, acc = carry[hh]
        in_head = (lane >= hh * FOX_HEAD_DIM) & (lane < (hh + 1) * FOX_HEAD_DIM)
        out = jnp.where(in_head, acc / l, out)
    o_ref[...] = out.astype(o_ref.dtype)


def _fox(fq, fk, fv, gates, gates_t, *, batch, seq):
    m = fq.shape[0]
    tq = TQ_FOX
    steps = seq // tq
    pairs = FOX_WIDTH // LANES
    q_spec = pl.BlockSpec((tq, LANES), lambda b, p, i: (b * steps + i, p))
    kv_spec = pl.BlockSpec((seq, LANES), lambda b, p, i: (b, p))
    in_specs = [
        q_spec, kv_spec, kv_spec,
        pl.BlockSpec((tq, LANES), lambda b, p, i: (b * steps + i, 0)),
        pl.BlockSpec((1, GATE_ROWS, seq), lambda b, p, i: (b, 0, 0)),
    ]
    return pl.pallas_call(
        functools.partial(_fox_kernel, tq=tq),
        grid=(batch, pairs, steps), in_specs=in_specs, out_specs=q_spec,
        out_shape=jax.ShapeDtypeStruct((m, FOX_WIDTH), BF16),
        compiler_params=pltpu.CompilerParams(
            dimension_semantics=("arbitrary", "arbitrary", "arbitrary"), vmem_limit_bytes=VMEM_LIMIT),
        name="fox",
    )(fq, fk, fv, gates, gates_t)


def _post_kernel(x_ref, og_ref, of_ref, p_ref, wo_g_ref, wo_f_ref, ffn_nw_ref, wgate_ref, wup_ref, wdown_ref,
                 ple_nw_ref, wpg_ref, wpp_ref, fin_nw_ref, out_ref, act_ref, *, tf):
    h = x_ref[...] + _dot(og_ref[...], wo_g_ref[...]) + _dot(of_ref[...], wo_f_ref[...])

    hn = (h * _rms_scale(h) * ffn_nw_ref[...]).astype(BF16)
    d_ff = wgate_ref.shape[1]
    for c in range(d_ff // tf):
        cols = slice(c * tf, (c + 1) * tf)
        gate = _dot(hn, wgate_ref[:, cols])
        up = _dot(hn, wup_ref[:, cols])
        act_ref[:, cols] = (gate * _sigmoid(gate) * up).astype(BF16)
    h = h + _dot(act_ref[...], wdown_ref[...])

    hn = (h * _rms_scale(h) * ple_nw_ref[...]).astype(BF16)
    ple_gate = _sigmoid(_dot(hn, wpg_ref[...]))
    h = h + ple_gate * _dot(p_ref[...].astype(BF16), wpp_ref[...])

    out_ref[...] = h * _rms_scale(h) * fin_nw_ref[...]


def _post(x2, o_g, o_f, p2, wo_g, wo_f, ffn_nw, wgate, wup, wdown, ple_nw, wpg, wpp, fin_nw):
    m, d = x2.shape
    tm = TM_POST
    tok = lambda w: pl.BlockSpec((tm, w), lambda i: (i, 0))
    weights = (wo_g, wo_f, ffn_nw, wgate, wup, wdown, ple_nw, wpg, wpp, fin_nw)
    in_specs = [tok(d), tok(GDN_WIDTH), tok(FOX_WIDTH), tok(p2.shape[1])] + [_const_spec(w.shape) for w in weights]
    return pl.pallas_call(
        functools.partial(_post_kernel, tf=TF_POST),
        grid=(m // tm,), in_specs=in_specs, out_specs=tok(d),
        out_shape=jax.ShapeDtypeStruct((m, d), F32),
        scratch_shapes=[pltpu.VMEM((tm, wgate.shape[1]), BF16)],
        compiler_params=pltpu.CompilerParams(dimension_semantics=("arbitrary",), vmem_limit_bytes=VMEM_LIMIT),
        name="post",
    )(x2, o_g, o_f, p2, *weights)


def _layer(x, p_i, attn_norm_w, w_in, conv_w, a_log, dt_bias, gdn_norm_w, fox_f_bias, w_out,
           ffn_norm_w, w_gate_up, w_down, ple_norm_w, w_ple_gate, w_ple_proj, out_norm_w):
    batch, seq, d = x.shape
    m = batch * seq
    assert seq % TM_IN == 0 and seq % TG_GDN == 0 and seq % TQ_FOX == 0 and m % TM_POST == 0
    row = lambda w: w.reshape(1, -1).astype(F32)

    c0 = 3 * GDN_WIDTH
    c1 = c0 + GDN_WIDTH
    c2 = c1 + 2 * GDN_HEADS
    c3 = c2 + 3 * FOX_WIDTH
    wg = w_in[:, :c0].astype(BF16)
    wz = w_in[:, c0:c1].astype(BF16)
    wf = w_in[:, c2:c3].astype(BF16)
    n_small = 2 * GDN_HEADS + FOX_HEADS
    ws = jnp.concatenate([w_in[:, c1:c2], w_in[:, c3:]], axis=1)
    ws = jnp.pad(ws, ((0, 0), (0, LANES - n_small))).astype(BF16)
    zeros4 = jnp.zeros((GDN_HEADS,), F32)
    pad = jnp.zeros((LANES - n_small,), F32)
    prm = jnp.stack([
        jnp.concatenate([dt_bias.astype(F32), zeros4, fox_f_bias.astype(F32), pad]),
        jnp.concatenate([a_log.astype(F32), zeros4, jnp.zeros((FOX_HEADS,), F32), pad]),
    ])
    prm = jnp.pad(prm, ((0, SUBLANES - prm.shape[0]), (0, 0)))

    x2 = x.reshape(m, d)
    gq, gk, gv, gz, fq, fk, fv, gates, gates_t = _inproj(
        x2, row(attn_norm_w), wg, wz, wf, ws, conv_w.astype(F32), prm, batch=batch, seq=seq)
    o_g = _gdn(gq, gk, gv, gz, gates, gates_t, row(gdn_norm_w), batch=batch, seq=seq)
    o_f = _fox(fq, fk, fv, gates, gates_t, batch=batch, seq=seq)

    d_ff = w_down.shape[0]
    return _post(
        x2, o_g, o_f, p_i.reshape(m, -1),
        w_out[:GDN_WIDTH].astype(BF16), w_out[GDN_WIDTH:].astype(BF16), row(ffn_norm_w),
        w_gate_up[:, :d_ff].astype(BF16), w_gate_up[:, d_ff:].astype(BF16), w_down.astype(BF16),
        row(ple_norm_w), w_ple_gate.astype(BF16), w_ple_proj.astype(BF16), row(out_norm_w),
    ).reshape(batch, seq, d)


def kernel(x, p, attn_norm_w, w_in, conv_w, a_log, dt_bias, gdn_norm_w, fox_f_bias, w_out, ffn_norm_w,
           w_gate_up, w_down, ple_norm_w, w_ple_gate, w_ple_proj, final_norm_w):
    assert p.shape[0] == 1, "single-layer problem"
    return _layer(x, p[0], attn_norm_w[0], w_in[0], conv_w[0], a_log[0], dt_bias[0], gdn_norm_w[0],
                  fox_f_bias[0], w_out[0], ffn_norm_w[0], w_gate_up[0], w_down[0], ple_norm_w[0],
                  w_ple_gate[0], w_ple_proj[0], final_norm_w)
```

```python
import functools

import jax
import jax.numpy as jnp
from jax import lax
from jax.experimental import pallas as pl
from jax.experimental.pallas import tpu as pltpu

F32 = jnp.float32
BF16 = jnp.bfloat16

EPS = 1e-6
GDN_HEADS = 4
GDN_HEAD_DIM = 128
GDN_WIDTH = GDN_HEADS * GDN_HEAD_DIM
FOX_HEADS = 8
FOX_HEAD_DIM = 64
FOX_WIDTH = FOX_HEADS * FOX_HEAD_DIM
CONV_WIDTH = 4
CHUNK = 64
PAIR = 2 * CHUNK
LANES = 128
SUBLANES = 8
NEG_BIG = -1e30

GC_LANE = 0
BETA_LANE = 4
CF_LANE = 8
GATE_ROWS = 16

VMEM_LIMIT = 56 * 1024 * 1024

TM_IN = 256
TP_GDN = 256
BG_GDN = 4
TQ_FOX = 256
TM_POST = 512
TF_POST = 256


def _dot(a, b):
    return jnp.dot(a, b, preferred_element_type=F32)


def _dot_nt(a, b):
    return lax.dot_general(a, b, (((1,), (1,)), ((), ())), preferred_element_type=F32)


def _dot_tn(a, b):
    return lax.dot_general(a, b, (((0,), (0,)), ((), ())), preferred_element_type=F32)


def _dot_inv(a, b):
    return _dot(a.astype(BF16), b.astype(BF16))


def _sigmoid(x):
    return 1.0 / (1.0 + jnp.exp(-x))


def _softplus(x):
    return jnp.maximum(x, 0.0) + jnp.log1p(jnp.exp(-jnp.abs(x)))


def _rms_scale(x):
    return lax.rsqrt(jnp.mean(x * x, axis=-1, keepdims=True) + EPS)


def _const_spec(shape):
    nd = len(shape)
    return pl.BlockSpec(shape, lambda *_: (0,) * nd, pipeline_mode=pl.Buffered(1))


def _inproj_kernel(x_ref, nw_ref, wg_ref, wz_ref, wf_ref, ws_ref, conv_ref, prm_ref,
                   gq_ref, gk_ref, gv_ref, gz_ref, fq_ref, fk_ref, fv_ref, gates_ref, gates_t_ref,
                   buf_ref, carry_ref, *, tm, tiles_per_seq):
    i = pl.program_id(0)
    seq_start = (i % tiles_per_seq) == 0
    halo = SUBLANES

    x = x_ref[...]
    xn = (x * _rms_scale(x) * nw_ref[...]).astype(BF16)

    @pl.when(seq_start)
    def _():
        buf_ref[0:halo, :] = jnp.zeros((halo, buf_ref.shape[1]), F32)
        carry_ref[...] = jnp.zeros_like(carry_ref)

    @pl.when(jnp.logical_not(seq_start))
    def _():
        buf_ref[0:halo, :] = buf_ref[tm:tm + halo, :]

    buf_ref[halo:halo + tm, :] = _dot(xn, wg_ref[...])

    out_refs = (gq_ref, gk_ref, gv_ref)
    for j in range(3 * GDN_HEADS):
        cols = slice(j * LANES, (j + 1) * LANES)
        acc = buf_ref[halo:halo + tm, cols] * conv_ref[CONV_WIDTH - 1:CONV_WIDTH, cols]
        for t in range(CONV_WIDTH - 1):
            off = halo - (CONV_WIDTH - 1) + t
            acc = acc + buf_ref[off:off + tm, cols] * conv_ref[t:t + 1, cols]
        y = acc * _sigmoid(acc)
        kind, head = divmod(j, GDN_HEADS)
        if kind < 2:
            y = y * lax.rsqrt(jnp.sum(y * y, axis=-1, keepdims=True) + EPS)
        if kind == 0:
            y = y * (GDN_HEAD_DIM ** -0.5)
        out_refs[kind][:, head * LANES:(head + 1) * LANES] = y

    gz_ref[...] = _dot(xn, wz_ref[...])

    f_qkv = _dot(xn, wf_ref[...])
    fq_ref[...] = (f_qkv[:, 0:FOX_WIDTH] * (FOX_HEAD_DIM ** -0.5)).astype(BF16)
    fk_ref[...] = f_qkv[:, FOX_WIDTH:2 * FOX_WIDTH].astype(BF16)
    fv_ref[...] = f_qkv[:, 2 * FOX_WIDTH:3 * FOX_WIDTH].astype(BF16)

    z = _dot(xn, ws_ref[...]) + prm_ref[0:1, :]
    lane = lax.broadcasted_iota(jnp.int32, (tm, LANES), 1)
    row = lax.broadcasted_iota(jnp.int32, (tm, LANES), 0)
    is_gc = lane < BETA_LANE
    is_cf = (lane >= CF_LANE) & (lane < CF_LANE + FOX_HEADS)
    soft = jnp.log1p(jnp.exp(-jnp.abs(z)))
    g_decay = -jnp.exp(prm_ref[1:2, :]) * (jnp.maximum(z, 0.0) + soft)
    log_f = -(jnp.maximum(-z, 0.0) + soft)
    val = jnp.where(is_gc, g_decay, jnp.where(lane < CF_LANE, _sigmoid(z), jnp.where(is_cf, log_f, 0.0)))

    pos = jnp.where(is_gc, row % CHUNK, row)
    summed = is_gc | is_cf
    d = 1
    while d < tm:
        shifted = pltpu.roll(val, d, axis=0)
        val = val + jnp.where(summed & (pos >= d), shifted, 0.0)
        d *= 2
    val = val + jnp.where(is_cf, carry_ref[...], 0.0)
    carry_ref[...] = val[tm - 1:tm, :]

    gates_ref[...] = val
    gates_t_ref[0] = val.T[0:GATE_ROWS, :]


def _inproj(x2, nw, wg, wz, wf, ws, conv_w, prm, *, batch, seq):
    m, d = x2.shape
    tm = TM_IN
    tiles_per_seq = seq // tm
    grid = (m // tm,)
    tok = lambda w: pl.BlockSpec((tm, w), lambda i: (i, 0))
    out_shape = (
        jax.ShapeDtypeStruct((m, GDN_WIDTH), F32),
        jax.ShapeDtypeStruct((m, GDN_WIDTH), F32),
        jax.ShapeDtypeStruct((m, GDN_WIDTH), F32),
        jax.ShapeDtypeStruct((m, GDN_WIDTH), F32),
        jax.ShapeDtypeStruct((m, FOX_WIDTH), BF16),
        jax.ShapeDtypeStruct((m, FOX_WIDTH), BF16),
        jax.ShapeDtypeStruct((m, FOX_WIDTH), BF16),
        jax.ShapeDtypeStruct((m, LANES), F32),
        jax.ShapeDtypeStruct((batch, GATE_ROWS, seq), F32),
    )
    out_specs = (
        tok(GDN_WIDTH), tok(GDN_WIDTH), tok(GDN_WIDTH), tok(GDN_WIDTH),
        tok(FOX_WIDTH), tok(FOX_WIDTH), tok(FOX_WIDTH), tok(LANES),
        pl.BlockSpec((1, GATE_ROWS, tm), lambda i: (i // tiles_per_seq, 0, i % tiles_per_seq)),
    )
    in_specs = [
        tok(d), _const_spec(nw.shape), _const_spec(wg.shape), _const_spec(wz.shape),
        _const_spec(wf.shape), _const_spec(ws.shape), _const_spec(conv_w.shape), _const_spec(prm.shape),
    ]
    return pl.pallas_call(
        functools.partial(_inproj_kernel, tm=tm, tiles_per_seq=tiles_per_seq),
        grid=grid, in_specs=in_specs, out_specs=out_specs, out_shape=out_shape,
        scratch_shapes=[pltpu.VMEM((tm + SUBLANES, 3 * GDN_WIDTH), F32), pltpu.VMEM((1, LANES), F32)],
        compiler_params=pltpu.CompilerParams(dimension_semantics=("arbitrary",), vmem_limit_bytes=VMEM_LIMIT),
        name="inproj",
    )(x2, nw, wg, wz, wf, ws, conv_w, prm)


def _gdn_prep_kernel(q_ref, k_ref, v_ref, gates_ref, gates_t_ref, u_ref, wq_ref, kdt_ref, qk_ref, *, tp):
    ri = lax.broadcasted_iota(jnp.int32, (PAIR, PAIR), 0)
    ci = lax.broadcasted_iota(jnp.int32, (PAIR, PAIR), 1)
    same_chunk = (ri // CHUNK) == (ci // CHUNK)
    lower = same_chunk & (ri >= ci)
    strict = same_chunk & (ri > ci)
    eye = (ri == ci).astype(F32)
    first_chunk_row = lax.broadcasted_iota(jnp.int32, (PAIR, LANES), 0) < CHUNK

    chains = [(pr, h) for pr in range(tp // PAIR) for h in range(GDN_HEADS)]
    rows_of = lambda pr: slice(pr * PAIR, (pr + 1) * PAIR)
    cols_of = lambda h: slice(h * GDN_HEAD_DIM, (h + 1) * GDN_HEAD_DIM)

    gts = [gates_ref[rows_of(pr), :] for pr in range(tp // PAIR)]
    gt_ts = [gates_t_ref[0, :, rows_of(pr)] for pr in range(tp // PAIR)]
    k16s, a_mats, decays, e_gcs = [], [], [], []
    for pr, h in chains:
        k = k_ref[rows_of(pr), cols_of(h)]
        gc = gts[pr][:, GC_LANE + h:GC_LANE + h + 1]
        beta = gts[pr][:, BETA_LANE + h:BETA_LANE + h + 1]
        gc_row = gt_ts[pr][GC_LANE + h:GC_LANE + h + 1, :]
        decay = jnp.where(lower, jnp.exp(jnp.where(lower, gc - gc_row, 0.0)), 0.0)
        k16 = k.astype(BF16)
        a_mats.append(jnp.where(strict, _dot_nt((k * beta).astype(BF16), k16) * decay, 0.0))
        k16s.append(k16)
        decays.append(decay)
        e_gcs.append(jnp.exp(gc))

    invs = [eye - a for a in a_mats]
    pows = a_mats
    for _ in range(5):
        pows = [_dot_inv(a, a) for a in pows]
        invs = [_dot_inv(t, eye + a) for t, a in zip(invs, pows)]

    for n, (pr, h) in enumerate(chains):
        rows, cols = rows_of(pr), cols_of(h)
        gc = gts[pr][:, GC_LANE + h:GC_LANE + h + 1]
        beta = gts[pr][:, BETA_LANE + h:BETA_LANE + h + 1]
        q = q_ref[rows, cols]
        k = k_ref[rows, cols]
        inv16 = invs[n].astype(BF16)
        u_ref[rows, cols] = _dot(inv16, (v_ref[rows, cols] * beta).astype(BF16))
        w = _dot(inv16, ((k * beta) * e_gcs[n]).astype(BF16)).astype(BF16)
        qe = (q * e_gcs[n]).astype(BF16)
        qk = jnp.where(lower, _dot_nt(q.astype(BF16), k16s[n]) * decays[n], 0.0).astype(BF16)
        g_last = jnp.where(first_chunk_row, gts[pr][CHUNK - 1:CHUNK, :], gts[pr][PAIR - 1:PAIR, :])
        kd = k * jnp.exp(g_last[:, GC_LANE + h:GC_LANE + h + 1] - gc)
        kdt_ref[0, h, :, rows] = kd.T.astype(BF16)
        for c in range(PAIR // CHUNK):
            sub = slice(c * CHUNK, (c + 1) * CHUNK)
            r0 = 2 * (pr * PAIR + c * CHUNK)
            wq_ref[r0:r0 + CHUNK, cols] = w[sub]
            wq_ref[r0 + CHUNK:r0 + 2 * CHUNK, cols] = qe[sub]
            qk_ref[pr * PAIR + c * CHUNK:pr * PAIR + (c + 1) * CHUNK, h * CHUNK:(h + 1) * CHUNK] = qk[sub, sub]


def _gdn_scan_kernel(u_ref, wq_ref, kdt_ref, qk_ref, z_ref, gates_ref, nw_ref, o_ref, state_ref, *, bg):
    @pl.when(pl.program_id(1) == 0)
    def _():
        state_ref[...] = jnp.zeros_like(state_ref)

    nw = nw_ref[...]
    chains = [(b, h) for b in range(bg) for h in range(GDN_HEADS)]
    cols_of = lambda h: slice(h * GDN_HEAD_DIM, (h + 1) * GDN_HEAD_DIM)
    states = [state_ref[b * GDN_HEADS + h] for b, h in chains]
    for c in range(PAIR // CHUNK):
        rows = slice(c * CHUNK, (c + 1) * CHUNK)
        last = (c + 1) * CHUNK - 1
        rs = [_dot(wq_ref[b, 2 * c * CHUNK:2 * (c + 1) * CHUNK, cols_of(h)], states[n].astype(BF16))
              for n, (b, h) in enumerate(chains)]
        v_news = [(u_ref[b, rows, cols_of(h)] - rs[n][0:CHUNK]).astype(BF16) for n, (b, h) in enumerate(chains)]
        upds = [_dot(kdt_ref[b, h, :, rows], v_news[n]) for n, (b, h) in enumerate(chains)]
        states = [states[n] * jnp.exp(gates_ref[b, last:last + 1, GC_LANE + h:GC_LANE + h + 1]) + upds[n]
                  for n, (b, h) in enumerate(chains)]
        for n, (b, h) in enumerate(chains):
            o = rs[n][CHUNK:2 * CHUNK] + _dot(qk_ref[b, rows, h * CHUNK:(h + 1) * CHUNK], v_news[n])
            o = o * lax.rsqrt(jnp.mean(o * o, axis=-1, keepdims=True) + EPS) * nw
            zz = z_ref[b, rows, cols_of(h)]
            o_ref[b, rows, cols_of(h)] = (o * (zz * _sigmoid(zz))).astype(o_ref.dtype)
    for n, (b, h) in enumerate(chains):
        state_ref[b * GDN_HEADS + h] = states[n]


def _gdn(gq, gk, gv, gz, gates, gates_t, nw, *, batch, seq):
    m = gq.shape[0]
    tp = TP_GDN
    steps = seq // tp
    tok = lambda w: pl.BlockSpec((tp, w), lambda b, j: (b * steps + j, 0))
    u, wq, kdt, qk = pl.pallas_call(
        functools.partial(_gdn_prep_kernel, tp=tp),
        grid=(batch, steps),
        in_specs=[tok(GDN_WIDTH), tok(GDN_WIDTH), tok(GDN_WIDTH), tok(LANES),
                  pl.BlockSpec((1, GATE_ROWS, tp), lambda b, j: (b, 0, j))],
        out_specs=(tok(GDN_WIDTH),
                   pl.BlockSpec((2 * tp, GDN_WIDTH), lambda b, j: (b * steps + j, 0)),
                   pl.BlockSpec((1, GDN_HEADS, GDN_HEAD_DIM, tp), lambda b, j: (b, 0, 0, j)),
                   tok(GDN_HEADS * CHUNK)),
        out_shape=(jax.ShapeDtypeStruct((m, GDN_WIDTH), F32),
                   jax.ShapeDtypeStruct((2 * m, GDN_WIDTH), BF16),
                   jax.ShapeDtypeStruct((batch, GDN_HEADS, GDN_HEAD_DIM, seq), BF16),
                   jax.ShapeDtypeStruct((m, GDN_HEADS * CHUNK), BF16)),
        compiler_params=pltpu.CompilerParams(
            dimension_semantics=("arbitrary", "arbitrary"), vmem_limit_bytes=VMEM_LIMIT),
        name="gdn_prep",
    )(gq, gk, gv, gates, gates_t)

    bg = BG_GDN
    seq_blk = lambda rows, w: pl.BlockSpec((bg, rows, w), lambda bi, j: (bi, j, 0))
    o = pl.pallas_call(
        functools.partial(_gdn_scan_kernel, bg=bg),
        grid=(batch // bg, seq // PAIR),
        in_specs=[seq_blk(PAIR, GDN_WIDTH), seq_blk(2 * PAIR, GDN_WIDTH),
                  pl.BlockSpec((bg, GDN_HEADS, GDN_HEAD_DIM, PAIR), lambda bi, j: (bi, 0, 0, j)),
                  seq_blk(PAIR, GDN_HEADS * CHUNK), seq_blk(PAIR, GDN_WIDTH), seq_blk(PAIR, LANES),
                  _const_spec(nw.shape)],
        out_specs=seq_blk(PAIR, GDN_WIDTH),
        out_shape=jax.ShapeDtypeStruct((batch, seq, GDN_WIDTH), BF16),
        scratch_shapes=[pltpu.VMEM((bg * GDN_HEADS, GDN_HEAD_DIM, GDN_HEAD_DIM), F32)],
        compiler_params=pltpu.CompilerParams(
            dimension_semantics=("arbitrary", "arbitrary"), vmem_limit_bytes=VMEM_LIMIT),
        name="gdn_scan",
    )(u.reshape(batch, seq, GDN_WIDTH), wq.reshape(batch, 2 * seq, GDN_WIDTH), kdt,
      qk.reshape(batch, seq, GDN_HEADS * CHUNK), gz.reshape(batch, seq, GDN_WIDTH),
      gates.reshape(batch, seq, LANES), nw)
    return o.reshape(m, GDN_WIDTH)


def _fox_kernel(q_ref, k_ref, v_ref, gates_ref, gates_t_ref, o_ref, *, tq):
    pair = pl.program_id(1)
    i = pl.program_id(2)
    heads_per_block = LANES // FOX_HEAD_DIM

    q = q_ref[...]
    gts = gates_ref[...]
    lane = lax.broadcasted_iota(jnp.int32, (tq, LANES), 1)
    qs, cqs, crow_idx = [], [], []
    for hh in range(heads_per_block):
        head = pair * heads_per_block + hh
        in_head = (lane >= hh * FOX_HEAD_DIM) & (lane < (hh + 1) * FOX_HEAD_DIM)
        qs.append(jnp.where(in_head, q, jnp.zeros_like(q)))
        cqs.append(jnp.sum(jnp.where(lane == CF_LANE + head, gts, 0.0), axis=-1, keepdims=True))
        crow_idx.append(CF_LANE + head)

    def tile(j, carry, masked):
        start = pl.multiple_of(j * tq, tq)
        kt = k_ref[pl.ds(start, tq), :]
        vt = v_ref[pl.ds(start, tq), :]
        new = []
        for hh in range(heads_per_block):
            m, l, acc = carry[hh]
            c_row = gates_t_ref[0, pl.ds(crow_idx[hh], 1), pl.ds(start, tq)]
            s = _dot_nt(qs[hh], kt) + (cqs[hh] - c_row)
            if masked:
                ri = lax.broadcasted_iota(jnp.int32, (tq, tq), 0)
                ci = lax.broadcasted_iota(jnp.int32, (tq, tq), 1)
                s = jnp.where(ri >= ci, s, NEG_BIG)
            m_new = jnp.maximum(m, jnp.max(s, axis=-1, keepdims=True))
            p = jnp.exp(s - m_new)
            alpha = jnp.exp(m - m_new)
            l = alpha * l + jnp.sum(p, axis=-1, keepdims=True)
            acc = alpha * acc + _dot(p.astype(BF16), vt)
            new.append((m_new, l, acc))
        return tuple(new)

    init = tuple((jnp.full((tq, 1), NEG_BIG, F32), jnp.zeros((tq, 1), F32), jnp.zeros((tq, LANES), F32))
                 for _ in range(heads_per_block))
    carry = lax.fori_loop(0, i, lambda j, c: tile(j, c, False), init)
    carry = tile(i, carry, True)

    out = jnp.zeros((tq, LANES), F32)
    for hh in range(heads_per_block):
        _, l, acc = carry[hh]
        in_head = (lane >= hh * FOX_HEAD_DIM) & (lane < (hh + 1) * FOX_HEAD_DIM)
        out = jnp.where(in_head, acc / l, out)
    o_ref[...] = out.astype(o_ref.dtype)


def _fox(fq, fk, fv, gates, gates_t, *, batch, seq):
    m = fq.shape[0]
    tq = TQ_FOX
    steps = seq // tq
    pairs = FOX_WIDTH // LANES
    q_spec = pl.BlockSpec((tq, LANES), lambda b, p, i: (b * steps + i, p))
    kv_spec = pl.BlockSpec((seq, LANES), lambda b, p, i: (b, p))
    in_specs = [
        q_spec, kv_spec, kv_spec,
        pl.BlockSpec((tq, LANES), lambda b, p, i: (b * steps + i, 0)),
        pl.BlockSpec((1, GATE_ROWS, seq), lambda b, p, i: (b, 0, 0)),
    ]
    return pl.pallas_call(
        functools.partial(_fox_kernel, tq=tq),
        grid=(batch, pairs, steps), in_specs=in_specs, out_specs=q_spec,
        out_shape=jax.ShapeDtypeStruct((m, FOX_WIDTH), BF16),
        compiler_params=pltpu.CompilerParams(
            dimension_semantics=("arbitrary", "arbitrary", "arbitrary"), vmem_limit_bytes=VMEM_LIMIT),
        name="fox",
    )(fq, fk, fv, gates, gates_t)


def _post_kernel(x_ref, og_ref, of_ref, p_ref, wo_g_ref, wo_f_ref, ffn_nw_ref, wgate_ref, wup_ref, wdown_ref,
                 ple_nw_ref, wpg_ref, wpp_ref, fin_nw_ref, out_ref, act_ref, *, tf):
    h = x_ref[...] + _dot(og_ref[...], wo_g_ref[...]) + _dot(of_ref[...], wo_f_ref[...])

    hn = (h * _rms_scale(h) * ffn_nw_ref[...]).astype(BF16)
    d_ff = wgate_ref.shape[1]
    for c in range(d_ff // tf):
        cols = slice(c * tf, (c + 1) * tf)
        gate = _dot(hn, wgate_ref[:, cols])
        up = _dot(hn, wup_ref[:, cols])
        act_ref[:, cols] = (gate * _sigmoid(gate) * up).astype(BF16)
    h = h + _dot(act_ref[...], wdown_ref[...])

    hn = (h * _rms_scale(h) * ple_nw_ref[...]).astype(BF16)
    ple_gate = _sigmoid(_dot(hn, wpg_ref[...]))
    h = h + ple_gate * _dot(p_ref[...].astype(BF16), wpp_ref[...])

    out_ref[...] = h * _rms_scale(h) * fin_nw_ref[...]


def _post(x2, o_g, o_f, p2, wo_g, wo_f, ffn_nw, wgate, wup, wdown, ple_nw, wpg, wpp, fin_nw):
    m, d = x2.shape
    tm = TM_POST
    tok = lambda w: pl.BlockSpec((tm, w), lambda i: (i, 0))
    weights = (wo_g, wo_f, ffn_nw, wgate, wup, wdown, ple_nw, wpg, wpp, fin_nw)
    in_specs = [tok(d), tok(GDN_WIDTH), tok(FOX_WIDTH), tok(p2.shape[1])] + [_const_spec(w.shape) for w in weights]
    return pl.pallas_call(
        functools.partial(_post_kernel, tf=TF_POST),
        grid=(m // tm,), in_specs=in_specs, out_specs=tok(d),
        out_shape=jax.ShapeDtypeStruct((m, d), F32),
        scratch_shapes=[pltpu.VMEM((tm, wgate.shape[1]), BF16)],
        compiler_params=pltpu.CompilerParams(dimension_semantics=("arbitrary",), vmem_limit_bytes=VMEM_LIMIT),
        name="post",
    )(x2, o_g, o_f, p2, *weights)


def _layer(x, p_i, attn_norm_w, w_in, conv_w, a_log, dt_bias, gdn_norm_w, fox_f_bias, w_out,
           ffn_norm_w, w_gate_up, w_down, ple_norm_w, w_ple_gate, w_ple_proj, out_norm_w):
    batch, seq, d = x.shape
    m = batch * seq
    assert seq % TM_IN == 0 and seq % TP_GDN == 0 and seq % TQ_FOX == 0 and m % TM_POST == 0
    assert batch % BG_GDN == 0
    row = lambda w: w.reshape(1, -1).astype(F32)

    c0 = 3 * GDN_WIDTH
    c1 = c0 + GDN_WIDTH
    c2 = c1 + 2 * GDN_HEADS
    c3 = c2 + 3 * FOX_WIDTH
    wg = w_in[:, :c0].astype(BF16)
    wz = w_in[:, c0:c1].astype(BF16)
    wf = w_in[:, c2:c3].astype(BF16)
    n_small = 2 * GDN_HEADS + FOX_HEADS
    ws = jnp.concatenate([w_in[:, c1:c2], w_in[:, c3:]], axis=1)
    ws = jnp.pad(ws, ((0, 0), (0, LANES - n_small))).astype(BF16)
    zeros4 = jnp.zeros((GDN_HEADS,), F32)
    pad = jnp.zeros((LANES - n_small,), F32)
    prm = jnp.stack([
        jnp.concatenate([dt_bias.astype(F32), zeros4, fox_f_bias.astype(F32), pad]),
        jnp.concatenate([a_log.astype(F32), zeros4, jnp.zeros((FOX_HEADS,), F32), pad]),
    ])
    prm = jnp.pad(prm, ((0, SUBLANES - prm.shape[0]), (0, 0)))

    x2 = x.reshape(m, d)
    gq, gk, gv, gz, fq, fk, fv, gates, gates_t = _inproj(
        x2, row(attn_norm_w), wg, wz, wf, ws, conv_w.astype(F32), prm, batch=batch, seq=seq)
    o_g = _gdn(gq, gk, gv, gz, gates, gates_t, row(gdn_norm_w), batch=batch, seq=seq)
    o_f = _fox(fq, fk, fv, gates, gates_t, batch=batch, seq=seq)

    d_ff = w_down.shape[0]
    return _post(
        x2, o_g, o_f, p_i.reshape(m, -1),
        w_out[:GDN_WIDTH].astype(BF16), w_out[GDN_WIDTH:].astype(BF16), row(ffn_norm_w),
        w_gate_up[:, :d_ff].astype(BF16), w_gate_up[:, d_ff:].astype(BF16), w_down.astype(BF16),
        row(ple_norm_w), w_ple_gate.astype(BF16), w_ple_proj.astype(BF16), row(out_norm_w),
    ).reshape(batch, seq, d)


def kernel(x, p, attn_norm_w, w_in, conv_w, a_log, dt_bias, gdn_norm_w, fox_f_bias, w_out, ffn_norm_w,
           w_gate_up, w_down, ple_norm_w, w_ple_gate, w_ple_proj, final_norm_w):
    assert p.shape[0] == 1, "single-layer problem"
    return _layer(x, p[0], attn_norm_w[0], w_in[0], conv_w[0], a_log[0], dt_bias[0], gdn_norm_w[0],
                  fox_f_bias[0], w_out[0], ffn_norm_w[0], w_gate_up[0], w_down[0], ple_norm_w[0],
                  w_ple_gate[0], w_ple_proj[0], final_norm_w)
```

```python
import functools

import jax
import jax.numpy as jnp
from jax import lax
from jax.experimental import pallas as pl
from jax.experimental.pallas import tpu as pltpu

F32 = jnp.float32
BF16 = jnp.bfloat16

EPS = 1e-6
GDN_HEADS = 4
GDN_HEAD_DIM = 128
GDN_WIDTH = GDN_HEADS * GDN_HEAD_DIM
FOX_HEADS = 8
FOX_HEAD_DIM = 64
FOX_WIDTH = FOX_HEADS * FOX_HEAD_DIM
CONV_WIDTH = 4
CHUNK = 64
PAIR = 2 * CHUNK
LANES = 128
SUBLANES = 8
NEG_BIG = -1e30
LOG2E = 1.4426950408889634

GC_LANE = 0
BETA_LANE = 4
CF_LANE = 8
GATE_ROWS = 16

VMEM_LIMIT = 56 * 1024 * 1024

TM_IN = 256
TP_GDN = 256
BG_GDN = 4
TQ_FOX = 256
HEADS_FOX = 8
TM_POST = 512
TF_POST = 256


def _dot(a, b):
    return jnp.dot(a, b, preferred_element_type=F32)


def _dot_nt(a, b):
    return lax.dot_general(a, b, (((1,), (1,)), ((), ())), preferred_element_type=F32)


def _dot_tn(a, b):
    return lax.dot_general(a, b, (((0,), (0,)), ((), ())), preferred_element_type=F32)


def _dot_inv(a, b):
    return _dot(a.astype(BF16), b.astype(BF16))


def _sigmoid(x):
    return 1.0 / (1.0 + jnp.exp(-x))


def _softplus(x):
    return jnp.maximum(x, 0.0) + jnp.log1p(jnp.exp(-jnp.abs(x)))


def _rms_scale(x):
    return lax.rsqrt(jnp.mean(x * x, axis=-1, keepdims=True) + EPS)


def _const_spec(shape):
    nd = len(shape)
    return pl.BlockSpec(shape, lambda *_: (0,) * nd, pipeline_mode=pl.Buffered(1))


def _inproj_kernel(x_ref, nw_ref, wg_ref, wz_ref, wf_ref, wfvt_ref, ws_ref, conv_ref, prm_ref,
                   gq_ref, gk_ref, gv_ref, gz_ref, fqa_ref, fka_ref, fvt_ref, gates_ref, gates_t_ref,
                   buf_ref, carry_ref, *, tm, tiles_per_seq):
    i = pl.program_id(0)
    seq_start = (i % tiles_per_seq) == 0
    halo = SUBLANES

    x = x_ref[...]
    xn = (x * _rms_scale(x) * nw_ref[...]).astype(BF16)

    @pl.when(seq_start)
    def _():
        buf_ref[0:halo, :] = jnp.zeros((halo, buf_ref.shape[1]), F32)
        carry_ref[...] = jnp.zeros_like(carry_ref)

    @pl.when(jnp.logical_not(seq_start))
    def _():
        buf_ref[0:halo, :] = buf_ref[tm:tm + halo, :]

    buf_ref[halo:halo + tm, :] = _dot(xn, wg_ref[...])

    out_refs = (gq_ref, gk_ref, gv_ref)
    for j in range(3 * GDN_HEADS):
        cols = slice(j * LANES, (j + 1) * LANES)
        acc = buf_ref[halo:halo + tm, cols] * conv_ref[CONV_WIDTH - 1:CONV_WIDTH, cols]
        for t in range(CONV_WIDTH - 1):
            off = halo - (CONV_WIDTH - 1) + t
            acc = acc + buf_ref[off:off + tm, cols] * conv_ref[t:t + 1, cols]
        y = acc * _sigmoid(acc)
        kind, head = divmod(j, GDN_HEADS)
        if kind < 2:
            y = y * lax.rsqrt(jnp.sum(y * y, axis=-1, keepdims=True) + EPS)
        if kind == 0:
            y = y * (GDN_HEAD_DIM ** -0.5)
        out_refs[kind][:, head * LANES:(head + 1) * LANES] = y

    gz_ref[...] = _dot(xn, wz_ref[...])

    fvt_ref[0] = _dot_nt(wfvt_ref[...], xn).astype(BF16)

    z = _dot(xn, ws_ref[...]) + prm_ref[0:1, :]
    lane = lax.broadcasted_iota(jnp.int32, (tm, LANES), 1)
    row = lax.broadcasted_iota(jnp.int32, (tm, LANES), 0)
    is_gc = lane < BETA_LANE
    is_cf = (lane >= CF_LANE) & (lane < CF_LANE + FOX_HEADS)
    soft = jnp.log1p(jnp.exp(-jnp.abs(z)))
    g_decay = -jnp.exp(prm_ref[1:2, :]) * (jnp.maximum(z, 0.0) + soft)
    log_f = -(jnp.maximum(-z, 0.0) + soft)
    val = jnp.where(is_gc, g_decay, jnp.where(lane < CF_LANE, _sigmoid(z), jnp.where(is_cf, log_f, 0.0)))

    pos = jnp.where(is_gc, row % CHUNK, row)
    summed = is_gc | is_cf
    d = 1
    while d < tm:
        shifted = pltpu.roll(val, d, axis=0)
        val = val + jnp.where(summed & (pos >= d), shifted, 0.0)
        d *= 2
    val = val + jnp.where(is_cf, carry_ref[...], 0.0)
    carry_ref[...] = val[tm - 1:tm, :]

    gates_ref[...] = val
    gates_t_ref[0] = val.T[0:GATE_ROWS, :]

    f_qk = _dot(xn, wf_ref[...])
    c2 = val * LOG2E
    c_hi = c2.astype(BF16).astype(F32)
    c_mid = (c2 - c_hi).astype(BF16).astype(F32)
    c_lo = c2 - c_hi - c_mid
    for h in range(FOX_HEADS):
        src, half = divmod(h, 2)
        in_half = (lane >= half * FOX_HEAD_DIM) & (lane < (half + 1) * FOX_HEAD_DIM)
        base = (1 - half) * FOX_HEAD_DIM
        col = slice(CF_LANE + h, CF_LANE + h + 1)
        hi = jnp.broadcast_to(c_hi[:, col], (tm, LANES))
        mid = jnp.broadcast_to(c_mid[:, col], (tm, LANES))
        lo = jnp.broadcast_to(c_lo[:, col], (tm, LANES))
        q_bias = jnp.where(lane == base, hi, jnp.where(lane == base + 1, mid, jnp.where(lane == base + 2, lo,
                 jnp.where((lane >= base + 3) & (lane < base + 6), -1.0, 0.0))))
        k_bias = jnp.where(lane == base + 3, hi, jnp.where(lane == base + 4, mid, jnp.where(lane == base + 5, lo,
                 jnp.where((lane >= base) & (lane < base + 3), 1.0, 0.0))))
        q_slab = f_qk[:, src * LANES:(src + 1) * LANES] * (FOX_HEAD_DIM ** -0.5 * LOG2E)
        k_slab = f_qk[:, FOX_WIDTH + src * LANES:FOX_WIDTH + (src + 1) * LANES]
        fqa_ref[:, h * LANES:(h + 1) * LANES] = jnp.where(in_half, q_slab, q_bias).astype(BF16)
        fka_ref[:, h * LANES:(h + 1) * LANES] = jnp.where(in_half, k_slab, k_bias).astype(BF16)


def _inproj(x2, nw, wg, wz, wf, wfvt, ws, conv_w, prm, *, batch, seq):
    m, d = x2.shape
    tm = TM_IN
    tiles_per_seq = seq // tm
    grid = (m // tm,)
    tok = lambda w: pl.BlockSpec((tm, w), lambda i: (i, 0))
    seq_t = lambda rows: pl.BlockSpec((1, rows, tm), lambda i: (i // tiles_per_seq, 0, i % tiles_per_seq))
    out_shape = (
        jax.ShapeDtypeStruct((m, GDN_WIDTH), F32),
        jax.ShapeDtypeStruct((m, GDN_WIDTH), F32),
        jax.ShapeDtypeStruct((m, GDN_WIDTH), F32),
        jax.ShapeDtypeStruct((m, GDN_WIDTH), F32),
        jax.ShapeDtypeStruct((m, FOX_HEADS * LANES), BF16),
        jax.ShapeDtypeStruct((m, FOX_HEADS * LANES), BF16),
        jax.ShapeDtypeStruct((batch, FOX_WIDTH, seq), BF16),
        jax.ShapeDtypeStruct((m, LANES), F32),
        jax.ShapeDtypeStruct((batch, GATE_ROWS, seq), F32),
    )
    out_specs = (
        tok(GDN_WIDTH), tok(GDN_WIDTH), tok(GDN_WIDTH), tok(GDN_WIDTH),
        tok(FOX_HEADS * LANES), tok(FOX_HEADS * LANES), seq_t(FOX_WIDTH), tok(LANES), seq_t(GATE_ROWS),
    )
    in_specs = [
        tok(d), _const_spec(nw.shape), _const_spec(wg.shape), _const_spec(wz.shape), _const_spec(wf.shape),
        _const_spec(wfvt.shape), _const_spec(ws.shape), _const_spec(conv_w.shape), _const_spec(prm.shape),
    ]
    return pl.pallas_call(
        functools.partial(_inproj_kernel, tm=tm, tiles_per_seq=tiles_per_seq),
        grid=grid, in_specs=in_specs, out_specs=out_specs, out_shape=out_shape,
        scratch_shapes=[pltpu.VMEM((tm + SUBLANES, 3 * GDN_WIDTH), F32), pltpu.VMEM((1, LANES), F32)],
        compiler_params=pltpu.CompilerParams(dimension_semantics=("arbitrary",), vmem_limit_bytes=VMEM_LIMIT),
        name="inproj",
    )(x2, nw, wg, wz, wf, wfvt, ws, conv_w, prm)


def _gdn_prep_kernel(q_ref, k_ref, v_ref, gates_ref, gates_t_ref, u_ref, wq_ref, kdt_ref, qk_ref, *, tp):
    ri = lax.broadcasted_iota(jnp.int32, (PAIR, PAIR), 0)
    ci = lax.broadcasted_iota(jnp.int32, (PAIR, PAIR), 1)
    same_chunk = (ri // CHUNK) == (ci // CHUNK)
    lower = same_chunk & (ri >= ci)
    strict = same_chunk & (ri > ci)
    eye = (ri == ci).astype(F32)
    first_chunk_row = lax.broadcasted_iota(jnp.int32, (PAIR, LANES), 0) < CHUNK

    chains = [(pr, h) for pr in range(tp // PAIR) for h in range(GDN_HEADS)]
    rows_of = lambda pr: slice(pr * PAIR, (pr + 1) * PAIR)
    cols_of = lambda h: slice(h * GDN_HEAD_DIM, (h + 1) * GDN_HEAD_DIM)

    gts = [gates_ref[rows_of(pr), :] for pr in range(tp // PAIR)]
    gt_ts = [gates_t_ref[0, :, rows_of(pr)] for pr in range(tp // PAIR)]
    k16s, a_mats, decays, e_gcs = [], [], [], []
    for pr, h in chains:
        k = k_ref[rows_of(pr), cols_of(h)]
        gc = gts[pr][:, GC_LANE + h:GC_LANE + h + 1]
        beta = gts[pr][:, BETA_LANE + h:BETA_LANE + h + 1]
        gc_row = gt_ts[pr][GC_LANE + h:GC_LANE + h + 1, :]
        decay = jnp.where(lower, jnp.exp(jnp.where(lower, gc - gc_row, 0.0)), 0.0)
        k16 = k.astype(BF16)
        a_mats.append(jnp.where(strict, _dot_nt((k * beta).astype(BF16), k16) * decay, 0.0))
        k16s.append(k16)
        decays.append(decay)
        e_gcs.append(jnp.exp(gc))

    invs = [eye - a for a in a_mats]
    pows = a_mats
    for _ in range(5):
        pows = [_dot_inv(a, a) for a in pows]
        invs = [_dot_inv(t, eye + a) for t, a in zip(invs, pows)]

    for n, (pr, h) in enumerate(chains):
        rows, cols = rows_of(pr), cols_of(h)
        gc = gts[pr][:, GC_LANE + h:GC_LANE + h + 1]
        beta = gts[pr][:, BETA_LANE + h:BETA_LANE + h + 1]
        q = q_ref[rows, cols]
        k = k_ref[rows, cols]
        inv16 = invs[n].astype(BF16)
        u_ref[rows, cols] = _dot(inv16, (v_ref[rows, cols] * beta).astype(BF16))
        w = _dot(inv16, ((k * beta) * e_gcs[n]).astype(BF16)).astype(BF16)
        qe = (q * e_gcs[n]).astype(BF16)
        qk = jnp.where(lower, _dot_nt(q.astype(BF16), k16s[n]) * decays[n], 0.0).astype(BF16)
        g_last = jnp.where(first_chunk_row, gts[pr][CHUNK - 1:CHUNK, :], gts[pr][PAIR - 1:PAIR, :])
        kd = k * jnp.exp(g_last[:, GC_LANE + h:GC_LANE + h + 1] - gc)
        kdt_ref[0, h, :, rows] = kd.T.astype(BF16)
        for c in range(PAIR // CHUNK):
            sub = slice(c * CHUNK, (c + 1) * CHUNK)
            r0 = 2 * (pr * PAIR + c * CHUNK)
            wq_ref[r0:r0 + CHUNK, cols] = w[sub]
            wq_ref[r0 + CHUNK:r0 + 2 * CHUNK, cols] = qe[sub]
            qk_ref[pr * PAIR + c * CHUNK:pr * PAIR + (c + 1) * CHUNK, h * CHUNK:(h + 1) * CHUNK] = qk[sub, sub]


def _gdn_scan_kernel(u_ref, wq_ref, kdt_ref, qk_ref, z_ref, gates_ref, nw_ref, o_ref, state_ref, *, bg):
    @pl.when(pl.program_id(1) == 0)
    def _():
        state_ref[...] = jnp.zeros_like(state_ref)

    nw = nw_ref[...]
    chains = [(b, h) for b in range(bg) for h in range(GDN_HEADS)]
    cols_of = lambda h: slice(h * GDN_HEAD_DIM, (h + 1) * GDN_HEAD_DIM)
    states = [state_ref[b * GDN_HEADS + h] for b, h in chains]
    for c in range(PAIR // CHUNK):
        rows = slice(c * CHUNK, (c + 1) * CHUNK)
        last = (c + 1) * CHUNK - 1
        rs = [_dot(wq_ref[b, 2 * c * CHUNK:2 * (c + 1) * CHUNK, cols_of(h)], states[n].astype(BF16))
              for n, (b, h) in enumerate(chains)]
        v_news = [(u_ref[b, rows, cols_of(h)] - rs[n][0:CHUNK]).astype(BF16) for n, (b, h) in enumerate(chains)]
        upds = [_dot(kdt_ref[b, h, :, rows], v_news[n]) for n, (b, h) in enumerate(chains)]
        states = [states[n] * jnp.exp(gates_ref[b, last:last + 1, GC_LANE + h:GC_LANE + h + 1]) + upds[n]
                  for n, (b, h) in enumerate(chains)]
        for n, (b, h) in enumerate(chains):
            o = rs[n][CHUNK:2 * CHUNK] + _dot(qk_ref[b, rows, h * CHUNK:(h + 1) * CHUNK], v_news[n])
            o = o * lax.rsqrt(jnp.mean(o * o, axis=-1, keepdims=True) + EPS) * nw
            zz = z_ref[b, rows, cols_of(h)]
            o_ref[b, rows, cols_of(h)] = (o * (zz * _sigmoid(zz))).astype(o_ref.dtype)
    for n, (b, h) in enumerate(chains):
        state_ref[b * GDN_HEADS + h] = states[n]


def _gdn(gq, gk, gv, gz, gates, gates_t, nw, *, batch, seq):
    m = gq.shape[0]
    tp = TP_GDN
    steps = seq // tp
    tok = lambda w: pl.BlockSpec((tp, w), lambda b, j: (b * steps + j, 0))
    u, wq, kdt, qk = pl.pallas_call(
        functools.partial(_gdn_prep_kernel, tp=tp),
        grid=(batch, steps),
        in_specs=[tok(GDN_WIDTH), tok(GDN_WIDTH), tok(GDN_WIDTH), tok(LANES),
                  pl.BlockSpec((1, GATE_ROWS, tp), lambda b, j: (b, 0, j))],
        out_specs=(tok(GDN_WIDTH),
                   pl.BlockSpec((2 * tp, GDN_WIDTH), lambda b, j: (b * steps + j, 0)),
                   pl.BlockSpec((1, GDN_HEADS, GDN_HEAD_DIM, tp), lambda b, j: (b, 0, 0, j)),
                   tok(GDN_HEADS * CHUNK)),
        out_shape=(jax.ShapeDtypeStruct((m, GDN_WIDTH), F32),
                   jax.ShapeDtypeStruct((2 * m, GDN_WIDTH), BF16),
                   jax.ShapeDtypeStruct((batch, GDN_HEADS, GDN_HEAD_DIM, seq), BF16),
                   jax.ShapeDtypeStruct((m, GDN_HEADS * CHUNK), BF16)),
        compiler_params=pltpu.CompilerParams(
            dimension_semantics=("arbitrary", "arbitrary"), vmem_limit_bytes=VMEM_LIMIT),
        name="gdn_prep",
    )(gq, gk, gv, gates, gates_t)

    bg = BG_GDN
    seq_blk = lambda rows, w: pl.BlockSpec((bg, rows, w), lambda bi, j: (bi, j, 0))
    o = pl.pallas_call(
        functools.partial(_gdn_scan_kernel, bg=bg),
        grid=(batch // bg, seq // PAIR),
        in_specs=[seq_blk(PAIR, GDN_WIDTH), seq_blk(2 * PAIR, GDN_WIDTH),
                  pl.BlockSpec((bg, GDN_HEADS, GDN_HEAD_DIM, PAIR), lambda bi, j: (bi, 0, 0, j)),
                  seq_blk(PAIR, GDN_HEADS * CHUNK), seq_blk(PAIR, GDN_WIDTH), seq_blk(PAIR, LANES),
                  _const_spec(nw.shape)],
        out_specs=seq_blk(PAIR, GDN_WIDTH),
        out_shape=jax.ShapeDtypeStruct((batch, seq, GDN_WIDTH), BF16),
        scratch_shapes=[pltpu.VMEM((bg * GDN_HEADS, GDN_HEAD_DIM, GDN_HEAD_DIM), F32)],
        compiler_params=pltpu.CompilerParams(
            dimension_semantics=("arbitrary", "arbitrary"), vmem_limit_bytes=VMEM_LIMIT),
        name="gdn_scan",
    )(u.reshape(batch, seq, GDN_WIDTH), wq.reshape(batch, 2 * seq, GDN_WIDTH), kdt,
      qk.reshape(batch, seq, GDN_HEADS * CHUNK), gz.reshape(batch, seq, GDN_WIDTH),
      gates.reshape(batch, seq, LANES), nw)
    return o.reshape(m, GDN_WIDTH)


def _fox_kernel(qa_ref, ka_ref, vt_ref, o_ref, m_ref, l_ref, acc_ref, *, tq):
    i = pl.program_id(2)
    heads = qa_ref.shape[1] // LANES
    qas = [qa_ref[:, hh * LANES:(hh + 1) * LANES] for hh in range(heads)]

    m_ref[...] = jnp.full(m_ref.shape, NEG_BIG, F32)
    l_ref[...] = jnp.zeros(l_ref.shape, F32)
    acc_ref[...] = jnp.zeros(acc_ref.shape, F32)

    def chunk(start, size, mask_off):
        ka = ka_ref[pl.ds(start, size), :]
        sts = [_dot_nt(ka[:, hh * LANES:(hh + 1) * LANES], qas[hh]) for hh in range(heads)]
        if mask_off is not None:
            ri = lax.broadcasted_iota(jnp.int32, (size, tq), 0)
            ci = lax.broadcasted_iota(jnp.int32, (size, tq), 1)
            keep = (ri - ci) <= mask_off
            sts = [jnp.where(keep, s, NEG_BIG) for s in sts]
        m_olds = [m_ref[hh] for hh in range(heads)]
        m_news = [jnp.maximum(m, jnp.max(s, axis=0, keepdims=True)) for m, s in zip(m_olds, sts)]
        ps = [jnp.exp2(s - m) for s, m in zip(sts, m_news)]
        pvs = [_dot(vt_ref[0, hh * FOX_HEAD_DIM:(hh + 1) * FOX_HEAD_DIM, pl.ds(start, size)], ps[hh].astype(BF16))
               for hh in range(heads)]
        for hh in range(heads):
            alpha = jnp.exp2(m_olds[hh] - m_news[hh])
            rows = slice(hh * FOX_HEAD_DIM, (hh + 1) * FOX_HEAD_DIM)
            m_ref[hh] = m_news[hh]
            l_ref[hh] = alpha * l_ref[hh] + jnp.sum(ps[hh], axis=0, keepdims=True)
            acc_ref[rows, :] = alpha * acc_ref[rows, :] + pvs[hh]

    def body(j, carry):
        chunk(pl.multiple_of(j * 2 * tq, 2 * tq), 2 * tq, None)
        return carry

    lax.fori_loop(0, i // 2, body, 0)

    @pl.when(i % 2 == 0)
    def _():
        chunk(pl.multiple_of(i * tq, tq), tq, 0)

    @pl.when(i % 2 == 1)
    def _():
        chunk(pl.multiple_of((i - 1) * tq, tq), 2 * tq, tq)

    out_t = jnp.concatenate(
        [acc_ref[hh * FOX_HEAD_DIM:(hh + 1) * FOX_HEAD_DIM, :] / l_ref[hh] for hh in range(heads)], axis=0)
    o_ref[...] = out_t.T.astype(o_ref.dtype)


def _fox(fqa, fka, fvt, *, batch, seq):
    m = fqa.shape[0]
    tq = TQ_FOX
    steps = seq // tq
    heads = HEADS_FOX
    return pl.pallas_call(
        functools.partial(_fox_kernel, tq=tq),
        grid=(batch, FOX_HEADS // heads, steps),
        in_specs=[pl.BlockSpec((tq, heads * LANES), lambda b, p, i: (b * steps + i, p)),
                  pl.BlockSpec((seq, heads * LANES), lambda b, p, i: (b, p)),
                  pl.BlockSpec((1, heads * FOX_HEAD_DIM, seq), lambda b, p, i: (b, p, 0))],
        out_specs=pl.BlockSpec((tq, heads * FOX_HEAD_DIM), lambda b, p, i: (b * steps + i, p)),
        out_shape=jax.ShapeDtypeStruct((m, FOX_WIDTH), BF16),
        scratch_shapes=[pltpu.VMEM((heads, 1, tq), F32), pltpu.VMEM((heads, 1, tq), F32),
                        pltpu.VMEM((heads * FOX_HEAD_DIM, tq), F32)],
        compiler_params=pltpu.CompilerParams(
            dimension_semantics=("arbitrary", "arbitrary", "arbitrary"), vmem_limit_bytes=VMEM_LIMIT),
        name="fox",
    )(fqa, fka, fvt)


def _post_kernel(x_ref, og_ref, of_ref, p_ref, wo_g_ref, wo_f_ref, ffn_nw_ref, wgate_ref, wup_ref, wdown_ref,
                 ple_nw_ref, wpg_ref, wpp_ref, fin_nw_ref, out_ref, act_ref, *, tf):
    h = x_ref[...] + _dot(og_ref[...], wo_g_ref[...]) + _dot(of_ref[...], wo_f_ref[...])

    hn = (h * _rms_scale(h) * ffn_nw_ref[...]).astype(BF16)
    d_ff = wgate_ref.shape[1]
    for c in range(d_ff // tf):
        cols = slice(c * tf, (c + 1) * tf)
        gate = _dot(hn, wgate_ref[:, cols])
        up = _dot(hn, wup_ref[:, cols])
        act_ref[:, cols] = (gate * _sigmoid(gate) * up).astype(BF16)
    h = h + _dot(act_ref[...], wdown_ref[...])

    hn = (h * _rms_scale(h) * ple_nw_ref[...]).astype(BF16)
    ple_gate = _sigmoid(_dot(hn, wpg_ref[...]))
    h = h + ple_gate * _dot(p_ref[...].astype(BF16), wpp_ref[...])

    out_ref[...] = h * _rms_scale(h) * fin_nw_ref[...]


def _post(x2, o_g, o_f, p2, wo_g, wo_f, ffn_nw, wgate, wup, wdown, ple_nw, wpg, wpp, fin_nw):
    m, d = x2.shape
    tm = TM_POST
    tok = lambda w: pl.BlockSpec((tm, w), lambda i: (i, 0))
    weights = (wo_g, wo_f, ffn_nw, wgate, wup, wdown, ple_nw, wpg, wpp, fin_nw)
    in_specs = [tok(d), tok(GDN_WIDTH), tok(FOX_WIDTH), tok(p2.shape[1])] + [_const_spec(w.shape) for w in weights]
    return pl.pallas_call(
        functools.partial(_post_kernel, tf=TF_POST),
        grid=(m // tm,), in_specs=in_specs, out_specs=tok(d),
        out_shape=jax.ShapeDtypeStruct((m, d), F32),
        scratch_shapes=[pltpu.VMEM((tm, wgate.shape[1]), BF16)],
        compiler_params=pltpu.CompilerParams(dimension_semantics=("arbitrary",), vmem_limit_bytes=VMEM_LIMIT),
        name="post",
    )(x2, o_g, o_f, p2, *weights)


def _layer(x, p_i, attn_norm_w, w_in, conv_w, a_log, dt_bias, gdn_norm_w, fox_f_bias, w_out,
           ffn_norm_w, w_gate_up, w_down, ple_norm_w, w_ple_gate, w_ple_proj, out_norm_w):
    batch, seq, d = x.shape
    m = batch * seq
    assert seq % TM_IN == 0 and seq % TP_GDN == 0 and seq % TQ_FOX == 0 and m % TM_POST == 0
    assert batch % BG_GDN == 0
    row = lambda w: w.reshape(1, -1).astype(F32)

    c0 = 3 * GDN_WIDTH
    c1 = c0 + GDN_WIDTH
    c2 = c1 + 2 * GDN_HEADS
    c3 = c2 + 3 * FOX_WIDTH
    wg = w_in[:, :c0].astype(BF16)
    wz = w_in[:, c0:c1].astype(BF16)
    wf = w_in[:, c2:c2 + 2 * FOX_WIDTH].astype(BF16)
    wfvt = w_in[:, c2 + 2 * FOX_WIDTH:c3].T.astype(BF16)
    n_small = 2 * GDN_HEADS + FOX_HEADS
    ws = jnp.concatenate([w_in[:, c1:c2], w_in[:, c3:]], axis=1)
    ws = jnp.pad(ws, ((0, 0), (0, LANES - n_small))).astype(BF16)
    zeros4 = jnp.zeros((GDN_HEADS,), F32)
    pad = jnp.zeros((LANES - n_small,), F32)
    prm = jnp.stack([
        jnp.concatenate([dt_bias.astype(F32), zeros4, fox_f_bias.astype(F32), pad]),
        jnp.concatenate([a_log.astype(F32), zeros4, jnp.zeros((FOX_HEADS,), F32), pad]),
    ])
    prm = jnp.pad(prm, ((0, SUBLANES - prm.shape[0]), (0, 0)))

    x2 = x.reshape(m, d)
    gq, gk, gv, gz, fqa, fka, fvt, gates, gates_t = _inproj(
        x2, row(attn_norm_w), wg, wz, wf, wfvt, ws, conv_w.astype(F32), prm, batch=batch, seq=seq)
    o_g = _gdn(gq, gk, gv, gz, gates, gates_t, row(gdn_norm_w), batch=batch, seq=seq)
    o_f = _fox(fqa, fka, fvt, batch=batch, seq=seq)

    d_ff = w_down.shape[0]
    return _post(
        x2, o_g, o_f, p_i.reshape(m, -1),
        w_out[:GDN_WIDTH].astype(BF16), w_out[GDN_WIDTH:].astype(BF16), row(ffn_norm_w),
        w_gate_up[:, :d_ff].astype(BF16), w_gate_up[:, d_ff:].astype(BF16), w_down.astype(BF16),
        row(ple_norm_w), w_ple_gate.astype(BF16), w_ple_proj.astype(BF16), row(out_norm_w),
    ).reshape(batch, seq, d)


def kernel(x, p, attn_norm_w, w_in, conv_w, a_log, dt_bias, gdn_norm_w, fox_f_bias, w_out, ffn_norm_w,
           w_gate_up, w_down, ple_norm_w, w_ple_gate, w_ple_proj, final_norm_w):
    assert p.shape[0] == 1, "single-layer problem"
    return _layer(x, p[0], attn_norm_w[0], w_in[0], conv_w[0], a_log[0], dt_bias[0], gdn_norm_w[0],
                  fox_f_bias[0], w_out[0], ffn_norm_w[0], w_gate_up[0], w_down[0], ple_norm_w[0],
                  w_ple_gate[0], w_ple_proj[0], final_norm_w)
```

```python
import functools

import jax
import jax.numpy as jnp
import numpy as np
from jax import lax
from jax.experimental import pallas as pl
from jax.experimental.pallas import tpu as pltpu

F32 = jnp.float32
BF16 = jnp.bfloat16

EPS = 1e-6
GDN_HEADS = 4
GDN_HEAD_DIM = 128
GDN_WIDTH = GDN_HEADS * GDN_HEAD_DIM
FOX_HEADS = 8
FOX_HEAD_DIM = 64
FOX_WIDTH = FOX_HEADS * FOX_HEAD_DIM
CONV_WIDTH = 4
CHUNK = 64
PAIR = 2 * CHUNK
LANES = 128
SUBLANES = 8
NEG_BIG = -1e30
LOG2E = 1.4426950408889634

GC_LANE = 0
BETA_LANE = 4
CF_LANE = 8
GATE_ROWS = 16

VMEM_LIMIT = 56 * 1024 * 1024

TM_IN = 256
TP_GDN = 512
BG_GDN = 4
TQ_FOX = 256
HEADS_FOX = 8
TM_POST = 512
TF_POST = 256


def _dot(a, b):
    return jnp.dot(a, b, preferred_element_type=F32)


def _dot_nt(a, b):
    return lax.dot_general(a, b, (((1,), (1,)), ((), ())), preferred_element_type=F32)


def _dot_tn(a, b):
    return lax.dot_general(a, b, (((0,), (0,)), ((), ())), preferred_element_type=F32)


def _dot_inv(a, b):
    return _dot(a.astype(BF16), b.astype(BF16))


def _sigmoid(x):
    return 0.5 * jnp.tanh(0.5 * x) + 0.5


def _silu(x):
    h = 0.5 * x
    return h * jnp.tanh(h) + h


def _rms_scale(x):
    return lax.rsqrt(jnp.mean(x * x, axis=-1, keepdims=True) + EPS)


def _const_spec(shape):
    nd = len(shape)
    return pl.BlockSpec(shape, lambda *_: (0,) * nd, pipeline_mode=pl.Buffered(1))


def _inproj_kernel(x_ref, nw_ref, wg_ref, wz_ref, wf_ref, wfvt_ref, ws_ref, conv_ref, prm_ref, sel_ref,
                   gq_ref, gk_ref, gv_ref, gz_ref, fqa_ref, fka_ref, fvt_ref, gates_ref, gates_t_ref,
                   buf_ref, carry_ref, *, tm, tiles_per_seq):
    i = pl.program_id(0)
    seq_start = (i % tiles_per_seq) == 0
    halo = SUBLANES

    x = x_ref[...]
    xn = (x * _rms_scale(x) * nw_ref[...]).astype(BF16)

    @pl.when(seq_start)
    def _():
        buf_ref[0:halo, :] = jnp.zeros((halo, buf_ref.shape[1]), F32)
        carry_ref[...] = jnp.zeros_like(carry_ref)

    @pl.when(jnp.logical_not(seq_start))
    def _():
        buf_ref[0:halo, :] = buf_ref[tm:tm + halo, :]

    buf_ref[halo:halo + tm, :] = _dot(xn, wg_ref[...])

    out_refs = (gq_ref, gk_ref, gv_ref)
    for j in range(3 * GDN_HEADS):
        cols = slice(j * LANES, (j + 1) * LANES)
        acc = buf_ref[halo:halo + tm, cols] * conv_ref[CONV_WIDTH - 1:CONV_WIDTH, cols]
        for t in range(CONV_WIDTH - 1):
            off = halo - (CONV_WIDTH - 1) + t
            acc = acc + buf_ref[off:off + tm, cols] * conv_ref[t:t + 1, cols]
        y = _silu(acc)
        kind, head = divmod(j, GDN_HEADS)
        if kind < 2:
            y = y * lax.rsqrt(jnp.sum(y * y, axis=-1, keepdims=True) + EPS)
        if kind == 0:
            y = y * (GDN_HEAD_DIM ** -0.5)
        out_refs[kind][:, head * LANES:(head + 1) * LANES] = y

    gz_ref[...] = _dot(xn, wz_ref[...])

    fvt_ref[0] = _dot_nt(wfvt_ref[...], xn).astype(BF16)

    z = _dot(xn, ws_ref[...]) + prm_ref[0:1, :]
    lane = lax.broadcasted_iota(jnp.int32, (tm, LANES), 1)
    row = lax.broadcasted_iota(jnp.int32, (tm, LANES), 0)
    is_gc = lane < BETA_LANE
    is_cf = (lane >= CF_LANE) & (lane < CF_LANE + FOX_HEADS)
    soft = jnp.log1p(jnp.exp(-jnp.abs(z)))
    g_decay = -jnp.exp(prm_ref[1:2, :]) * (jnp.maximum(z, 0.0) + soft)
    log_f = -(jnp.maximum(-z, 0.0) + soft)
    val = jnp.where(is_gc, g_decay, jnp.where(lane < CF_LANE, _sigmoid(z), jnp.where(is_cf, log_f, 0.0)))

    pos = jnp.where(is_gc, row % CHUNK, row)
    summed = is_gc | is_cf
    d = 1
    while d < tm:
        shifted = pltpu.roll(val, d, axis=0)
        val = val + jnp.where(summed & (pos >= d), shifted, 0.0)
        d *= 2
    val = val + jnp.where(is_cf, carry_ref[...], 0.0)
    carry_ref[...] = val[tm - 1:tm, :]

    gates_ref[...] = val
    gates_t_ref[0] = val.T[0:GATE_ROWS, :]

    f_qk = _dot(xn, wf_ref[...])
    c2 = jnp.where(is_cf, val * LOG2E, 0.0)
    c_hi = c2.astype(BF16).astype(F32)
    c_mid = (c2 - c_hi).astype(BF16).astype(F32)
    c_lo = c2 - c_hi - c_mid
    parts = (jnp.where(lane == 0, 1.0, c_hi) + pltpu.roll(c_mid, FOX_HEADS, axis=1)
             + pltpu.roll(c_lo, 2 * FOX_HEADS, axis=1))
    bias = _dot(parts.astype(BF16), sel_ref[...])
    for h in range(FOX_HEADS):
        src, half = divmod(h, 2)
        in_half = (lane >= half * FOX_HEAD_DIM) & (lane < (half + 1) * FOX_HEAD_DIM)
        q_slab = f_qk[:, src * LANES:(src + 1) * LANES] * (FOX_HEAD_DIM ** -0.5 * LOG2E)
        k_slab = f_qk[:, FOX_WIDTH + src * LANES:FOX_WIDTH + (src + 1) * LANES]
        q_bias = bias[:, h * LANES:(h + 1) * LANES]
        k_bias = bias[:, (FOX_HEADS + h) * LANES:(FOX_HEADS + h + 1) * LANES]
        fqa_ref[:, h * LANES:(h + 1) * LANES] = jnp.where(in_half, q_slab, q_bias).astype(BF16)
        fka_ref[:, h * LANES:(h + 1) * LANES] = jnp.where(in_half, k_slab, k_bias).astype(BF16)


def _bias_selector():
    sel = np.zeros((LANES, 2 * FOX_HEADS * LANES), np.float32)
    for h in range(FOX_HEADS):
        base = (1 - h % 2) * FOX_HEAD_DIM
        q0 = h * LANES + base
        k0 = (FOX_HEADS + h) * LANES + base
        for part in range(3):
            src = CF_LANE + part * FOX_HEADS + h
            sel[src, q0 + part] = 1.0
            sel[0, q0 + 3 + part] = -1.0
            sel[0, k0 + part] = 1.0
            sel[src, k0 + 3 + part] = 1.0
    return jnp.asarray(sel, BF16)


def _inproj(x2, nw, wg, wz, wf, wfvt, ws, conv_w, prm, *, batch, seq):
    sel = _bias_selector()
    m, d = x2.shape
    tm = TM_IN
    tiles_per_seq = seq // tm
    grid = (m // tm,)
    tok = lambda w: pl.BlockSpec((tm, w), lambda i: (i, 0))
    seq_t = lambda rows: pl.BlockSpec((1, rows, tm), lambda i: (i // tiles_per_seq, 0, i % tiles_per_seq))
    out_shape = (
        jax.ShapeDtypeStruct((m, GDN_WIDTH), F32),
        jax.ShapeDtypeStruct((m, GDN_WIDTH), F32),
        jax.ShapeDtypeStruct((m, GDN_WIDTH), F32),
        jax.ShapeDtypeStruct((m, GDN_WIDTH), F32),
        jax.ShapeDtypeStruct((m, FOX_HEADS * LANES), BF16),
        jax.ShapeDtypeStruct((m, FOX_HEADS * LANES), BF16),
        jax.ShapeDtypeStruct((batch, FOX_WIDTH, seq), BF16),
        jax.ShapeDtypeStruct((m, LANES), F32),
        jax.ShapeDtypeStruct((batch, GATE_ROWS, seq), F32),
    )
    out_specs = (
        tok(GDN_WIDTH), tok(GDN_WIDTH), tok(GDN_WIDTH), tok(GDN_WIDTH),
        tok(FOX_HEADS * LANES), tok(FOX_HEADS * LANES), seq_t(FOX_WIDTH), tok(LANES), seq_t(GATE_ROWS),
    )
    in_specs = [
        tok(d), _const_spec(nw.shape), _const_spec(wg.shape), _const_spec(wz.shape), _const_spec(wf.shape),
        _const_spec(wfvt.shape), _const_spec(ws.shape), _const_spec(conv_w.shape), _const_spec(prm.shape),
        _const_spec(sel.shape),
    ]
    return pl.pallas_call(
        functools.partial(_inproj_kernel, tm=tm, tiles_per_seq=tiles_per_seq),
        grid=grid, in_specs=in_specs, out_specs=out_specs, out_shape=out_shape,
        scratch_shapes=[pltpu.VMEM((tm + SUBLANES, 3 * GDN_WIDTH), F32), pltpu.VMEM((1, LANES), F32)],
        compiler_params=pltpu.CompilerParams(dimension_semantics=("arbitrary",), vmem_limit_bytes=VMEM_LIMIT),
        name="inproj",
    )(x2, nw, wg, wz, wf, wfvt, ws, conv_w, prm, sel)


def _gdn_prep_kernel(q_ref, k_ref, v_ref, gates_ref, gates_t_ref, u_ref, wq_ref, kdt_ref, qk_ref, *, tp):
    ri = lax.broadcasted_iota(jnp.int32, (PAIR, PAIR), 0)
    ci = lax.broadcasted_iota(jnp.int32, (PAIR, PAIR), 1)
    same_chunk = (ri // CHUNK) == (ci // CHUNK)
    lower = same_chunk & (ri >= ci)
    strict = same_chunk & (ri > ci)
    eye = (ri == ci).astype(F32)
    first_chunk_row = lax.broadcasted_iota(jnp.int32, (PAIR, LANES), 0) < CHUNK

    chains = [(pr, h) for pr in range(tp // PAIR) for h in range(GDN_HEADS)]
    rows_of = lambda pr: slice(pr * PAIR, (pr + 1) * PAIR)
    cols_of = lambda h: slice(h * GDN_HEAD_DIM, (h + 1) * GDN_HEAD_DIM)

    gts = [gates_ref[rows_of(pr), :] for pr in range(tp // PAIR)]
    gt_ts = [gates_t_ref[0, :, rows_of(pr)] for pr in range(tp // PAIR)]
    k16s, a_mats, decays, e_gcs = [], [], [], []
    for pr, h in chains:
        k = k_ref[rows_of(pr), cols_of(h)]
        gc = gts[pr][:, GC_LANE + h:GC_LANE + h + 1]
        beta = gts[pr][:, BETA_LANE + h:BETA_LANE + h + 1]
        gc_row = gt_ts[pr][GC_LANE + h:GC_LANE + h + 1, :]
        decay = jnp.where(lower, jnp.exp(jnp.where(lower, gc - gc_row, 0.0)), 0.0)
        k16 = k.astype(BF16)
        a_mats.append(jnp.where(strict, _dot_nt((k * beta).astype(BF16), k16) * decay, 0.0))
        k16s.append(k16)
        decays.append(decay)
        e_gcs.append(jnp.exp(gc))

    invs = [eye - a for a in a_mats]
    pows = a_mats
    for _ in range(5):
        pows = [_dot_inv(a, a) for a in pows]
        invs = [_dot_inv(t, eye + a) for t, a in zip(invs, pows)]

    for n, (pr, h) in enumerate(chains):
        rows, cols = rows_of(pr), cols_of(h)
        gc = gts[pr][:, GC_LANE + h:GC_LANE + h + 1]
        beta = gts[pr][:, BETA_LANE + h:BETA_LANE + h + 1]
        q = q_ref[rows, cols]
        k = k_ref[rows, cols]
        inv16 = invs[n].astype(BF16)
        u_ref[rows, cols] = _dot(inv16, (v_ref[rows, cols] * beta).astype(BF16))
        w = _dot(inv16, ((k * beta) * e_gcs[n]).astype(BF16)).astype(BF16)
        qe = (q * e_gcs[n]).astype(BF16)
        qk = jnp.where(lower, _dot_nt(q.astype(BF16), k16s[n]) * decays[n], 0.0).astype(BF16)
        g_last = jnp.where(first_chunk_row, gts[pr][CHUNK - 1:CHUNK, :], gts[pr][PAIR - 1:PAIR, :])
        kd = k * jnp.exp(g_last[:, GC_LANE + h:GC_LANE + h + 1] - gc)
        kdt_ref[0, h, :, rows] = kd.T.astype(BF16)
        for c in range(PAIR // CHUNK):
            sub = slice(c * CHUNK, (c + 1) * CHUNK)
            r0 = 2 * (pr * PAIR + c * CHUNK)
            wq_ref[r0:r0 + CHUNK, cols] = w[sub]
            wq_ref[r0 + CHUNK:r0 + 2 * CHUNK, cols] = qe[sub]
            qk_ref[pr * PAIR + c * CHUNK:pr * PAIR + (c + 1) * CHUNK, h * CHUNK:(h + 1) * CHUNK] = qk[sub, sub]


def _gdn_scan_kernel(u_ref, wq_ref, kdt_ref, qk_ref, z_ref, gates_ref, nw_ref, o_ref, state_ref, *, bg):
    @pl.when(pl.program_id(1) == 0)
    def _():
        state_ref[...] = jnp.zeros_like(state_ref)

    nw = nw_ref[...]
    chains = [(b, h) for b in range(bg) for h in range(GDN_HEADS)]
    cols_of = lambda h: slice(h * GDN_HEAD_DIM, (h + 1) * GDN_HEAD_DIM)
    states = [state_ref[b * GDN_HEADS + h] for b, h in chains]
    for c in range(PAIR // CHUNK):
        rows = slice(c * CHUNK, (c + 1) * CHUNK)
        last = (c + 1) * CHUNK - 1
        rs = [_dot(wq_ref[b, 2 * c * CHUNK:2 * (c + 1) * CHUNK, cols_of(h)], states[n].astype(BF16))
              for n, (b, h) in enumerate(chains)]
        v_news = [(u_ref[b, rows, cols_of(h)] - rs[n][0:CHUNK]).astype(BF16) for n, (b, h) in enumerate(chains)]
        upds = [_dot(kdt_ref[b, h, :, rows], v_news[n]) for n, (b, h) in enumerate(chains)]
        states = [states[n] * jnp.exp(gates_ref[b, last:last + 1, GC_LANE + h:GC_LANE + h + 1]) + upds[n]
                  for n, (b, h) in enumerate(chains)]
        for n, (b, h) in enumerate(chains):
            o = rs[n][CHUNK:2 * CHUNK] + _dot(qk_ref[b, rows, h * CHUNK:(h + 1) * CHUNK], v_news[n])
            o = o * lax.rsqrt(jnp.mean(o * o, axis=-1, keepdims=True) + EPS) * nw
            zz = z_ref[b, rows, cols_of(h)]
            o_ref[b, rows, cols_of(h)] = (o * _silu(zz)).astype(o_ref.dtype)
    for n, (b, h) in enumerate(chains):
        state_ref[b * GDN_HEADS + h] = states[n]


def _gdn(gq, gk, gv, gz, gates, gates_t, nw, *, batch, seq):
    m = gq.shape[0]
    tp = TP_GDN
    steps = seq // tp
    tok = lambda w: pl.BlockSpec((tp, w), lambda b, j: (b * steps + j, 0))
    u, wq, kdt, qk = pl.pallas_call(
        functools.partial(_gdn_prep_kernel, tp=tp),
        grid=(batch, steps),
        in_specs=[tok(GDN_WIDTH), tok(GDN_WIDTH), tok(GDN_WIDTH), tok(LANES),
                  pl.BlockSpec((1, GATE_ROWS, tp), lambda b, j: (b, 0, j))],
        out_specs=(tok(GDN_WIDTH),
                   pl.BlockSpec((2 * tp, GDN_WIDTH), lambda b, j: (b * steps + j, 0)),
                   pl.BlockSpec((1, GDN_HEADS, GDN_HEAD_DIM, tp), lambda b, j: (b, 0, 0, j)),
                   tok(GDN_HEADS * CHUNK)),
        out_shape=(jax.ShapeDtypeStruct((m, GDN_WIDTH), F32),
                   jax.ShapeDtypeStruct((2 * m, GDN_WIDTH), BF16),
                   jax.ShapeDtypeStruct((batch, GDN_HEADS, GDN_HEAD_DIM, seq), BF16),
                   jax.ShapeDtypeStruct((m, GDN_HEADS * CHUNK), BF16)),
        compiler_params=pltpu.CompilerParams(
            dimension_semantics=("arbitrary", "arbitrary"), vmem_limit_bytes=VMEM_LIMIT),
        name="gdn_prep",
    )(gq, gk, gv, gates, gates_t)

    bg = BG_GDN
    seq_blk = lambda rows, w: pl.BlockSpec((bg, rows, w), lambda bi, j: (bi, j, 0))
    o = pl.pallas_call(
        functools.partial(_gdn_scan_kernel, bg=bg),
        grid=(batch // bg, seq // PAIR),
        in_specs=[seq_blk(PAIR, GDN_WIDTH), seq_blk(2 * PAIR, GDN_WIDTH),
                  pl.BlockSpec((bg, GDN_HEADS, GDN_HEAD_DIM, PAIR), lambda bi, j: (bi, 0, 0, j)),
                  seq_blk(PAIR, GDN_HEADS * CHUNK), seq_blk(PAIR, GDN_WIDTH), seq_blk(PAIR, LANES),
                  _const_spec(nw.shape)],
        out_specs=seq_blk(PAIR, GDN_WIDTH),
        out_shape=jax.ShapeDtypeStruct((batch, seq, GDN_WIDTH), BF16),
        scratch_shapes=[pltpu.VMEM((bg * GDN_HEADS, GDN_HEAD_DIM, GDN_HEAD_DIM), F32)],
        compiler_params=pltpu.CompilerParams(
            dimension_semantics=("arbitrary", "arbitrary"), vmem_limit_bytes=VMEM_LIMIT),
        name="gdn_scan",
    )(u.reshape(batch, seq, GDN_WIDTH), wq.reshape(batch, 2 * seq, GDN_WIDTH), kdt,
      qk.reshape(batch, seq, GDN_HEADS * CHUNK), gz.reshape(batch, seq, GDN_WIDTH),
      gates.reshape(batch, seq, LANES), nw)
    return o.reshape(m, GDN_WIDTH)


def _fox_kernel(qa_ref, ka_ref, vt_ref, o_ref, m_ref, l_ref, acc_ref, *, tq):
    i = pl.program_id(2)
    heads = qa_ref.shape[1] // LANES
    qas = [qa_ref[:, hh * LANES:(hh + 1) * LANES] for hh in range(heads)]

    m_ref[...] = jnp.full(m_ref.shape, NEG_BIG, F32)
    l_ref[...] = jnp.zeros(l_ref.shape, F32)
    acc_ref[...] = jnp.zeros(acc_ref.shape, F32)

    def chunk(start, size, mask_off):
        ka = ka_ref[pl.ds(start, size), :]
        sts = [_dot_nt(ka[:, hh * LANES:(hh + 1) * LANES], qas[hh]) for hh in range(heads)]
        if mask_off is not None:
            ri = lax.broadcasted_iota(jnp.int32, (size, tq), 0)
            ci = lax.broadcasted_iota(jnp.int32, (size, tq), 1)
            keep = (ri - ci) <= mask_off
            sts = [jnp.where(keep, s, NEG_BIG) for s in sts]
        m_olds = [m_ref[hh] for hh in range(heads)]
        m_news = [jnp.maximum(m, jnp.max(s, axis=0, keepdims=True)) for m, s in zip(m_olds, sts)]
        ps = [jnp.exp2(s - m) for s, m in zip(sts, m_news)]
        pvs = [_dot(vt_ref[0, hh * FOX_HEAD_DIM:(hh + 1) * FOX_HEAD_DIM, pl.ds(start, size)], ps[hh].astype(BF16))
               for hh in range(heads)]
        for hh in range(heads):
            alpha = jnp.exp2(m_olds[hh] - m_news[hh])
            rows = slice(hh * FOX_HEAD_DIM, (hh + 1) * FOX_HEAD_DIM)
            m_ref[hh] = m_news[hh]
            l_ref[hh] = alpha * l_ref[hh] + jnp.sum(ps[hh], axis=0, keepdims=True)
            acc_ref[rows, :] = alpha * acc_ref[rows, :] + pvs[hh]

    def body(j, carry):
        chunk(pl.multiple_of(j * 2 * tq, 2 * tq), 2 * tq, None)
        return carry

    lax.fori_loop(0, i // 2, body, 0)

    @pl.when(i % 2 == 0)
    def _():
        chunk(pl.multiple_of(i * tq, tq), tq, 0)

    @pl.when(i % 2 == 1)
    def _():
        chunk(pl.multiple_of((i - 1) * tq, tq), 2 * tq, tq)

    out_t = jnp.concatenate(
        [acc_ref[hh * FOX_HEAD_DIM:(hh + 1) * FOX_HEAD_DIM, :] / l_ref[hh] for hh in range(heads)], axis=0)
    o_ref[...] = out_t.T.astype(o_ref.dtype)


def _fox(fqa, fka, fvt, *, batch, seq):
    m = fqa.shape[0]
    tq = TQ_FOX
    steps = seq // tq
    heads = HEADS_FOX
    return pl.pallas_call(
        functools.partial(_fox_kernel, tq=tq),
        grid=(batch, FOX_HEADS // heads, steps),
        in_specs=[pl.BlockSpec((tq, heads * LANES), lambda b, p, i: (b * steps + i, p)),
                  pl.BlockSpec((seq, heads * LANES), lambda b, p, i: (b, p)),
                  pl.BlockSpec((1, heads * FOX_HEAD_DIM, seq), lambda b, p, i: (b, p, 0))],
        out_specs=pl.BlockSpec((tq, heads * FOX_HEAD_DIM), lambda b, p, i: (b * steps + i, p)),
        out_shape=jax.ShapeDtypeStruct((m, FOX_WIDTH), BF16),
        scratch_shapes=[pltpu.VMEM((heads, 1, tq), F32), pltpu.VMEM((heads, 1, tq), F32),
                        pltpu.VMEM((heads * FOX_HEAD_DIM, tq), F32)],
        compiler_params=pltpu.CompilerParams(
            dimension_semantics=("arbitrary", "arbitrary", "arbitrary"), vmem_limit_bytes=VMEM_LIMIT),
        name="fox",
    )(fqa, fka, fvt)


def _post_kernel(x_ref, og_ref, of_ref, p_ref, wo_g_ref, wo_f_ref, ffn_nw_ref, wgate_ref, wup_ref, wdown_ref,
                 ple_nw_ref, wpg_ref, wpp_ref, fin_nw_ref, out_ref, act_ref, *, tf):
    h = x_ref[...] + _dot(og_ref[...], wo_g_ref[...]) + _dot(of_ref[...], wo_f_ref[...])

    hn = (h * _rms_scale(h) * ffn_nw_ref[...]).astype(BF16)
    d_ff = wgate_ref.shape[1]
    for c in range(d_ff // tf):
        cols = slice(c * tf, (c + 1) * tf)
        gate = _dot(hn, wgate_ref[:, cols])
        up = _dot(hn, wup_ref[:, cols])
        act_ref[:, cols] = (_silu(gate) * up).astype(BF16)
    h = h + _dot(act_ref[...], wdown_ref[...])

    hn = (h * _rms_scale(h) * ple_nw_ref[...]).astype(BF16)
    ple_gate = _sigmoid(_dot(hn, wpg_ref[...]))
    h = h + ple_gate * _dot(p_ref[...].astype(BF16), wpp_ref[...])

    out_ref[...] = h * _rms_scale(h) * fin_nw_ref[...]


def _post(x2, o_g, o_f, p2, wo_g, wo_f, ffn_nw, wgate, wup, wdown, ple_nw, wpg, wpp, fin_nw):
    m, d = x2.shape
    tm = TM_POST
    tok = lambda w: pl.BlockSpec((tm, w), lambda i: (i, 0))
    weights = (wo_g, wo_f, ffn_nw, wgate, wup, wdown, ple_nw, wpg, wpp, fin_nw)
    in_specs = [tok(d), tok(GDN_WIDTH), tok(FOX_WIDTH), tok(p2.shape[1])] + [_const_spec(w.shape) for w in weights]
    return pl.pallas_call(
        functools.partial(_post_kernel, tf=TF_POST),
        grid=(m // tm,), in_specs=in_specs, out_specs=tok(d),
        out_shape=jax.ShapeDtypeStruct((m, d), F32),
        scratch_shapes=[pltpu.VMEM((tm, wgate.shape[1]), BF16)],
        compiler_params=pltpu.CompilerParams(dimension_semantics=("arbitrary",), vmem_limit_bytes=VMEM_LIMIT),
        name="post",
    )(x2, o_g, o_f, p2, *weights)


def _layer(x, p_i, attn_norm_w, w_in, conv_w, a_log, dt_bias, gdn_norm_w, fox_f_bias, w_out,
           ffn_norm_w, w_gate_up, w_down, ple_norm_w, w_ple_gate, w_ple_proj, out_norm_w):
    batch, seq, d = x.shape
    m = batch * seq
    assert seq % TM_IN == 0 and seq % TP_GDN == 0 and seq % TQ_FOX == 0 and m % TM_POST == 0
    assert batch % BG_GDN == 0
    row = lambda w: w.reshape(1, -1).astype(F32)

    c0 = 3 * GDN_WIDTH
    c1 = c0 + GDN_WIDTH
    c2 = c1 + 2 * GDN_HEADS
    c3 = c2 + 3 * FOX_WIDTH
    wg = w_in[:, :c0].astype(BF16)
    wz = w_in[:, c0:c1].astype(BF16)
    wf = w_in[:, c2:c2 + 2 * FOX_WIDTH].astype(BF16)
    wfvt = w_in[:, c2 + 2 * FOX_WIDTH:c3].T.astype(BF16)
    n_small = 2 * GDN_HEADS + FOX_HEADS
    ws = jnp.concatenate([w_in[:, c1:c2], w_in[:, c3:]], axis=1)
    ws = jnp.pad(ws, ((0, 0), (0, LANES - n_small))).astype(BF16)
    zeros4 = jnp.zeros((GDN_HEADS,), F32)
    pad = jnp.zeros((LANES - n_small,), F32)
    prm = jnp.stack([
        jnp.concatenate([dt_bias.astype(F32), zeros4, fox_f_bias.astype(F32), pad]),
        jnp.concatenate([a_log.astype(F32), zeros4, jnp.zeros((FOX_HEADS,), F32), pad]),
    ])
    prm = jnp.pad(prm, ((0, SUBLANES - prm.shape[0]), (0, 0)))

    x2 = x.reshape(m, d)
    gq, gk, gv, gz, fqa, fka, fvt, gates, gates_t = _inproj(
        x2, row(attn_norm_w), wg, wz, wf, wfvt, ws, conv_w.astype(F32), prm, batch=batch, seq=seq)
    o_g = _gdn(gq, gk, gv, gz, gates, gates_t, row(gdn_norm_w), batch=batch, seq=seq)
    o_f = _fox(fqa, fka, fvt, batch=batch, seq=seq)

    d_ff = w_down.shape[0]
    return _post(
        x2, o_g, o_f, p_i.reshape(m, -1),
        w_out[:GDN_WIDTH].astype(BF16), w_out[GDN_WIDTH:].astype(BF16), row(ffn_norm_w),
        w_gate_up[:, :d_ff].astype(BF16), w_gate_up[:, d_ff:].astype(BF16), w_down.astype(BF16),
        row(ple_norm_w), w_ple_gate.astype(BF16), w_ple_proj.astype(BF16), row(out_norm_w),
    ).reshape(batch, seq, d)


def kernel(x, p, attn_norm_w, w_in, conv_w, a_log, dt_bias, gdn_norm_w, fox_f_bias, w_out, ffn_norm_w,
           w_gate_up, w_down, ple_norm_w, w_ple_gate, w_ple_proj, final_norm_w):
    assert p.shape[0] == 1, "single-layer problem"
    return _layer(x, p[0], attn_norm_w[0], w_in[0], conv_w[0], a_log[0], dt_bias[0], gdn_norm_w[0],
                  fox_f_bias[0], w_out[0], ffn_norm_w[0], w_gate_up[0], w_down[0], ple_norm_w[0],
                  w_ple_gate[0], w_ple_proj[0], final_norm_w)
```

```python
import functools

import jax
import jax.numpy as jnp
import numpy as np
from jax import lax
from jax.experimental import pallas as pl
from jax.experimental.pallas import tpu as pltpu

F32 = jnp.float32
BF16 = jnp.bfloat16

EPS = 1e-6
GDN_HEADS = 4
GDN_HEAD_DIM = 128
GDN_WIDTH = GDN_HEADS * GDN_HEAD_DIM
FOX_HEADS = 8
FOX_HEAD_DIM = 64
FOX_WIDTH = FOX_HEADS * FOX_HEAD_DIM
CONV_WIDTH = 4
CHUNK = 64
PAIR = 2 * CHUNK
LANES = 128
SUBLANES = 8
BF16_ROWS = 16
FOX_VT_ROWS = FOX_HEAD_DIM + BF16_ROWS
NEG_BIG = -1e30
LOG2E = 1.4426950408889634

GC_LANE = 0
BETA_LANE = 4
CF_LANE = 8
GATE_ROWS = 16

VMEM_LIMIT = 56 * 1024 * 1024

TM_IN = 512
TP_GDN = 512
BG_GDN = 4
TQ_FOX = 256
HEADS_FOX = 8
TM_POST = 512
TF_POST = 256
W_STEPS_POST = 8


def _dot(a, b):
    return jnp.dot(a, b, preferred_element_type=F32)


def _dot_nt(a, b):
    return lax.dot_general(a, b, (((1,), (1,)), ((), ())), preferred_element_type=F32)


def _dot_tn(a, b):
    return lax.dot_general(a, b, (((0,), (0,)), ((), ())), preferred_element_type=F32)


def _dot_inv(a, b):
    return _dot(a.astype(BF16), b.astype(BF16))


def _sigmoid(x):
    return 0.5 * jnp.tanh(0.5 * x) + 0.5


def _silu(x):
    h = 0.5 * x
    return h * jnp.tanh(h) + h


def _rms_scale(x):
    return lax.rsqrt(jnp.mean(x * x, axis=-1, keepdims=True) + EPS)


def _const_spec(shape):
    nd = len(shape)
    return pl.BlockSpec(shape, lambda *_: (0,) * nd, pipeline_mode=pl.Buffered(1))


def _inproj_kernel(x_ref, nw_ref, wg_ref, wz_ref, wf_ref, wfvt_ref, ws_ref, conv_ref, prm_ref, sel_ref,
                   gq_ref, gk_ref, gv_ref, gz_ref, fqa_ref, fka_ref, fvt_ref, gates_ref, gates_t_ref,
                   buf_ref, carry_ref, *, tm, tiles_per_seq):
    i = pl.program_id(0)
    seq_start = (i % tiles_per_seq) == 0
    halo = SUBLANES

    x = x_ref[...]
    xn = (x * _rms_scale(x) * nw_ref[...]).astype(BF16)

    @pl.when(seq_start)
    def _():
        buf_ref[0:halo, :] = jnp.zeros((halo, buf_ref.shape[1]), F32)
        carry_ref[...] = jnp.zeros_like(carry_ref)

    @pl.when(jnp.logical_not(seq_start))
    def _():
        buf_ref[0:halo, :] = buf_ref[tm:tm + halo, :]

    buf_ref[halo:halo + tm, :] = _dot(xn, wg_ref[...])

    out_refs = (gq_ref, gk_ref, gv_ref)
    for j in range(3 * GDN_HEADS):
        cols = slice(j * LANES, (j + 1) * LANES)
        acc = buf_ref[halo:halo + tm, cols] * conv_ref[CONV_WIDTH - 1:CONV_WIDTH, cols]
        for t in range(CONV_WIDTH - 1):
            off = halo - (CONV_WIDTH - 1) + t
            acc = acc + buf_ref[off:off + tm, cols] * conv_ref[t:t + 1, cols]
        y = _silu(acc)
        kind, head = divmod(j, GDN_HEADS)
        if kind < 2:
            y = y * lax.rsqrt(jnp.sum(y * y, axis=-1, keepdims=True) + EPS)
        if kind == 0:
            y = y * (GDN_HEAD_DIM ** -0.5)
        out_refs[kind][:, head * LANES:(head + 1) * LANES] = y

    gz_ref[...] = _dot(xn, wz_ref[...])

    vt = _dot_nt(wfvt_ref[...], xn).astype(BF16)
    for h in range(FOX_HEADS):
        r0 = h * FOX_VT_ROWS
        fvt_ref[0, r0:r0 + FOX_HEAD_DIM, :] = vt[h * FOX_HEAD_DIM:(h + 1) * FOX_HEAD_DIM]
        fvt_ref[0, r0 + FOX_HEAD_DIM:r0 + FOX_VT_ROWS, :] = jnp.ones((FOX_VT_ROWS - FOX_HEAD_DIM, tm), BF16)

    z = _dot(xn, ws_ref[...]) + prm_ref[0:1, :]
    lane = lax.broadcasted_iota(jnp.int32, (tm, LANES), 1)
    row = lax.broadcasted_iota(jnp.int32, (tm, LANES), 0)
    is_gc = lane < BETA_LANE
    is_cf = (lane >= CF_LANE) & (lane < CF_LANE + FOX_HEADS)
    soft = jnp.log1p(jnp.exp(-jnp.abs(z)))
    g_decay = -jnp.exp(prm_ref[1:2, :]) * (jnp.maximum(z, 0.0) + soft)
    log_f = -(jnp.maximum(-z, 0.0) + soft)
    val = jnp.where(is_gc, g_decay, jnp.where(lane < CF_LANE, _sigmoid(z), jnp.where(is_cf, log_f, 0.0)))

    pos = jnp.where(is_gc, row % CHUNK, row)
    summed = is_gc | is_cf
    d = 1
    while d < tm:
        shifted = pltpu.roll(val, d, axis=0)
        val = val + jnp.where(summed & (pos >= d), shifted, 0.0)
        d *= 2
    val = val + jnp.where(is_cf, carry_ref[...], 0.0)
    carry_ref[...] = val[tm - 1:tm, :]

    gates_ref[...] = val
    gates_t_ref[0] = val.T[0:GATE_ROWS, :]

    f_qk = _dot(xn, wf_ref[...])
    c2 = jnp.where(is_cf, val * LOG2E, 0.0)
    c_hi = c2.astype(BF16).astype(F32)
    c_mid = (c2 - c_hi).astype(BF16).astype(F32)
    c_lo = c2 - c_hi - c_mid
    parts = (jnp.where(lane == 0, 1.0, c_hi) + pltpu.roll(c_mid, FOX_HEADS, axis=1)
             + pltpu.roll(c_lo, 2 * FOX_HEADS, axis=1))
    bias = _dot(parts.astype(BF16), sel_ref[...])
    for h in range(FOX_HEADS):
        src, half = divmod(h, 2)
        in_half = (lane >= half * FOX_HEAD_DIM) & (lane < (half + 1) * FOX_HEAD_DIM)
        q_slab = f_qk[:, src * LANES:(src + 1) * LANES] * (FOX_HEAD_DIM ** -0.5 * LOG2E)
        k_slab = f_qk[:, FOX_WIDTH + src * LANES:FOX_WIDTH + (src + 1) * LANES]
        q_bias = bias[:, h * LANES:(h + 1) * LANES]
        k_bias = bias[:, (FOX_HEADS + h) * LANES:(FOX_HEADS + h + 1) * LANES]
        fqa_ref[:, h * LANES:(h + 1) * LANES] = jnp.where(in_half, q_slab, q_bias).astype(BF16)
        fka_ref[:, h * LANES:(h + 1) * LANES] = jnp.where(in_half, k_slab, k_bias).astype(BF16)


def _bias_selector():
    sel = np.zeros((LANES, 2 * FOX_HEADS * LANES), np.float32)
    for h in range(FOX_HEADS):
        base = (1 - h % 2) * FOX_HEAD_DIM
        q0 = h * LANES + base
        k0 = (FOX_HEADS + h) * LANES + base
        for part in range(3):
            src = CF_LANE + part * FOX_HEADS + h
            sel[src, q0 + part] = 1.0
            sel[0, q0 + 3 + part] = -1.0
            sel[0, k0 + part] = 1.0
            sel[src, k0 + 3 + part] = 1.0
    return jnp.asarray(sel, BF16)


def _inproj(x2, nw, wg, wz, wf, wfvt, ws, conv_w, prm, *, batch, seq):
    sel = _bias_selector()
    m, d = x2.shape
    tm = TM_IN
    tiles_per_seq = seq // tm
    grid = (m // tm,)
    tok = lambda w: pl.BlockSpec((tm, w), lambda i: (i, 0))
    seq_t = lambda rows: pl.BlockSpec((1, rows, tm), lambda i: (i // tiles_per_seq, 0, i % tiles_per_seq))
    out_shape = (
        jax.ShapeDtypeStruct((m, GDN_WIDTH), F32),
        jax.ShapeDtypeStruct((m, GDN_WIDTH), F32),
        jax.ShapeDtypeStruct((m, GDN_WIDTH), F32),
        jax.ShapeDtypeStruct((m, GDN_WIDTH), F32),
        jax.ShapeDtypeStruct((m, FOX_HEADS * LANES), BF16),
        jax.ShapeDtypeStruct((m, FOX_HEADS * LANES), BF16),
        jax.ShapeDtypeStruct((batch, FOX_HEADS * FOX_VT_ROWS, seq), BF16),
        jax.ShapeDtypeStruct((m, LANES), F32),
        jax.ShapeDtypeStruct((batch, GATE_ROWS, seq), F32),
    )
    out_specs = (
        tok(GDN_WIDTH), tok(GDN_WIDTH), tok(GDN_WIDTH), tok(GDN_WIDTH),
        tok(FOX_HEADS * LANES), tok(FOX_HEADS * LANES), seq_t(FOX_HEADS * FOX_VT_ROWS), tok(LANES),
        seq_t(GATE_ROWS),
    )
    in_specs = [
        tok(d), _const_spec(nw.shape), _const_spec(wg.shape), _const_spec(wz.shape), _const_spec(wf.shape),
        _const_spec(wfvt.shape), _const_spec(ws.shape), _const_spec(conv_w.shape), _const_spec(prm.shape),
        _const_spec(sel.shape),
    ]
    return pl.pallas_call(
        functools.partial(_inproj_kernel, tm=tm, tiles_per_seq=tiles_per_seq),
        grid=grid, in_specs=in_specs, out_specs=out_specs, out_shape=out_shape,
        scratch_shapes=[pltpu.VMEM((tm + SUBLANES, 3 * GDN_WIDTH), F32), pltpu.VMEM((1, LANES), F32)],
        compiler_params=pltpu.CompilerParams(dimension_semantics=("arbitrary",), vmem_limit_bytes=VMEM_LIMIT),
        name="inproj",
    )(x2, nw, wg, wz, wf, wfvt, ws, conv_w, prm, sel)


def _gdn_prep_kernel(q_ref, k_ref, v_ref, gates_ref, gates_t_ref, u_ref, wq_ref, kdt_ref, qk_ref, *, tp):
    ri = lax.broadcasted_iota(jnp.int32, (PAIR, PAIR), 0)
    ci = lax.broadcasted_iota(jnp.int32, (PAIR, PAIR), 1)
    same_chunk = (ri // CHUNK) == (ci // CHUNK)
    lower = same_chunk & (ri >= ci)
    strict = same_chunk & (ri > ci)
    eye = (ri == ci).astype(F32)
    first_chunk_row = lax.broadcasted_iota(jnp.int32, (PAIR, LANES), 0) < CHUNK

    chains = [(pr, h) for pr in range(tp // PAIR) for h in range(GDN_HEADS)]
    rows_of = lambda pr: slice(pr * PAIR, (pr + 1) * PAIR)
    cols_of = lambda h: slice(h * GDN_HEAD_DIM, (h + 1) * GDN_HEAD_DIM)

    gts = [gates_ref[rows_of(pr), :] for pr in range(tp // PAIR)]
    gt_ts = [gates_t_ref[0, :, rows_of(pr)] for pr in range(tp // PAIR)]
    k16s, a_mats, decays, e_gcs = [], [], [], []
    for pr, h in chains:
        k = k_ref[rows_of(pr), cols_of(h)]
        gc = gts[pr][:, GC_LANE + h:GC_LANE + h + 1]
        beta = gts[pr][:, BETA_LANE + h:BETA_LANE + h + 1]
        gc_row = gt_ts[pr][GC_LANE + h:GC_LANE + h + 1, :]
        decay = jnp.where(lower, jnp.exp(jnp.where(lower, gc - gc_row, 0.0)), 0.0)
        k16 = k.astype(BF16)
        a_mats.append(jnp.where(strict, _dot_nt((k * beta).astype(BF16), k16) * decay, 0.0))
        k16s.append(k16)
        decays.append(decay)
        e_gcs.append(jnp.exp(gc))

    invs = [eye - a for a in a_mats]
    pows = a_mats
    for _ in range(5):
        pows = [_dot_inv(a, a) for a in pows]
        invs = [_dot_inv(t, eye + a) for t, a in zip(invs, pows)]

    for n, (pr, h) in enumerate(chains):
        rows, cols = rows_of(pr), cols_of(h)
        gc = gts[pr][:, GC_LANE + h:GC_LANE + h + 1]
        beta = gts[pr][:, BETA_LANE + h:BETA_LANE + h + 1]
        q = q_ref[rows, cols]
        k = k_ref[rows, cols]
        inv16 = invs[n].astype(BF16)
        u_ref[rows, cols] = _dot(inv16, (v_ref[rows, cols] * beta).astype(BF16))
        w = _dot(inv16, ((k * beta) * e_gcs[n]).astype(BF16)).astype(BF16)
        qe = (q * e_gcs[n]).astype(BF16)
        qk = jnp.where(lower, _dot_nt(q.astype(BF16), k16s[n]) * decays[n], 0.0).astype(BF16)
        g_last = jnp.where(first_chunk_row, gts[pr][CHUNK - 1:CHUNK, :], gts[pr][PAIR - 1:PAIR, :])
        kd = k * jnp.exp(g_last[:, GC_LANE + h:GC_LANE + h + 1] - gc)
        kdt_ref[0, h, :, rows] = kd.T.astype(BF16)
        for c in range(PAIR // CHUNK):
            sub = slice(c * CHUNK, (c + 1) * CHUNK)
            r0 = 2 * (pr * PAIR + c * CHUNK)
            wq_ref[r0:r0 + CHUNK, cols] = w[sub]
            wq_ref[r0 + CHUNK:r0 + 2 * CHUNK, cols] = qe[sub]
            qk_ref[pr * PAIR + c * CHUNK:pr * PAIR + (c + 1) * CHUNK, h * CHUNK:(h + 1) * CHUNK] = qk[sub, sub]


def _gdn_scan_kernel(u_ref, wq_ref, kdt_ref, qk_ref, z_ref, gates_ref, nw_ref, o_ref, state_ref, *, bg):
    @pl.when(pl.program_id(1) == 0)
    def _():
        state_ref[...] = jnp.zeros_like(state_ref)

    nw = nw_ref[...]
    chains = [(b, h) for b in range(bg) for h in range(GDN_HEADS)]
    cols_of = lambda h: slice(h * GDN_HEAD_DIM, (h + 1) * GDN_HEAD_DIM)
    states = [state_ref[b * GDN_HEADS + h] for b, h in chains]
    for c in range(PAIR // CHUNK):
        rows = slice(c * CHUNK, (c + 1) * CHUNK)
        last = (c + 1) * CHUNK - 1
        rs = [_dot(wq_ref[b, 2 * c * CHUNK:2 * (c + 1) * CHUNK, cols_of(h)], states[n].astype(BF16))
              for n, (b, h) in enumerate(chains)]
        v_news = [(u_ref[b, rows, cols_of(h)] - rs[n][0:CHUNK]).astype(BF16) for n, (b, h) in enumerate(chains)]
        upds = [_dot(kdt_ref[b, h, :, rows], v_news[n]) for n, (b, h) in enumerate(chains)]
        states = [states[n] * jnp.exp(gates_ref[b, last:last + 1, GC_LANE + h:GC_LANE + h + 1]) + upds[n]
                  for n, (b, h) in enumerate(chains)]
        for n, (b, h) in enumerate(chains):
            o = rs[n][CHUNK:2 * CHUNK] + _dot(qk_ref[b, rows, h * CHUNK:(h + 1) * CHUNK], v_news[n])
            o = o * lax.rsqrt(jnp.mean(o * o, axis=-1, keepdims=True) + EPS) * nw
            zz = z_ref[b, rows, cols_of(h)]
            o_ref[b, rows, cols_of(h)] = (o * _silu(zz)).astype(o_ref.dtype)
    for n, (b, h) in enumerate(chains):
        state_ref[b * GDN_HEADS + h] = states[n]


def _gdn(gq, gk, gv, gz, gates, gates_t, nw, *, batch, seq):
    m = gq.shape[0]
    tp = TP_GDN
    steps = seq // tp
    tok = lambda w: pl.BlockSpec((tp, w), lambda b, j: (b * steps + j, 0))
    u, wq, kdt, qk = pl.pallas_call(
        functools.partial(_gdn_prep_kernel, tp=tp),
        grid=(batch, steps),
        in_specs=[tok(GDN_WIDTH), tok(GDN_WIDTH), tok(GDN_WIDTH), tok(LANES),
                  pl.BlockSpec((1, GATE_ROWS, tp), lambda b, j: (b, 0, j))],
        out_specs=(tok(GDN_WIDTH),
                   pl.BlockSpec((2 * tp, GDN_WIDTH), lambda b, j: (b * steps + j, 0)),
                   pl.BlockSpec((1, GDN_HEADS, GDN_HEAD_DIM, tp), lambda b, j: (b, 0, 0, j)),
                   tok(GDN_HEADS * CHUNK)),
        out_shape=(jax.ShapeDtypeStruct((m, GDN_WIDTH), F32),
                   jax.ShapeDtypeStruct((2 * m, GDN_WIDTH), BF16),
                   jax.ShapeDtypeStruct((batch, GDN_HEADS, GDN_HEAD_DIM, seq), BF16),
                   jax.ShapeDtypeStruct((m, GDN_HEADS * CHUNK), BF16)),
        compiler_params=pltpu.CompilerParams(
            dimension_semantics=("arbitrary", "arbitrary"), vmem_limit_bytes=VMEM_LIMIT),
        name="gdn_prep",
    )(gq, gk, gv, gates, gates_t)

    bg = BG_GDN
    seq_blk = lambda rows, w: pl.BlockSpec((bg, rows, w), lambda bi, j: (bi, j, 0))
    o = pl.pallas_call(
        functools.partial(_gdn_scan_kernel, bg=bg),
        grid=(batch // bg, seq // PAIR),
        in_specs=[seq_blk(PAIR, GDN_WIDTH), seq_blk(2 * PAIR, GDN_WIDTH),
                  pl.BlockSpec((bg, GDN_HEADS, GDN_HEAD_DIM, PAIR), lambda bi, j: (bi, 0, 0, j)),
                  seq_blk(PAIR, GDN_HEADS * CHUNK), seq_blk(PAIR, GDN_WIDTH), seq_blk(PAIR, LANES),
                  _const_spec(nw.shape)],
        out_specs=seq_blk(PAIR, GDN_WIDTH),
        out_shape=jax.ShapeDtypeStruct((batch, seq, GDN_WIDTH), BF16),
        scratch_shapes=[pltpu.VMEM((bg * GDN_HEADS, GDN_HEAD_DIM, GDN_HEAD_DIM), F32)],
        compiler_params=pltpu.CompilerParams(
            dimension_semantics=("arbitrary", "arbitrary"), vmem_limit_bytes=VMEM_LIMIT),
        name="gdn_scan",
    )(u.reshape(batch, seq, GDN_WIDTH), wq.reshape(batch, 2 * seq, GDN_WIDTH), kdt,
      qk.reshape(batch, seq, GDN_HEADS * CHUNK), gz.reshape(batch, seq, GDN_WIDTH),
      gates.reshape(batch, seq, LANES), nw)
    return o.reshape(m, GDN_WIDTH)


def _fox_kernel(qa_ref, ka_ref, vt_ref, o_ref, m_ref, acc_ref, *, tq):
    i = pl.program_id(2)
    heads = qa_ref.shape[1] // LANES
    qas = [qa_ref[:, hh * LANES:(hh + 1) * LANES] for hh in range(heads)]

    m_ref[...] = jnp.full(m_ref.shape, NEG_BIG, F32)
    acc_ref[...] = jnp.zeros(acc_ref.shape, F32)

    def chunk(start, size, mask_off):
        ka = ka_ref[pl.ds(start, size), :]
        sts = [_dot_nt(ka[:, hh * LANES:(hh + 1) * LANES], qas[hh]) for hh in range(heads)]
        if mask_off is not None:
            ri = lax.broadcasted_iota(jnp.int32, (size, tq), 0)
            ci = lax.broadcasted_iota(jnp.int32, (size, tq), 1)
            keep = (ri - ci) <= mask_off
            sts = [jnp.where(keep, s, NEG_BIG) for s in sts]
        m_olds = [m_ref[hh] for hh in range(heads)]
        m_news = [jnp.maximum(m, jnp.max(s, axis=0, keepdims=True)) for m, s in zip(m_olds, sts)]
        ps = [jnp.exp2(s - m).astype(BF16) for s, m in zip(sts, m_news)]
        pvs = [_dot(vt_ref[0, hh * FOX_VT_ROWS:(hh + 1) * FOX_VT_ROWS, pl.ds(start, size)], ps[hh])
               for hh in range(heads)]
        for hh in range(heads):
            alpha = jnp.exp2(m_olds[hh] - m_news[hh])
            rows = slice(hh * FOX_VT_ROWS, (hh + 1) * FOX_VT_ROWS)
            m_ref[hh] = m_news[hh]
            acc_ref[rows, :] = alpha * acc_ref[rows, :] + pvs[hh]

    def body(j, carry):
        chunk(pl.multiple_of(j * 2 * tq, 2 * tq), 2 * tq, None)
        return carry

    lax.fori_loop(0, i // 2, body, 0)

    @pl.when(i % 2 == 0)
    def _():
        chunk(pl.multiple_of(i * tq, tq), tq, 0)

    @pl.when(i % 2 == 1)
    def _():
        chunk(pl.multiple_of((i - 1) * tq, tq), 2 * tq, tq)

    out_t = jnp.concatenate(
        [acc_ref[hh * FOX_VT_ROWS:hh * FOX_VT_ROWS + FOX_HEAD_DIM, :]
         / acc_ref[hh * FOX_VT_ROWS + FOX_HEAD_DIM:hh * FOX_VT_ROWS + FOX_HEAD_DIM + 1, :]
         for hh in range(heads)], axis=0)
    o_ref[...] = out_t.T.astype(o_ref.dtype)


def _fox(fqa, fka, fvt, *, batch, seq):
    m = fqa.shape[0]
    tq = TQ_FOX
    steps = seq // tq
    heads = HEADS_FOX
    return pl.pallas_call(
        functools.partial(_fox_kernel, tq=tq),
        grid=(batch, FOX_HEADS // heads, steps),
        in_specs=[pl.BlockSpec((tq, heads * LANES), lambda b, p, i: (b * steps + i, p)),
                  pl.BlockSpec((seq, heads * LANES), lambda b, p, i: (b, p)),
                  pl.BlockSpec((1, heads * FOX_VT_ROWS, seq), lambda b, p, i: (b, p, 0))],
        out_specs=pl.BlockSpec((tq, heads * FOX_HEAD_DIM), lambda b, p, i: (b * steps + i, p)),
        out_shape=jax.ShapeDtypeStruct((m, FOX_WIDTH), BF16),
        scratch_shapes=[pltpu.VMEM((heads, 1, tq), F32), pltpu.VMEM((heads * FOX_VT_ROWS, tq), F32)],
        compiler_params=pltpu.CompilerParams(
            dimension_semantics=("arbitrary", "arbitrary", "arbitrary"), vmem_limit_bytes=VMEM_LIMIT),
        name="fox",
    )(fqa, fka, fvt)


def _post_kernel(x_ref, og_ref, of_ref, p_ref, wo_c, wgu_c, wdown_c, wpg_c, wpp_c,
                 ffn_nw_ref, ple_nw_ref, fin_nw_ref, out_ref,
                 wo_ref, wgate_ref, wup_ref, wdown_ref, wpg_ref, wpp_ref, act_ref, *, w_steps, tf):
    step = pl.program_id(0)
    d_ff = wgate_ref.shape[1]

    @pl.when(step < w_steps)
    def _():
        def put(dst_ref, chunk):
            rows = chunk.shape[0]
            dst_ref[pl.ds(pl.multiple_of(step * rows, rows), rows), :] = chunk.astype(BF16)

        put(wo_ref, wo_c[...])
        put(wgate_ref, wgu_c[:, 0:d_ff])
        put(wup_ref, wgu_c[:, d_ff:2 * d_ff])
        put(wdown_ref, wdown_c[...])
        put(wpg_ref, wpg_c[...])
        put(wpp_ref, wpp_c[...])

    @pl.when(step >= w_steps)
    def _():
        h = (x_ref[...] + _dot(og_ref[...], wo_ref[0:GDN_WIDTH, :])
             + _dot(of_ref[...], wo_ref[GDN_WIDTH:GDN_WIDTH + FOX_WIDTH, :]))

        hn = (h * _rms_scale(h) * ffn_nw_ref[...]).astype(BF16)
        for c in range(d_ff // tf):
            cols = slice(c * tf, (c + 1) * tf)
            gate = _dot(hn, wgate_ref[:, cols])
            up = _dot(hn, wup_ref[:, cols])
            act_ref[:, cols] = (_silu(gate) * up).astype(BF16)
        h = h + _dot(act_ref[...], wdown_ref[...])

        hn = (h * _rms_scale(h) * ple_nw_ref[...]).astype(BF16)
        ple_gate = _sigmoid(_dot(hn, wpg_ref[...]))
        h = h + ple_gate * _dot(p_ref[...].astype(BF16), wpp_ref[...])

        out_ref[...] = h * _rms_scale(h) * fin_nw_ref[...]


def _post(x2, o_g, o_f, p2, w_out, w_gate_up, w_down, w_ple_gate, w_ple_proj, ffn_nw, ple_nw, fin_nw):
    m, d = x2.shape
    tm = TM_POST
    w_steps = W_STEPS_POST
    d_ff = w_down.shape[0]
    tok = lambda w: pl.BlockSpec((tm, w), lambda s: (jnp.maximum(s - w_steps, 0), 0))
    wchunk = lambda w: pl.BlockSpec((w.shape[0] // w_steps, w.shape[1]), lambda s: (jnp.minimum(s, w_steps - 1), 0))
    weights = (w_out, w_gate_up, w_down, w_ple_gate, w_ple_proj)
    norms = (ffn_nw, ple_nw, fin_nw)
    in_specs = ([tok(d), tok(GDN_WIDTH), tok(FOX_WIDTH), tok(p2.shape[1])] + [wchunk(w) for w in weights]
                + [_const_spec(w.shape) for w in norms])
    return pl.pallas_call(
        functools.partial(_post_kernel, w_steps=w_steps, tf=TF_POST),
        grid=(w_steps + m // tm,), in_specs=in_specs, out_specs=tok(d),
        out_shape=jax.ShapeDtypeStruct((m, d), F32),
        scratch_shapes=[pltpu.VMEM(w_out.shape, BF16), pltpu.VMEM((d, d_ff), BF16), pltpu.VMEM((d, d_ff), BF16),
                        pltpu.VMEM(w_down.shape, BF16), pltpu.VMEM(w_ple_gate.shape, BF16),
                        pltpu.VMEM(w_ple_proj.shape, BF16), pltpu.VMEM((tm, d_ff), BF16)],
        compiler_params=pltpu.CompilerParams(dimension_semantics=("arbitrary",), vmem_limit_bytes=VMEM_LIMIT),
        name="post",
    )(x2, o_g, o_f, p2, *weights, *norms)


def _layer(x, p_i, attn_norm_w, w_in, conv_w, a_log, dt_bias, gdn_norm_w, fox_f_bias, w_out,
           ffn_norm_w, w_gate_up, w_down, ple_norm_w, w_ple_gate, w_ple_proj, out_norm_w):
    batch, seq, d = x.shape
    m = batch * seq
    assert seq % TM_IN == 0 and seq % TP_GDN == 0 and seq % TQ_FOX == 0 and m % TM_POST == 0
    assert batch % BG_GDN == 0
    row = lambda w: w.reshape(1, -1).astype(F32)

    c0 = 3 * GDN_WIDTH
    c1 = c0 + GDN_WIDTH
    c2 = c1 + 2 * GDN_HEADS
    c3 = c2 + 3 * FOX_WIDTH
    wg = w_in[:, :c0].astype(BF16)
    wz = w_in[:, c0:c1].astype(BF16)
    wf = w_in[:, c2:c2 + 2 * FOX_WIDTH].astype(BF16)
    wfvt = w_in[:, c2 + 2 * FOX_WIDTH:c3].T.astype(BF16)
    n_small = 2 * GDN_HEADS + FOX_HEADS
    ws = jnp.concatenate([w_in[:, c1:c2], w_in[:, c3:]], axis=1)
    ws = jnp.pad(ws, ((0, 0), (0, LANES - n_small))).astype(BF16)
    zeros4 = jnp.zeros((GDN_HEADS,), F32)
    pad = jnp.zeros((LANES - n_small,), F32)
    prm = jnp.stack([
        jnp.concatenate([dt_bias.astype(F32), zeros4, fox_f_bias.astype(F32), pad]),
        jnp.concatenate([a_log.astype(F32), zeros4, jnp.zeros((FOX_HEADS,), F32), pad]),
    ])
    prm = jnp.pad(prm, ((0, SUBLANES - prm.shape[0]), (0, 0)))

    x2 = x.reshape(m, d)
    gq, gk, gv, gz, fqa, fka, fvt, gates, gates_t = _inproj(
        x2, row(attn_norm_w), wg, wz, wf, wfvt, ws, conv_w.astype(F32), prm, batch=batch, seq=seq)
    o_g = _gdn(gq, gk, gv, gz, gates, gates_t, row(gdn_norm_w), batch=batch, seq=seq)
    o_f = _fox(fqa, fka, fvt, batch=batch, seq=seq)

    return _post(
        x2, o_g, o_f, p_i.reshape(m, -1),
        w_out.astype(F32), w_gate_up.astype(F32), w_down.astype(F32), w_ple_gate.astype(F32),
        w_ple_proj.astype(F32), row(ffn_norm_w), row(ple_norm_w), row(out_norm_w),
    ).reshape(batch, seq, d)


def kernel(x, p, attn_norm_w, w_in, conv_w, a_log, dt_bias, gdn_norm_w, fox_f_bias, w_out, ffn_norm_w,
           w_gate_up, w_down, ple_norm_w, w_ple_gate, w_ple_proj, final_norm_w):
    assert p.shape[0] == 1, "single-layer problem"
    return _layer(x, p[0], attn_norm_w[0], w_in[0], conv_w[0], a_log[0], dt_bias[0], gdn_norm_w[0],
                  fox_f_bias[0], w_out[0], ffn_norm_w[0], w_gate_up[0], w_down[0], ple_norm_w[0],
                  w_ple_gate[0], w_ple_proj[0], final_norm_w)
```

```python
import functools

import jax
import jax.numpy as jnp
import numpy as np
from jax import lax
from jax.experimental import pallas as pl
from jax.experimental.pallas import tpu as pltpu

F32 = jnp.float32
BF16 = jnp.bfloat16

EPS = 1e-6
GDN_HEADS = 4
GDN_HEAD_DIM = 128
GDN_WIDTH = GDN_HEADS * GDN_HEAD_DIM
FOX_HEADS = 8
FOX_HEAD_DIM = 64
FOX_WIDTH = FOX_HEADS * FOX_HEAD_DIM
CONV_WIDTH = 4
CHUNK = 64
PAIR = 2 * CHUNK
LANES = 128
SUBLANES = 8
BF16_ROWS = 16
FOX_VT_ROWS = FOX_HEAD_DIM + BF16_ROWS
NEG_BIG = -1e30
LOG2E = 1.4426950408889634

GC_LANE = 0
BETA_LANE = 4
CF_LANE = 8
GATE_ROWS = 16

VMEM_LIMIT = 56 * 1024 * 1024

TM_IN = 512
TS_IN = 512
BG_GDN = 4
TQ_FOX = 256
HEADS_FOX = 8
TM_POST = 512
TF_POST = 256
W_STEPS_POST = 8


def _dot(a, b):
    return jnp.dot(a, b, preferred_element_type=F32)


def _dot_nt(a, b):
    return lax.dot_general(a, b, (((1,), (1,)), ((), ())), preferred_element_type=F32)


def _dot_tn(a, b):
    return lax.dot_general(a, b, (((0,), (0,)), ((), ())), preferred_element_type=F32)


def _dot_inv(a, b):
    return _dot(a.astype(BF16), b.astype(BF16))


def _sigmoid(x):
    return 0.5 * jnp.tanh(0.5 * x) + 0.5


def _silu(x):
    h = 0.5 * x
    return h * jnp.tanh(h) + h


def _rms_scale(x):
    return lax.rsqrt(jnp.mean(x * x, axis=-1, keepdims=True) + EPS)


def _const_spec(shape):
    nd = len(shape)
    return pl.BlockSpec(shape, lambda *_: (0,) * nd, pipeline_mode=pl.Buffered(1))


def _inproj_kernel(x_ref, nw_ref, wg_ref, wz_ref, wf_ref, wfvt_ref, ws_ref, conv_ref, prm_ref, sel_ref,
                   gq_ref, gk_ref, gv_ref, gz_ref, fqa_ref, fka_ref, fvt_ref, gates_ref, gates_t_ref,
                   buf_ref, carry_ref, *, tm, ts, tiles_per_seq):
    i = pl.program_id(0)
    seq_start = (i % tiles_per_seq) == 0
    halo = SUBLANES

    @pl.when(seq_start)
    def _():
        buf_ref[0:halo, :] = jnp.zeros((halo, buf_ref.shape[1]), F32)
        carry_ref[...] = jnp.zeros_like(carry_ref)

    @pl.when(jnp.logical_not(seq_start))
    def _():
        buf_ref[0:halo, :] = buf_ref[tm:tm + halo, :]

    lane = lax.broadcasted_iota(jnp.int32, (ts, LANES), 1)
    row = lax.broadcasted_iota(jnp.int32, (ts, LANES), 0)
    is_gc = lane < BETA_LANE
    is_cf = (lane >= CF_LANE) & (lane < CF_LANE + FOX_HEADS)
    pos = jnp.where(is_gc, row % CHUNK, row)
    summed = is_gc | is_cf
    out_refs = (gq_ref, gk_ref, gv_ref)

    for r0 in range(0, tm, ts):
        rows = slice(r0, r0 + ts)
        x = x_ref[rows, :]
        xn = (x * _rms_scale(x) * nw_ref[...]).astype(BF16)

        buf_ref[halo + r0:halo + r0 + ts, :] = _dot(xn, wg_ref[...])
        for j in range(3 * GDN_HEADS):
            cols = slice(j * LANES, (j + 1) * LANES)
            acc = buf_ref[halo + r0:halo + r0 + ts, cols] * conv_ref[CONV_WIDTH - 1:CONV_WIDTH, cols]
            for t in range(CONV_WIDTH - 1):
                off = halo + r0 - (CONV_WIDTH - 1) + t
                acc = acc + buf_ref[off:off + ts, cols] * conv_ref[t:t + 1, cols]
            y = _silu(acc)
            kind, head = divmod(j, GDN_HEADS)
            if kind < 2:
                y = y * lax.rsqrt(jnp.sum(y * y, axis=-1, keepdims=True) + EPS)
            if kind == 0:
                y = y * (GDN_HEAD_DIM ** -0.5)
            out_refs[kind][rows, head * LANES:(head + 1) * LANES] = y

        gz_ref[rows, :] = _dot(xn, wz_ref[...])

        vt = _dot_nt(wfvt_ref[...], xn).astype(BF16)
        for h in range(FOX_HEADS):
            v0 = h * FOX_VT_ROWS
            fvt_ref[0, v0:v0 + FOX_HEAD_DIM, rows] = vt[h * FOX_HEAD_DIM:(h + 1) * FOX_HEAD_DIM]
            fvt_ref[0, v0 + FOX_HEAD_DIM:v0 + FOX_VT_ROWS, rows] = jnp.ones((FOX_VT_ROWS - FOX_HEAD_DIM, ts), BF16)

        z = _dot(xn, ws_ref[...]) + prm_ref[0:1, :]
        soft = jnp.log1p(jnp.exp(-jnp.abs(z)))
        g_decay = -jnp.exp(prm_ref[1:2, :]) * (jnp.maximum(z, 0.0) + soft)
        log_f = -(jnp.maximum(-z, 0.0) + soft)
        val = jnp.where(is_gc, g_decay, jnp.where(lane < CF_LANE, _sigmoid(z), jnp.where(is_cf, log_f, 0.0)))

        d = 1
        while d < ts:
            shifted = pltpu.roll(val, d, axis=0)
            val = val + jnp.where(summed & (pos >= d), shifted, 0.0)
            d *= 2
        val = val + jnp.where(is_cf, carry_ref[...], 0.0)
        carry_ref[...] = val[ts - 1:ts, :]

        gates_ref[rows, :] = val
        gates_t_ref[0, :, rows] = val.T[0:GATE_ROWS, :]

        f_qk = _dot(xn, wf_ref[...])
        c2 = jnp.where(is_cf, val * LOG2E, 0.0)
        c_hi = c2.astype(BF16).astype(F32)
        c_mid = (c2 - c_hi).astype(BF16).astype(F32)
        c_lo = c2 - c_hi - c_mid
        parts = (jnp.where(lane == 0, 1.0, c_hi) + pltpu.roll(c_mid, FOX_HEADS, axis=1)
                 + pltpu.roll(c_lo, 2 * FOX_HEADS, axis=1))
        bias = _dot(parts.astype(BF16), sel_ref[...])
        for h in range(FOX_HEADS):
            src, half = divmod(h, 2)
            in_half = (lane >= half * FOX_HEAD_DIM) & (lane < (half + 1) * FOX_HEAD_DIM)
            q_slab = f_qk[:, src * LANES:(src + 1) * LANES] * (FOX_HEAD_DIM ** -0.5 * LOG2E)
            k_slab = f_qk[:, FOX_WIDTH + src * LANES:FOX_WIDTH + (src + 1) * LANES]
            q_bias = bias[:, h * LANES:(h + 1) * LANES]
            k_bias = bias[:, (FOX_HEADS + h) * LANES:(FOX_HEADS + h + 1) * LANES]
            fqa_ref[rows, h * LANES:(h + 1) * LANES] = jnp.where(in_half, q_slab, q_bias).astype(BF16)
            fka_ref[rows, h * LANES:(h + 1) * LANES] = jnp.where(in_half, k_slab, k_bias).astype(BF16)


def _bias_selector():
    sel = np.zeros((LANES, 2 * FOX_HEADS * LANES), np.float32)
    for h in range(FOX_HEADS):
        base = (1 - h % 2) * FOX_HEAD_DIM
        q0 = h * LANES + base
        k0 = (FOX_HEADS + h) * LANES + base
        for part in range(3):
            src = CF_LANE + part * FOX_HEADS + h
            sel[src, q0 + part] = 1.0
            sel[0, q0 + 3 + part] = -1.0
            sel[0, k0 + part] = 1.0
            sel[src, k0 + 3 + part] = 1.0
    return jnp.asarray(sel, BF16)


def _inproj(x2, nw, wg, wz, wf, wfvt, ws, conv_w, prm, *, batch, seq):
    sel = _bias_selector()
    m, d = x2.shape
    tm = TM_IN
    tiles_per_seq = seq // tm
    grid = (m // tm,)
    tok = lambda w: pl.BlockSpec((tm, w), lambda i: (i, 0))
    seq_t = lambda rows: pl.BlockSpec((1, rows, tm), lambda i: (i // tiles_per_seq, 0, i % tiles_per_seq))
    out_shape = (
        jax.ShapeDtypeStruct((m, GDN_WIDTH), F32),
        jax.ShapeDtypeStruct((m, GDN_WIDTH), F32),
        jax.ShapeDtypeStruct((m, GDN_WIDTH), F32),
        jax.ShapeDtypeStruct((m, GDN_WIDTH), F32),
        jax.ShapeDtypeStruct((m, FOX_HEADS * LANES), BF16),
        jax.ShapeDtypeStruct((m, FOX_HEADS * LANES), BF16),
        jax.ShapeDtypeStruct((batch, FOX_HEADS * FOX_VT_ROWS, seq), BF16),
        jax.ShapeDtypeStruct((m, LANES), F32),
        jax.ShapeDtypeStruct((batch, GATE_ROWS, seq), F32),
    )
    out_specs = (
        tok(GDN_WIDTH), tok(GDN_WIDTH), tok(GDN_WIDTH), tok(GDN_WIDTH),
        tok(FOX_HEADS * LANES), tok(FOX_HEADS * LANES), seq_t(FOX_HEADS * FOX_VT_ROWS), tok(LANES),
        seq_t(GATE_ROWS),
    )
    in_specs = [
        tok(d), _const_spec(nw.shape), _const_spec(wg.shape), _const_spec(wz.shape), _const_spec(wf.shape),
        _const_spec(wfvt.shape), _const_spec(ws.shape), _const_spec(conv_w.shape), _const_spec(prm.shape),
        _const_spec(sel.shape),
    ]
    return pl.pallas_call(
        functools.partial(_inproj_kernel, tm=tm, ts=TS_IN, tiles_per_seq=tiles_per_seq),
        grid=grid, in_specs=in_specs, out_specs=out_specs, out_shape=out_shape,
        scratch_shapes=[pltpu.VMEM((tm + SUBLANES, 3 * GDN_WIDTH), F32), pltpu.VMEM((1, LANES), F32)],
        compiler_params=pltpu.CompilerParams(dimension_semantics=("arbitrary",), vmem_limit_bytes=VMEM_LIMIT),
        name="inproj",
    )(x2, nw, wg, wz, wf, wfvt, ws, conv_w, prm, sel)


def _gdn_kernel(q_ref, k_ref, v_ref, z_ref, gates_ref, gates_t_ref, nw_ref, o_ref, state_ref, *, bg):
    @pl.when(pl.program_id(1) == 0)
    def _():
        state_ref[...] = jnp.zeros_like(state_ref)

    ri = lax.broadcasted_iota(jnp.int32, (PAIR, PAIR), 0)
    ci = lax.broadcasted_iota(jnp.int32, (PAIR, PAIR), 1)
    same_chunk = (ri // CHUNK) == (ci // CHUNK)
    lower = same_chunk & (ri >= ci)
    strict = same_chunk & (ri > ci)
    eye = (ri == ci).astype(F32)
    first_chunk_row = lax.broadcasted_iota(jnp.int32, (PAIR, LANES), 0) < CHUNK
    nw = nw_ref[...]

    chains = [(b, h) for b in range(bg) for h in range(GDN_HEADS)]
    cols_of = lambda h: slice(h * GDN_HEAD_DIM, (h + 1) * GDN_HEAD_DIM)
    col_of = lambda gt, lane: gt[:, lane:lane + 1]

    gts = [gates_ref[b] for b in range(bg)]
    gt_ts = [gates_t_ref[b] for b in range(bg)]
    k16s, a_mats, decays, e_gcs = [], [], [], []
    for b, h in chains:
        k = k_ref[b, :, cols_of(h)]
        gc = col_of(gts[b], GC_LANE + h)
        gc_row = gt_ts[b][GC_LANE + h:GC_LANE + h + 1, :]
        decay = jnp.where(lower, jnp.exp(jnp.where(lower, gc - gc_row, 0.0)), 0.0)
        k16 = k.astype(BF16)
        a_mats.append(jnp.where(strict, _dot_nt((k * col_of(gts[b], BETA_LANE + h)).astype(BF16), k16) * decay, 0.0))
        k16s.append(k16)
        decays.append(decay)
        e_gcs.append(jnp.exp(gc))

    invs = [eye - a for a in a_mats]
    pows = a_mats
    for _ in range(5):
        pows = [_dot_inv(a, a) for a in pows]
        invs = [_dot_inv(t, eye + a) for t, a in zip(invs, pows)]

    us, wqs, qks, kdts, decs = [], [], [], [], []
    for n, (b, h) in enumerate(chains):
        gc = col_of(gts[b], GC_LANE + h)
        beta = col_of(gts[b], BETA_LANE + h)
        q = q_ref[b, :, cols_of(h)]
        k = k_ref[b, :, cols_of(h)]
        inv16 = invs[n].astype(BF16)
        us.append(_dot(inv16, (v_ref[b, :, cols_of(h)] * beta).astype(BF16)))
        w = _dot(inv16, ((k * beta) * e_gcs[n]).astype(BF16)).astype(BF16)
        qe = (q * e_gcs[n]).astype(BF16)
        wqs.append([jnp.concatenate([w[c * CHUNK:(c + 1) * CHUNK], qe[c * CHUNK:(c + 1) * CHUNK]], axis=0)
                    for c in range(PAIR // CHUNK)])
        qks.append(jnp.where(lower, _dot_nt(q.astype(BF16), k16s[n]) * decays[n], 0.0).astype(BF16))
        g_last = jnp.where(first_chunk_row, gts[b][CHUNK - 1:CHUNK, :], gts[b][PAIR - 1:PAIR, :])
        g_last = col_of(g_last, GC_LANE + h)
        kdts.append((k * jnp.exp(g_last - gc)).T.astype(BF16))
        decs.append(jnp.exp(g_last))

    states = [state_ref[b * GDN_HEADS + h] for b, h in chains]
    zeros = jnp.zeros((CHUNK, GDN_HEAD_DIM), BF16)
    for c in range(PAIR // CHUNK):
        rows = slice(c * CHUNK, (c + 1) * CHUNK)
        rs = [_dot(wqs[n][c], states[n].astype(BF16)) for n in range(len(chains))]
        v_news = [(us[n][rows] - rs[n][0:CHUNK]).astype(BF16) for n in range(len(chains))]
        rhs = [jnp.concatenate([v, zeros] if c == 0 else [zeros, v], axis=0) for v in v_news]
        upds = [_dot(kdts[n], rhs[n]) for n in range(len(chains))]
        states = [states[n] * decs[n][(c + 1) * CHUNK - 1:(c + 1) * CHUNK, :] + upds[n] for n in range(len(chains))]
        for n, (b, h) in enumerate(chains):
            o = rs[n][CHUNK:2 * CHUNK] + _dot(qks[n][rows], rhs[n])
            o = o * lax.rsqrt(jnp.mean(o * o, axis=-1, keepdims=True) + EPS) * nw
            o_ref[b, rows, cols_of(h)] = (o * _silu(z_ref[b, rows, cols_of(h)])).astype(o_ref.dtype)
    for n, (b, h) in enumerate(chains):
        state_ref[b * GDN_HEADS + h] = states[n]


def _gdn(gq, gk, gv, gz, gates, gates_t, nw, *, batch, seq):
    m = gq.shape[0]
    bg = BG_GDN
    seq3 = lambda a: a.reshape(batch, seq, a.shape[-1])
    blk = lambda w: pl.BlockSpec((bg, PAIR, w), lambda bi, j: (bi, j, 0))
    o = pl.pallas_call(
        functools.partial(_gdn_kernel, bg=bg),
        grid=(batch // bg, seq // PAIR),
        in_specs=[blk(GDN_WIDTH), blk(GDN_WIDTH), blk(GDN_WIDTH), blk(GDN_WIDTH), blk(LANES),
                  pl.BlockSpec((bg, GATE_ROWS, PAIR), lambda bi, j: (bi, 0, j)), _const_spec(nw.shape)],
        out_specs=blk(GDN_WIDTH),
        out_shape=jax.ShapeDtypeStruct((batch, seq, GDN_WIDTH), BF16),
        scratch_shapes=[pltpu.VMEM((bg * GDN_HEADS, GDN_HEAD_DIM, GDN_HEAD_DIM), F32)],
        compiler_params=pltpu.CompilerParams(
            dimension_semantics=("arbitrary", "arbitrary"), vmem_limit_bytes=VMEM_LIMIT),
        name="gdn",
    )(seq3(gq), seq3(gk), seq3(gv), seq3(gz), seq3(gates), gates_t, nw)
    return o.reshape(m, GDN_WIDTH)


def _fox_kernel(qa_ref, ka_ref, vt_ref, o_ref, m_ref, acc_ref, *, tq):
    i = pl.program_id(2)
    heads = qa_ref.shape[1] // LANES
    qas = [qa_ref[:, hh * LANES:(hh + 1) * LANES] for hh in range(heads)]

    m_ref[...] = jnp.full(m_ref.shape, NEG_BIG, F32)
    acc_ref[...] = jnp.zeros(acc_ref.shape, F32)

    def chunk(start, size, mask_off):
        ka = ka_ref[pl.ds(start, size), :]
        sts = [_dot_nt(ka[:, hh * LANES:(hh + 1) * LANES], qas[hh]) for hh in range(heads)]
        if mask_off is not None:
            ri = lax.broadcasted_iota(jnp.int32, (size, tq), 0)
            ci = lax.broadcasted_iota(jnp.int32, (size, tq), 1)
            keep = (ri - ci) <= mask_off
            sts = [jnp.where(keep, s, NEG_BIG) for s in sts]
        m_olds = [m_ref[hh] for hh in range(heads)]
        m_news = [jnp.maximum(m, jnp.max(s, axis=0, keepdims=True)) for m, s in zip(m_olds, sts)]
        ps = [jnp.exp2(s - m).astype(BF16) for s, m in zip(sts, m_news)]
        pvs = [_dot(vt_ref[0, hh * FOX_VT_ROWS:(hh + 1) * FOX_VT_ROWS, pl.ds(start, size)], ps[hh])
               for hh in range(heads)]
        for hh in range(heads):
            alpha = jnp.exp2(m_olds[hh] - m_news[hh])
            rows = slice(hh * FOX_VT_ROWS, (hh + 1) * FOX_VT_ROWS)
            m_ref[hh] = m_news[hh]
            acc_ref[rows, :] = alpha * acc_ref[rows, :] + pvs[hh]

    def body(j, carry):
        chunk(pl.multiple_of(j * 2 * tq, 2 * tq), 2 * tq, None)
        return carry

    lax.fori_loop(0, i // 2, body, 0)

    @pl.when(i % 2 == 0)
    def _():
        chunk(pl.multiple_of(i * tq, tq), tq, 0)

    @pl.when(i % 2 == 1)
    def _():
        chunk(pl.multiple_of((i - 1) * tq, tq), 2 * tq, tq)

    out_t = jnp.concatenate(
        [acc_ref[hh * FOX_VT_ROWS:hh * FOX_VT_ROWS + FOX_HEAD_DIM, :]
         / acc_ref[hh * FOX_VT_ROWS + FOX_HEAD_DIM:hh * FOX_VT_ROWS + FOX_HEAD_DIM + 1, :]
         for hh in range(heads)], axis=0)
    o_ref[...] = out_t.T.astype(o_ref.dtype)


def _fox(fqa, fka, fvt, *, batch, seq):
    m = fqa.shape[0]
    tq = TQ_FOX
    steps = seq // tq
    heads = HEADS_FOX
    return pl.pallas_call(
        functools.partial(_fox_kernel, tq=tq),
        grid=(batch, FOX_HEADS // heads, steps),
        in_specs=[pl.BlockSpec((tq, heads * LANES), lambda b, p, i: (b * steps + i, p)),
                  pl.BlockSpec((seq, heads * LANES), lambda b, p, i: (b, p)),
                  pl.BlockSpec((1, heads * FOX_VT_ROWS, seq), lambda b, p, i: (b, p, 0))],
        out_specs=pl.BlockSpec((tq, heads * FOX_HEAD_DIM), lambda b, p, i: (b * steps + i, p)),
        out_shape=jax.ShapeDtypeStruct((m, FOX_WIDTH), BF16),
        scratch_shapes=[pltpu.VMEM((heads, 1, tq), F32), pltpu.VMEM((heads * FOX_VT_ROWS, tq), F32)],
        compiler_params=pltpu.CompilerParams(
            dimension_semantics=("arbitrary", "arbitrary", "arbitrary"), vmem_limit_bytes=VMEM_LIMIT),
        name="fox",
    )(fqa, fka, fvt)


def _post_kernel(x_ref, og_ref, of_ref, p_ref, wo_c, wgu_c, wdown_c, wpg_c, wpp_c,
                 ffn_nw_ref, ple_nw_ref, fin_nw_ref, out_ref,
                 wo_ref, wgate_ref, wup_ref, wdown_ref, wpg_ref, wpp_ref, act_ref, *, w_steps, tf):
    step = pl.program_id(0)
    d_ff = wgate_ref.shape[1]

    @pl.when(step < w_steps)
    def _():
        def put(dst_ref, chunk):
            rows = chunk.shape[0]
            dst_ref[pl.ds(pl.multiple_of(step * rows, rows), rows), :] = chunk.astype(BF16)

        put(wo_ref, wo_c[...])
        put(wgate_ref, wgu_c[:, 0:d_ff])
        put(wup_ref, wgu_c[:, d_ff:2 * d_ff])
        put(wdown_ref, wdown_c[...])
        put(wpg_ref, wpg_c[...])
        put(wpp_ref, wpp_c[...])

    @pl.when(step >= w_steps)
    def _():
        h = (x_ref[...] + _dot(og_ref[...], wo_ref[0:GDN_WIDTH, :])
             + _dot(of_ref[...], wo_ref[GDN_WIDTH:GDN_WIDTH + FOX_WIDTH, :]))

        hn = (h * _rms_scale(h) * ffn_nw_ref[...]).astype(BF16)
        for c in range(d_ff // tf):
            cols = slice(c * tf, (c + 1) * tf)
            gate = _dot(hn, wgate_ref[:, cols])
            up = _dot(hn, wup_ref[:, cols])
            act_ref[:, cols] = (_silu(gate) * up).astype(BF16)
        h = h + _dot(act_ref[...], wdown_ref[...])

        hn = (h * _rms_scale(h) * ple_nw_ref[...]).astype(BF16)
        ple_gate = _sigmoid(_dot(hn, wpg_ref[...]))
        h = h + ple_gate * _dot(p_ref[...].astype(BF16), wpp_ref[...])

        out_ref[...] = h * _rms_scale(h) * fin_nw_ref[...]


def _post(x2, o_g, o_f, p2, w_out, w_gate_up, w_down, w_ple_gate, w_ple_proj, ffn_nw, ple_nw, fin_nw):
    m, d = x2.shape
    tm = TM_POST
    w_steps = W_STEPS_POST
    d_ff = w_down.shape[0]
    tok = lambda w: pl.BlockSpec((tm, w), lambda s: (jnp.maximum(s - w_steps, 0), 0))
    wchunk = lambda w: pl.BlockSpec((w.shape[0] // w_steps, w.shape[1]), lambda s: (jnp.minimum(s, w_steps - 1), 0))
    weights = (w_out, w_gate_up, w_down, w_ple_gate, w_ple_proj)
    norms = (ffn_nw, ple_nw, fin_nw)
    in_specs = ([tok(d), tok(GDN_WIDTH), tok(FOX_WIDTH), tok(p2.shape[1])] + [wchunk(w) for w in weights]
                + [_const_spec(w.shape) for w in norms])
    return pl.pallas_call(
        functools.partial(_post_kernel, w_steps=w_steps, tf=TF_POST),
        grid=(w_steps + m // tm,), in_specs=in_specs, out_specs=tok(d),
        out_shape=jax.ShapeDtypeStruct((m, d), F32),
        scratch_shapes=[pltpu.VMEM(w_out.shape, BF16), pltpu.VMEM((d, d_ff), BF16), pltpu.VMEM((d, d_ff), BF16),
                        pltpu.VMEM(w_down.shape, BF16), pltpu.VMEM(w_ple_gate.shape, BF16),
                        pltpu.VMEM(w_ple_proj.shape, BF16), pltpu.VMEM((tm, d_ff), BF16)],
        compiler_params=pltpu.CompilerParams(dimension_semantics=("arbitrary",), vmem_limit_bytes=VMEM_LIMIT),
        name="post",
    )(x2, o_g, o_f, p2, *weights, *norms)


def _layer(x, p_i, attn_norm_w, w_in, conv_w, a_log, dt_bias, gdn_norm_w, fox_f_bias, w_out,
           ffn_norm_w, w_gate_up, w_down, ple_norm_w, w_ple_gate, w_ple_proj, out_norm_w):
    batch, seq, d = x.shape
    m = batch * seq
    assert seq % TM_IN == 0 and seq % PAIR == 0 and seq % TQ_FOX == 0 and m % TM_POST == 0
    assert batch % BG_GDN == 0
    row = lambda w: w.reshape(1, -1).astype(F32)

    c0 = 3 * GDN_WIDTH
    c1 = c0 + GDN_WIDTH
    c2 = c1 + 2 * GDN_HEADS
    c3 = c2 + 3 * FOX_WIDTH
    wg = w_in[:, :c0].astype(BF16)
    wz = w_in[:, c0:c1].astype(BF16)
    wf = w_in[:, c2:c2 + 2 * FOX_WIDTH].astype(BF16)
    wfvt = w_in[:, c2 + 2 * FOX_WIDTH:c3].T.astype(BF16)
    n_small = 2 * GDN_HEADS + FOX_HEADS
    ws = jnp.concatenate([w_in[:, c1:c2], w_in[:, c3:]], axis=1)
    ws = jnp.pad(ws, ((0, 0), (0, LANES - n_small))).astype(BF16)
    zeros4 = jnp.zeros((GDN_HEADS,), F32)
    pad = jnp.zeros((LANES - n_small,), F32)
    prm = jnp.stack([
        jnp.concatenate([dt_bias.astype(F32), zeros4, fox_f_bias.astype(F32), pad]),
        jnp.concatenate([a_log.astype(F32), zeros4, jnp.zeros((FOX_HEADS,), F32), pad]),
    ])
    prm = jnp.pad(prm, ((0, SUBLANES - prm.shape[0]), (0, 0)))

    x2 = x.reshape(m, d)
    gq, gk, gv, gz, fqa, fka, fvt, gates, gates_t = _inproj(
        x2, row(attn_norm_w), wg, wz, wf, wfvt, ws, conv_w.astype(F32), prm, batch=batch, seq=seq)
    o_g = _gdn(gq, gk, gv, gz, gates, gates_t, row(gdn_norm_w), batch=batch, seq=seq)
    o_f = _fox(fqa, fka, fvt, batch=batch, seq=seq)

    return _post(
        x2, o_g, o_f, p_i.reshape(m, -1),
        w_out.astype(F32), w_gate_up.astype(F32), w_down.astype(F32), w_ple_gate.astype(F32),
        w_ple_proj.astype(F32), row(ffn_norm_w), row(ple_norm_w), row(out_norm_w),
    ).reshape(batch, seq, d)


def kernel(x, p, attn_norm_w, w_in, conv_w, a_log, dt_bias, gdn_norm_w, fox_f_bias, w_out, ffn_norm_w,
           w_gate_up, w_down, ple_norm_w, w_ple_gate, w_ple_proj, final_norm_w):
    assert p.shape[0] == 1, "single-layer problem"
    return _layer(x, p[0], attn_norm_w[0], w_in[0], conv_w[0], a_log[0], dt_bias[0], gdn_norm_w[0],
                  fox_f_bias[0], w_out[0], ffn_norm_w[0], w_gate_up[0], w_down[0], ple_norm_w[0],
                  w_ple_gate[0], w_ple_proj[0], final_norm_w)
```

```python
import functools

import jax
import jax.numpy as jnp
import numpy as np
from jax import lax
from jax.experimental import pallas as pl
from jax.experimental.pallas import tpu as pltpu

F32 = jnp.float32
BF16 = jnp.bfloat16

EPS = 1e-6
GDN_HEADS = 4
GDN_HEAD_DIM = 128
GDN_WIDTH = GDN_HEADS * GDN_HEAD_DIM
FOX_HEADS = 8
FOX_HEAD_DIM = 64
FOX_WIDTH = FOX_HEADS * FOX_HEAD_DIM
CONV_WIDTH = 4
CHUNK = 64
PAIR = 2 * CHUNK
LANES = 128
SUBLANES = 8
BF16_ROWS = 16
FOX_VT_ROWS = FOX_HEAD_DIM + BF16_ROWS
NEG_BIG = -1e30
LOG2E = 1.4426950408889634

GC_LANE = 0
BETA_LANE = 4
CF_LANE = 8
GATE_ROWS = 16

VMEM_LIMIT = 56 * 1024 * 1024

TM_IN = 512
TS_IN = 512
BG_GDN = 4
TQ_FOX = 256
HEADS_FOX = 8
TM_POST = 512
TS_POST = 256
TF_POST = 256
W_STEPS_POST = 8


def _dot(a, b):
    return jnp.dot(a, b, preferred_element_type=F32)


def _dot_nt(a, b):
    return lax.dot_general(a, b, (((1,), (1,)), ((), ())), preferred_element_type=F32)


def _dot_tn(a, b):
    return lax.dot_general(a, b, (((0,), (0,)), ((), ())), preferred_element_type=F32)


def _dot_inv(a, b):
    return _dot(a.astype(BF16), b.astype(BF16))


def _sigmoid(x):
    return 0.5 * jnp.tanh(0.5 * x) + 0.5


def _silu(x):
    h = 0.5 * x
    return h * jnp.tanh(h) + h


def _rms_scale(x):
    return lax.rsqrt(jnp.mean(x * x, axis=-1, keepdims=True) + EPS)


def _const_spec(shape):
    nd = len(shape)
    return pl.BlockSpec(shape, lambda *_: (0,) * nd, pipeline_mode=pl.Buffered(1))


def _inproj_kernel(x_ref, nw_ref, wg_ref, wz_ref, wf_ref, wfvt_ref, conv_ref, prm_ref, sel_ref,
                   gq_ref, gk_ref, gv_ref, gz_ref, fqa_ref, fka_ref, fvt_ref, gates_ref, gates_t_ref,
                   buf_ref, carry_ref, *, tm, ts, tiles_per_seq):
    i = pl.program_id(0)
    seq_start = (i % tiles_per_seq) == 0
    halo = SUBLANES

    @pl.when(seq_start)
    def _():
        buf_ref[0:halo, :] = jnp.zeros((halo, buf_ref.shape[1]), F32)
        carry_ref[...] = jnp.zeros_like(carry_ref)

    @pl.when(jnp.logical_not(seq_start))
    def _():
        buf_ref[0:halo, :] = buf_ref[tm:tm + halo, :]

    lane = lax.broadcasted_iota(jnp.int32, (ts, LANES), 1)
    row = lax.broadcasted_iota(jnp.int32, (ts, LANES), 0)
    is_gc = lane < BETA_LANE
    is_cf = (lane >= CF_LANE) & (lane < CF_LANE + FOX_HEADS)
    pos = jnp.where(is_gc, row % CHUNK, row)
    summed = is_gc | is_cf
    out_refs = (gq_ref, gk_ref, gv_ref)

    for r0 in range(0, tm, ts):
        rows = slice(r0, r0 + ts)
        x = x_ref[rows, :]
        xn = (x * _rms_scale(x) * nw_ref[...]).astype(BF16)

        buf_ref[halo + r0:halo + r0 + ts, :] = _dot(xn, wg_ref[...])
        for j in range(3 * GDN_HEADS):
            cols = slice(j * LANES, (j + 1) * LANES)
            acc = buf_ref[halo + r0:halo + r0 + ts, cols] * conv_ref[CONV_WIDTH - 1:CONV_WIDTH, cols]
            for t in range(CONV_WIDTH - 1):
                off = halo + r0 - (CONV_WIDTH - 1) + t
                acc = acc + buf_ref[off:off + ts, cols] * conv_ref[t:t + 1, cols]
            y = _silu(acc)
            kind, head = divmod(j, GDN_HEADS)
            if kind < 2:
                y = y * lax.rsqrt(jnp.sum(y * y, axis=-1, keepdims=True) + EPS)
            if kind == 0:
                y = y * (GDN_HEAD_DIM ** -0.5)
            out_refs[kind][rows, head * LANES:(head + 1) * LANES] = y

        zs = _dot(xn, wz_ref[...])
        gz_ref[rows, :] = zs[:, 0:GDN_WIDTH]

        vt = _dot_nt(wfvt_ref[...], xn).astype(BF16)
        for h in range(FOX_HEADS):
            v0 = h * FOX_VT_ROWS
            fvt_ref[0, v0:v0 + FOX_HEAD_DIM, rows] = vt[h * FOX_HEAD_DIM:(h + 1) * FOX_HEAD_DIM]
            fvt_ref[0, v0 + FOX_HEAD_DIM:v0 + FOX_VT_ROWS, rows] = jnp.ones((FOX_VT_ROWS - FOX_HEAD_DIM, ts), BF16)

        z = zs[:, GDN_WIDTH:GDN_WIDTH + LANES] + prm_ref[0:1, :]
        soft = jnp.log1p(jnp.exp(-jnp.abs(z)))
        g_decay = -jnp.exp(prm_ref[1:2, :]) * (jnp.maximum(z, 0.0) + soft)
        log_f = -(jnp.maximum(-z, 0.0) + soft)
        val = jnp.where(is_gc, g_decay, jnp.where(lane < CF_LANE, _sigmoid(z), jnp.where(is_cf, log_f, 0.0)))

        d = 1
        while d < ts:
            shifted = pltpu.roll(val, d, axis=0)
            val = val + jnp.where(summed & (pos >= d), shifted, 0.0)
            d *= 2
        val = val + jnp.where(is_cf, carry_ref[...], 0.0)
        carry_ref[...] = val[ts - 1:ts, :]

        gates_ref[rows, :] = val
        gates_t_ref[0, :, rows] = val.T[0:GATE_ROWS, :]

        f_qk = _dot(xn, wf_ref[...])
        c2 = jnp.where(is_cf, val * LOG2E, 0.0)
        c_hi = c2.astype(BF16).astype(F32)
        c_mid = (c2 - c_hi).astype(BF16).astype(F32)
        c_lo = c2 - c_hi - c_mid
        parts = c_hi + pltpu.roll(c_mid, FOX_HEADS, axis=1) + pltpu.roll(c_lo, 2 * FOX_HEADS, axis=1)
        bias = _dot(parts.astype(BF16), sel_ref[...])
        for h in range(FOX_HEADS):
            src, half = divmod(h, 2)
            in_half = (lane >= half * FOX_HEAD_DIM) & (lane < (half + 1) * FOX_HEAD_DIM)
            base = (1 - half) * FOX_HEAD_DIM
            first = (lane >= base) & (lane < base + 3)
            second = (lane >= base + 3) & (lane < base + 6)
            q_slab = f_qk[:, src * LANES:(src + 1) * LANES] * (FOX_HEAD_DIM ** -0.5 * LOG2E)
            k_slab = f_qk[:, FOX_WIDTH + src * LANES:FOX_WIDTH + (src + 1) * LANES]
            c_slab = bias[:, h * LANES:(h + 1) * LANES]
            fqa_ref[rows, h * LANES:(h + 1) * LANES] = jnp.where(
                in_half, q_slab, jnp.where(second, -1.0, c_slab)).astype(BF16)
            fka_ref[rows, h * LANES:(h + 1) * LANES] = jnp.where(
                in_half, k_slab, jnp.where(first, 1.0, c_slab)).astype(BF16)


def _bias_selector():
    sel = np.zeros((LANES, FOX_HEADS * LANES), np.float32)
    for h in range(FOX_HEADS):
        base = h * LANES + (1 - h % 2) * FOX_HEAD_DIM
        for part in range(3):
            src = CF_LANE + part * FOX_HEADS + h
            sel[src, base + part] = 1.0
            sel[src, base + 3 + part] = 1.0
    return jnp.asarray(sel, BF16)


def _inproj(x2, nw, wg, wz, wf, wfvt, conv_w, prm, *, batch, seq):
    sel = _bias_selector()
    m, d = x2.shape
    tm = TM_IN
    tiles_per_seq = seq // tm
    grid = (m // tm,)
    tok = lambda w: pl.BlockSpec((tm, w), lambda i: (i, 0))
    seq_t = lambda rows: pl.BlockSpec((1, rows, tm), lambda i: (i // tiles_per_seq, 0, i % tiles_per_seq))
    out_shape = (
        jax.ShapeDtypeStruct((m, GDN_WIDTH), F32),
        jax.ShapeDtypeStruct((m, GDN_WIDTH), F32),
        jax.ShapeDtypeStruct((m, GDN_WIDTH), F32),
        jax.ShapeDtypeStruct((m, GDN_WIDTH), F32),
        jax.ShapeDtypeStruct((m, FOX_HEADS * LANES), BF16),
        jax.ShapeDtypeStruct((m, FOX_HEADS * LANES), BF16),
        jax.ShapeDtypeStruct((batch, FOX_HEADS * FOX_VT_ROWS, seq), BF16),
        jax.ShapeDtypeStruct((m, LANES), F32),
        jax.ShapeDtypeStruct((batch, GATE_ROWS, seq), F32),
    )
    out_specs = (
        tok(GDN_WIDTH), tok(GDN_WIDTH), tok(GDN_WIDTH), tok(GDN_WIDTH),
        tok(FOX_HEADS * LANES), tok(FOX_HEADS * LANES), seq_t(FOX_HEADS * FOX_VT_ROWS), tok(LANES),
        seq_t(GATE_ROWS),
    )
    in_specs = [
        tok(d), _const_spec(nw.shape), _const_spec(wg.shape), _const_spec(wz.shape), _const_spec(wf.shape),
        _const_spec(wfvt.shape), _const_spec(conv_w.shape), _const_spec(prm.shape),
        _const_spec(sel.shape),
    ]
    return pl.pallas_call(
        functools.partial(_inproj_kernel, tm=tm, ts=TS_IN, tiles_per_seq=tiles_per_seq),
        grid=grid, in_specs=in_specs, out_specs=out_specs, out_shape=out_shape,
        scratch_shapes=[pltpu.VMEM((tm + SUBLANES, 3 * GDN_WIDTH), F32), pltpu.VMEM((1, LANES), F32)],
        compiler_params=pltpu.CompilerParams(dimension_semantics=("arbitrary",), vmem_limit_bytes=VMEM_LIMIT),
        name="inproj",
    )(x2, nw, wg, wz, wf, wfvt, conv_w, prm, sel)


def _gdn_kernel(q_ref, k_ref, v_ref, z_ref, gates_ref, gates_t_ref, nw_ref, o_ref, state_ref, *, bg):
    @pl.when(pl.program_id(1) == 0)
    def _():
        state_ref[...] = jnp.zeros_like(state_ref)

    ri = lax.broadcasted_iota(jnp.int32, (PAIR, PAIR), 0)
    ci = lax.broadcasted_iota(jnp.int32, (PAIR, PAIR), 1)
    same_chunk = (ri // CHUNK) == (ci // CHUNK)
    lower = same_chunk & (ri >= ci)
    strict = same_chunk & (ri > ci)
    eye = (ri == ci).astype(F32)
    first_chunk_row = lax.broadcasted_iota(jnp.int32, (PAIR, LANES), 0) < CHUNK
    nw = nw_ref[...]

    chains = [(b, h) for b in range(bg) for h in range(GDN_HEADS)]
    cols_of = lambda h: slice(h * GDN_HEAD_DIM, (h + 1) * GDN_HEAD_DIM)
    col_of = lambda gt, lane: gt[:, lane:lane + 1]

    gts = [gates_ref[b] for b in range(bg)]
    gt_ts = [gates_t_ref[b] for b in range(bg)]
    k16s, a_mats, decays, e_gcs = [], [], [], []
    for b, h in chains:
        k = k_ref[b, :, cols_of(h)]
        gc = col_of(gts[b], GC_LANE + h)
        gc_row = gt_ts[b][GC_LANE + h:GC_LANE + h + 1, :]
        decay = jnp.where(lower, jnp.exp(jnp.where(lower, gc - gc_row, 0.0)), 0.0)
        k16 = k.astype(BF16)
        a_mats.append(jnp.where(strict, _dot_nt((k * col_of(gts[b], BETA_LANE + h)).astype(BF16), k16) * decay, 0.0))
        k16s.append(k16)
        decays.append(decay)
        e_gcs.append(jnp.exp(gc))

    invs = [eye - a for a in a_mats]
    pows = a_mats
    for _ in range(5):
        pows = [_dot_inv(a, a) for a in pows]
        invs = [_dot_inv(t, eye + a) for t, a in zip(invs, pows)]

    us, wqs, qks, kdts, decs = [], [], [], [], []
    for n, (b, h) in enumerate(chains):
        gc = col_of(gts[b], GC_LANE + h)
        beta = col_of(gts[b], BETA_LANE + h)
        q = q_ref[b, :, cols_of(h)]
        k = k_ref[b, :, cols_of(h)]
        inv16 = invs[n].astype(BF16)
        us.append(_dot(inv16, (v_ref[b, :, cols_of(h)] * beta).astype(BF16)))
        w = _dot(inv16, ((k * beta) * e_gcs[n]).astype(BF16)).astype(BF16)
        qe = (q * e_gcs[n]).astype(BF16)
        wqs.append([jnp.concatenate([w[c * CHUNK:(c + 1) * CHUNK], qe[c * CHUNK:(c + 1) * CHUNK]], axis=0)
                    for c in range(PAIR // CHUNK)])
        qks.append(jnp.where(lower, _dot_nt(q.astype(BF16), k16s[n]) * decays[n], 0.0).astype(BF16))
        g_last = jnp.where(first_chunk_row, gts[b][CHUNK - 1:CHUNK, :], gts[b][PAIR - 1:PAIR, :])
        g_last = col_of(g_last, GC_LANE + h)
        kdts.append((k * jnp.exp(g_last - gc)).T.astype(BF16))
        decs.append(jnp.exp(g_last))

    states = [state_ref[b * GDN_HEADS + h] for b, h in chains]
    zeros = jnp.zeros((CHUNK, GDN_HEAD_DIM), BF16)
    for c in range(PAIR // CHUNK):
        rows = slice(c * CHUNK, (c + 1) * CHUNK)
        rs = [_dot(wqs[n][c], states[n].astype(BF16)) for n in range(len(chains))]
        v_news = [(us[n][rows] - rs[n][0:CHUNK]).astype(BF16) for n in range(len(chains))]
        rhs = [jnp.concatenate([v, zeros] if c == 0 else [zeros, v], axis=0) for v in v_news]
        upds = [_dot(kdts[n], rhs[n]) for n in range(len(chains))]
        states = [states[n] * decs[n][(c + 1) * CHUNK - 1:(c + 1) * CHUNK, :] + upds[n] for n in range(len(chains))]
        for n, (b, h) in enumerate(chains):
            o = rs[n][CHUNK:2 * CHUNK] + _dot(qks[n][rows], rhs[n])
            o = o * lax.rsqrt(jnp.mean(o * o, axis=-1, keepdims=True) + EPS) * nw
            o_ref[b, rows, cols_of(h)] = (o * _silu(z_ref[b, rows, cols_of(h)])).astype(o_ref.dtype)
    for n, (b, h) in enumerate(chains):
        state_ref[b * GDN_HEADS + h] = states[n]


def _gdn(gq, gk, gv, gz, gates, gates_t, nw, *, batch, seq):
    m = gq.shape[0]
    bg = BG_GDN
    seq3 = lambda a: a.reshape(batch, seq, a.shape[-1])
    blk = lambda w: pl.BlockSpec((bg, PAIR, w), lambda bi, j: (bi, j, 0))
    o = pl.pallas_call(
        functools.partial(_gdn_kernel, bg=bg),
        grid=(batch // bg, seq // PAIR),
        in_specs=[blk(GDN_WIDTH), blk(GDN_WIDTH), blk(GDN_WIDTH), blk(GDN_WIDTH), blk(LANES),
                  pl.BlockSpec((bg, GATE_ROWS, PAIR), lambda bi, j: (bi, 0, j)), _const_spec(nw.shape)],
        out_specs=blk(GDN_WIDTH),
        out_shape=jax.ShapeDtypeStruct((batch, seq, GDN_WIDTH), BF16),
        scratch_shapes=[pltpu.VMEM((bg * GDN_HEADS, GDN_HEAD_DIM, GDN_HEAD_DIM), F32)],
        compiler_params=pltpu.CompilerParams(
            dimension_semantics=("arbitrary", "arbitrary"), vmem_limit_bytes=VMEM_LIMIT),
        name="gdn",
    )(seq3(gq), seq3(gk), seq3(gv), seq3(gz), seq3(gates), gates_t, nw)
    return o.reshape(m, GDN_WIDTH)


def _fox_kernel(qa_ref, ka_ref, vt_ref, o_ref, m_ref, acc_ref, *, tq):
    i = pl.program_id(2)
    heads = qa_ref.shape[1] // LANES
    qas = [qa_ref[:, hh * LANES:(hh + 1) * LANES] for hh in range(heads)]

    m_ref[...] = jnp.full(m_ref.shape, NEG_BIG, F32)
    acc_ref[...] = jnp.zeros(acc_ref.shape, F32)

    def chunk(start, size, mask_off):
        ka = ka_ref[pl.ds(start, size), :]
        sts = [_dot_nt(ka[:, hh * LANES:(hh + 1) * LANES], qas[hh]) for hh in range(heads)]
        if mask_off is not None:
            ri = lax.broadcasted_iota(jnp.int32, (size, tq), 0)
            ci = lax.broadcasted_iota(jnp.int32, (size, tq), 1)
            keep = (ri - ci) <= mask_off
            sts = [jnp.where(keep, s, NEG_BIG) for s in sts]
        m_olds = [m_ref[hh] for hh in range(heads)]
        m_news = [jnp.maximum(m, jnp.max(s, axis=0, keepdims=True)) for m, s in zip(m_olds, sts)]
        ps = [jnp.exp2(s - m).astype(BF16) for s, m in zip(sts, m_news)]
        pvs = [_dot(vt_ref[0, hh * FOX_VT_ROWS:(hh + 1) * FOX_VT_ROWS, pl.ds(start, size)], ps[hh])
               for hh in range(heads)]
        for hh in range(heads):
            alpha = jnp.exp2(m_olds[hh] - m_news[hh])
            rows = slice(hh * FOX_VT_ROWS, (hh + 1) * FOX_VT_ROWS)
            m_ref[hh] = m_news[hh]
            acc_ref[rows, :] = alpha * acc_ref[rows, :] + pvs[hh]

    def body(j, carry):
        chunk(pl.multiple_of(j * 2 * tq, 2 * tq), 2 * tq, None)
        return carry

    lax.fori_loop(0, i // 2, body, 0)

    @pl.when(i % 2 == 0)
    def _():
        chunk(pl.multiple_of(i * tq, tq), tq, 0)

    @pl.when(i % 2 == 1)
    def _():
        chunk(pl.multiple_of((i - 1) * tq, tq), 2 * tq, tq)

    out_t = jnp.concatenate(
        [acc_ref[hh * FOX_VT_ROWS:hh * FOX_VT_ROWS + FOX_HEAD_DIM, :]
         / acc_ref[hh * FOX_VT_ROWS + FOX_HEAD_DIM:hh * FOX_VT_ROWS + FOX_HEAD_DIM + 1, :]
         for hh in range(heads)], axis=0)
    o_ref[...] = out_t.T.astype(o_ref.dtype)


def _fox(fqa, fka, fvt, *, batch, seq):
    m = fqa.shape[0]
    tq = TQ_FOX
    steps = seq // tq
    heads = HEADS_FOX
    return pl.pallas_call(
        functools.partial(_fox_kernel, tq=tq),
        grid=(batch, FOX_HEADS // heads, steps),
        in_specs=[pl.BlockSpec((tq, heads * LANES), lambda b, p, i: (b * steps + i, p)),
                  pl.BlockSpec((seq, heads * LANES), lambda b, p, i: (b, p)),
                  pl.BlockSpec((1, heads * FOX_VT_ROWS, seq), lambda b, p, i: (b, p, 0))],
        out_specs=pl.BlockSpec((tq, heads * FOX_HEAD_DIM), lambda b, p, i: (b * steps + i, p)),
        out_shape=jax.ShapeDtypeStruct((m, FOX_WIDTH), BF16),
        scratch_shapes=[pltpu.VMEM((heads, 1, tq), F32), pltpu.VMEM((heads * FOX_VT_ROWS, tq), F32)],
        compiler_params=pltpu.CompilerParams(
            dimension_semantics=("arbitrary", "arbitrary", "arbitrary"), vmem_limit_bytes=VMEM_LIMIT),
        name="fox",
    )(fqa, fka, fvt)


def _post_kernel(x_ref, og_ref, of_ref, p_ref, wo_c, wgu_c, wdown_c, wpg_c, wpp_c,
                 ffn_nw_ref, ple_nw_ref, fin_nw_ref, out_ref,
                 wo_ref, wgate_ref, wup_ref, wdown_ref, wpg_ref, wpp_ref, act_ref, *, w_steps, ts, tf):
    step = pl.program_id(0)
    d_ff = wgate_ref.shape[1]

    @pl.when(step < w_steps)
    def _():
        def put(dst_ref, chunk):
            rows = chunk.shape[0]
            dst_ref[pl.ds(pl.multiple_of(step * rows, rows), rows), :] = chunk.astype(BF16)

        put(wo_ref, wo_c[...])
        put(wgate_ref, wgu_c[:, 0:d_ff])
        put(wup_ref, wgu_c[:, d_ff:2 * d_ff])
        put(wdown_ref, wdown_c[...])
        put(wpg_ref, wpg_c[...])
        put(wpp_ref, wpp_c[...])

    @pl.when(step >= w_steps)
    def _():
        tm = x_ref.shape[0]
        groups = [slice(r, r + ts) for r in range(0, tm, ts)]
        hs = [x_ref[g, :] + _dot(og_ref[g, :], wo_ref[0:GDN_WIDTH, :])
              + _dot(of_ref[g, :], wo_ref[GDN_WIDTH:GDN_WIDTH + FOX_WIDTH, :]) for g in groups]

        hns = [(h * _rms_scale(h) * ffn_nw_ref[...]).astype(BF16) for h in hs]
        for g, hn in zip(groups, hns):
            for c in range(d_ff // tf):
                cols = slice(c * tf, (c + 1) * tf)
                gate = _dot(hn, wgate_ref[:, cols])
                up = _dot(hn, wup_ref[:, cols])
                act_ref[g, cols] = (_silu(gate) * up).astype(BF16)
        hs = [h + _dot(act_ref[g, :], wdown_ref[...]) for g, h in zip(groups, hs)]

        hns = [(h * _rms_scale(h) * ple_nw_ref[...]).astype(BF16) for h in hs]
        ple_gates = [_sigmoid(_dot(hn, wpg_ref[...])) for hn in hns]
        hs = [h + pg * _dot(p_ref[g, :].astype(BF16), wpp_ref[...]) for g, h, pg in zip(groups, hs, ple_gates)]

        for g, h in zip(groups, hs):
            out_ref[g, :] = h * _rms_scale(h) * fin_nw_ref[...]


def _post(x2, o_g, o_f, p2, w_out, w_gate_up, w_down, w_ple_gate, w_ple_proj, ffn_nw, ple_nw, fin_nw):
    m, d = x2.shape
    tm = TM_POST
    w_steps = W_STEPS_POST
    d_ff = w_down.shape[0]
    tok = lambda w: pl.BlockSpec((tm, w), lambda s: (jnp.maximum(s - w_steps, 0), 0))
    wchunk = lambda w: pl.BlockSpec((w.shape[0] // w_steps, w.shape[1]), lambda s: (jnp.minimum(s, w_steps - 1), 0))
    weights = (w_out, w_gate_up, w_down, w_ple_gate, w_ple_proj)
    norms = (ffn_nw, ple_nw, fin_nw)
    in_specs = ([tok(d), tok(GDN_WIDTH), tok(FOX_WIDTH), tok(p2.shape[1])] + [wchunk(w) for w in weights]
                + [_const_spec(w.shape) for w in norms])
    return pl.pallas_call(
        functools.partial(_post_kernel, w_steps=w_steps, ts=TS_POST, tf=TF_POST),
        grid=(w_steps + m // tm,), in_specs=in_specs, out_specs=tok(d),
        out_shape=jax.ShapeDtypeStruct((m, d), F32),
        scratch_shapes=[pltpu.VMEM(w_out.shape, BF16), pltpu.VMEM((d, d_ff), BF16), pltpu.VMEM((d, d_ff), BF16),
                        pltpu.VMEM(w_down.shape, BF16), pltpu.VMEM(w_ple_gate.shape, BF16),
                        pltpu.VMEM(w_ple_proj.shape, BF16), pltpu.VMEM((tm, d_ff), BF16)],
        compiler_params=pltpu.CompilerParams(dimension_semantics=("arbitrary",), vmem_limit_bytes=VMEM_LIMIT),
        name="post",
    )(x2, o_g, o_f, p2, *weights, *norms)


def _layer(x, p_i, attn_norm_w, w_in, conv_w, a_log, dt_bias, gdn_norm_w, fox_f_bias, w_out,
           ffn_norm_w, w_gate_up, w_down, ple_norm_w, w_ple_gate, w_ple_proj, out_norm_w):
    batch, seq, d = x.shape
    m = batch * seq
    assert seq % TM_IN == 0 and seq % PAIR == 0 and seq % TQ_FOX == 0 and m % TM_POST == 0
    assert batch % BG_GDN == 0
    row = lambda w: w.reshape(1, -1).astype(F32)

    c0 = 3 * GDN_WIDTH
    c1 = c0 + GDN_WIDTH
    c2 = c1 + 2 * GDN_HEADS
    c3 = c2 + 3 * FOX_WIDTH
    wg = w_in[:, :c0].astype(BF16)
    wf = w_in[:, c2:c2 + 2 * FOX_WIDTH].astype(BF16)
    wfvt = w_in[:, c2 + 2 * FOX_WIDTH:c3].T.astype(BF16)
    n_small = 2 * GDN_HEADS + FOX_HEADS
    wz = jnp.concatenate([w_in[:, c0:c2], w_in[:, c3:]], axis=1)
    wz = jnp.pad(wz, ((0, 0), (0, LANES - n_small))).astype(BF16)
    zeros4 = jnp.zeros((GDN_HEADS,), F32)
    pad = jnp.zeros((LANES - n_small,), F32)
    prm = jnp.stack([
        jnp.concatenate([dt_bias.astype(F32), zeros4, fox_f_bias.astype(F32), pad]),
        jnp.concatenate([a_log.astype(F32), zeros4, jnp.zeros((FOX_HEADS,), F32), pad]),
    ])
    prm = jnp.pad(prm, ((0, SUBLANES - prm.shape[0]), (0, 0)))

    x2 = x.reshape(m, d)
    gq, gk, gv, gz, fqa, fka, fvt, gates, gates_t = _inproj(
        x2, row(attn_norm_w), wg, wz, wf, wfvt, conv_w.astype(F32), prm, batch=batch, seq=seq)
    o_g = _gdn(gq, gk, gv, gz, gates, gates_t, row(gdn_norm_w), batch=batch, seq=seq)
    o_f = _fox(fqa, fka, fvt, batch=batch, seq=seq)

    return _post(
        x2, o_g, o_f, p_i.reshape(m, -1),
        w_out.astype(F32), w_gate_up.astype(F32), w_down.astype(F32), w_ple_gate.astype(F32),
        w_ple_proj.astype(F32), row(ffn_norm_w), row(ple_norm_w), row(out_norm_w),
    ).reshape(batch, seq, d)


def kernel(x, p, attn_norm_w, w_in, conv_w, a_log, dt_bias, gdn_norm_w, fox_f_bias, w_out, ffn_norm_w,
           w_gate_up, w_down, ple_norm_w, w_ple_gate, w_ple_proj, final_norm_w):
    assert p.shape[0] == 1, "single-layer problem"
    return _layer(x, p[0], attn_norm_w[0], w_in[0], conv_w[0], a_log[0], dt_bias[0], gdn_norm_w[0],
                  fox_f_bias[0], w_out[0], ffn_norm_w[0], w_gate_up[0], w_down[0], ple_norm_w[0],
                  w_ple_gate[0], w_ple_proj[0], final_norm_w)
```

```python
import functools

import jax
import jax.numpy as jnp
import numpy as np
from jax import lax
from jax.experimental import pallas as pl
from jax.experimental.pallas import tpu as pltpu

F32 = jnp.float32
BF16 = jnp.bfloat16

EPS = 1e-6
GDN_HEADS = 4
GDN_HEAD_DIM = 128
GDN_WIDTH = GDN_HEADS * GDN_HEAD_DIM
FOX_HEADS = 8
FOX_HEAD_DIM = 64
FOX_WIDTH = FOX_HEADS * FOX_HEAD_DIM
CONV_WIDTH = 4
CHUNK = 64
PAIR = 2 * CHUNK
LANES = 128
SUBLANES = 8
BF16_ROWS = 16
FOX_VT_ROWS = FOX_HEAD_DIM + BF16_ROWS
NEG_BIG = -1e30
LOG2E = 1.4426950408889634

GC_LANE = 0
BETA_LANE = 4
CF_LANE = 8
GATE_ROWS = 16

VMEM_LIMIT = 56 * 1024 * 1024

TM_IN = 512
TS_IN = 512
W_STEPS_IN = 8
BG_GDN = 4
TQ_FOX = 256
HEADS_FOX = 8
TM_POST = 512
TS_POST = 256
TF_POST = 256
W_STEPS_POST = 8


def _dot(a, b):
    return jnp.dot(a, b, preferred_element_type=F32)


def _dot_nt(a, b):
    return lax.dot_general(a, b, (((1,), (1,)), ((), ())), preferred_element_type=F32)


def _dot_tn(a, b):
    return lax.dot_general(a, b, (((0,), (0,)), ((), ())), preferred_element_type=F32)


def _dot_inv(a, b):
    return _dot(a.astype(BF16), b.astype(BF16))


def _sigmoid(x):
    return 0.5 * jnp.tanh(0.5 * x) + 0.5


def _silu(x):
    h = 0.5 * x
    return h * jnp.tanh(h) + h


def _rms_scale(x):
    return lax.rsqrt(jnp.mean(x * x, axis=-1, keepdims=True) + EPS)


def _const_spec(shape):
    nd = len(shape)
    return pl.BlockSpec(shape, lambda *_: (0,) * nd, pipeline_mode=pl.Buffered(1))


def _inproj_kernel(x_ref, nw_ref, win_c, wfvt_ref, ws_ref, conv_ref, prm_ref, sel_ref,
                   gq_ref, gk_ref, gv_ref, gz_ref, fqa_ref, fka_ref, fvt_ref, gates_ref, gates_t_ref,
                   buf_ref, carry_ref, wg_ref, wz_ref, wf_ref, *, w_steps, splits, **tile_params):
    step = pl.program_id(0)

    @pl.when(step < w_steps)
    def _():
        rows = win_c.shape[0]
        dst = pl.ds(pl.multiple_of(step * rows, rows), rows)
        c0, c1, c2 = splits
        wg_ref[dst, :] = win_c[:, 0:c0].astype(BF16)
        wz_ref[dst, :] = win_c[:, c0:c1].astype(BF16)
        wf_ref[dst, :] = win_c[:, c2:c2 + wf_ref.shape[1]].astype(BF16)

    @pl.when(step >= w_steps)
    def _():
        _inproj_tile(step - w_steps, x_ref, nw_ref, wg_ref, wz_ref, wf_ref, wfvt_ref, ws_ref, conv_ref, prm_ref,
                     sel_ref, gq_ref, gk_ref, gv_ref, gz_ref, fqa_ref, fka_ref, fvt_ref, gates_ref, gates_t_ref,
                     buf_ref, carry_ref, **tile_params)


def _inproj_tile(i, x_ref, nw_ref, wg_ref, wz_ref, wf_ref, wfvt_ref, ws_ref, conv_ref, prm_ref, sel_ref,
                 gq_ref, gk_ref, gv_ref, gz_ref, fqa_ref, fka_ref, fvt_ref, gates_ref, gates_t_ref,
                 buf_ref, carry_ref, *, tm, ts, tiles_per_seq):
    seq_start = (i % tiles_per_seq) == 0
    halo = SUBLANES

    @pl.when(seq_start)
    def _():
        buf_ref[0:halo, :] = jnp.zeros((halo, buf_ref.shape[1]), F32)
        carry_ref[...] = jnp.zeros_like(carry_ref)

    @pl.when(jnp.logical_not(seq_start))
    def _():
        buf_ref[0:halo, :] = buf_ref[tm:tm + halo, :]

    lane = lax.broadcasted_iota(jnp.int32, (ts, LANES), 1)
    row = lax.broadcasted_iota(jnp.int32, (ts, LANES), 0)
    is_gc = lane < BETA_LANE
    is_cf = (lane >= CF_LANE) & (lane < CF_LANE + FOX_HEADS)
    pos = jnp.where(is_gc, row % CHUNK, row)
    summed = is_gc | is_cf
    out_refs = (gq_ref, gk_ref, gv_ref)

    for r0 in range(0, tm, ts):
        rows = slice(r0, r0 + ts)
        x = x_ref[rows, :]
        xn = (x * _rms_scale(x) * nw_ref[...]).astype(BF16)

        buf_ref[halo + r0:halo + r0 + ts, :] = _dot(xn, wg_ref[...])
        for j in range(3 * GDN_HEADS):
            cols = slice(j * LANES, (j + 1) * LANES)
            acc = buf_ref[halo + r0:halo + r0 + ts, cols] * conv_ref[CONV_WIDTH - 1:CONV_WIDTH, cols]
            for t in range(CONV_WIDTH - 1):
                off = halo + r0 - (CONV_WIDTH - 1) + t
                acc = acc + buf_ref[off:off + ts, cols] * conv_ref[t:t + 1, cols]
            y = _silu(acc)
            kind, head = divmod(j, GDN_HEADS)
            if kind < 2:
                y = y * lax.rsqrt(jnp.sum(y * y, axis=-1, keepdims=True) + EPS)
            if kind == 0:
                y = y * (GDN_HEAD_DIM ** -0.5)
            out_refs[kind][rows, head * LANES:(head + 1) * LANES] = y

        gz_ref[rows, :] = _dot(xn, wz_ref[...])

        vt = _dot_nt(wfvt_ref[...], xn).astype(BF16)
        for h in range(FOX_HEADS):
            v0 = h * FOX_VT_ROWS
            fvt_ref[0, v0:v0 + FOX_HEAD_DIM, rows] = vt[h * FOX_HEAD_DIM:(h + 1) * FOX_HEAD_DIM]
            fvt_ref[0, v0 + FOX_HEAD_DIM:v0 + FOX_VT_ROWS, rows] = jnp.ones((FOX_VT_ROWS - FOX_HEAD_DIM, ts), BF16)

        z = _dot(xn, ws_ref[...]) + prm_ref[0:1, :]
        soft = jnp.log1p(jnp.exp(-jnp.abs(z)))
        g_decay = -jnp.exp(prm_ref[1:2, :]) * (jnp.maximum(z, 0.0) + soft)
        log_f = -(jnp.maximum(-z, 0.0) + soft)
        val = jnp.where(is_gc, g_decay, jnp.where(lane < CF_LANE, _sigmoid(z), jnp.where(is_cf, log_f, 0.0)))

        d = 1
        while d < ts:
            shifted = pltpu.roll(val, d, axis=0)
            val = val + jnp.where(summed & (pos >= d), shifted, 0.0)
            d *= 2
        val = val + jnp.where(is_cf, carry_ref[...], 0.0)
        carry_ref[...] = val[ts - 1:ts, :]

        gates_ref[rows, :] = val
        gates_t_ref[0, :, rows] = val.T[0:GATE_ROWS, :]

        f_qk = _dot(xn, wf_ref[...])
        c2 = jnp.where(is_cf, val * LOG2E, 0.0)
        c_hi = c2.astype(BF16).astype(F32)
        c_mid = (c2 - c_hi).astype(BF16).astype(F32)
        c_lo = c2 - c_hi - c_mid
        parts = (jnp.where(lane == 0, 1.0, c_hi) + pltpu.roll(c_mid, FOX_HEADS, axis=1)
                 + pltpu.roll(c_lo, 2 * FOX_HEADS, axis=1))
        bias = _dot(parts.astype(BF16), sel_ref[...])
        for h in range(FOX_HEADS):
            src, half = divmod(h, 2)
            in_half = (lane >= half * FOX_HEAD_DIM) & (lane < (half + 1) * FOX_HEAD_DIM)
            q_slab = f_qk[:, src * LANES:(src + 1) * LANES] * (FOX_HEAD_DIM ** -0.5 * LOG2E)
            k_slab = f_qk[:, FOX_WIDTH + src * LANES:FOX_WIDTH + (src + 1) * LANES]
            q_bias = bias[:, h * LANES:(h + 1) * LANES]
            k_bias = bias[:, (FOX_HEADS + h) * LANES:(FOX_HEADS + h + 1) * LANES]
            fqa_ref[rows, h * LANES:(h + 1) * LANES] = jnp.where(in_half, q_slab, q_bias).astype(BF16)
            fka_ref[rows, h * LANES:(h + 1) * LANES] = jnp.where(in_half, k_slab, k_bias).astype(BF16)


def _bias_selector():
    sel = np.zeros((LANES, 2 * FOX_HEADS * LANES), np.float32)
    for h in range(FOX_HEADS):
        base = (1 - h % 2) * FOX_HEAD_DIM
        q0 = h * LANES + base
        k0 = (FOX_HEADS + h) * LANES + base
        for part in range(3):
            src = CF_LANE + part * FOX_HEADS + h
            sel[src, q0 + part] = 1.0
            sel[0, q0 + 3 + part] = -1.0
            sel[0, k0 + part] = 1.0
            sel[src, k0 + 3 + part] = 1.0
    return jnp.asarray(sel, BF16)


def _inproj(x2, nw, w_in, splits, wfvt, ws, conv_w, prm, *, batch, seq):
    sel = _bias_selector()
    m, d = x2.shape
    tm = TM_IN
    tiles_per_seq = seq // tm
    w_steps = W_STEPS_IN
    grid = (w_steps + m // tm,)
    tile = lambda s: jnp.maximum(s - w_steps, 0)
    tok = lambda w: pl.BlockSpec((tm, w), lambda s: (tile(s), 0))
    seq_t = lambda rows: pl.BlockSpec(
        (1, rows, tm), lambda s: (tile(s) // tiles_per_seq, 0, tile(s) % tiles_per_seq))
    c0, c1, c2 = splits
    out_shape = (
        jax.ShapeDtypeStruct((m, GDN_WIDTH), F32),
        jax.ShapeDtypeStruct((m, GDN_WIDTH), F32),
        jax.ShapeDtypeStruct((m, GDN_WIDTH), F32),
        jax.ShapeDtypeStruct((m, GDN_WIDTH), F32),
        jax.ShapeDtypeStruct((m, FOX_HEADS * LANES), BF16),
        jax.ShapeDtypeStruct((m, FOX_HEADS * LANES), BF16),
        jax.ShapeDtypeStruct((batch, FOX_HEADS * FOX_VT_ROWS, seq), BF16),
        jax.ShapeDtypeStruct((m, LANES), F32),
        jax.ShapeDtypeStruct((batch, GATE_ROWS, seq), F32),
    )
    out_specs = (
        tok(GDN_WIDTH), tok(GDN_WIDTH), tok(GDN_WIDTH), tok(GDN_WIDTH),
        tok(FOX_HEADS * LANES), tok(FOX_HEADS * LANES), seq_t(FOX_HEADS * FOX_VT_ROWS), tok(LANES),
        seq_t(GATE_ROWS),
    )
    in_specs = [
        tok(d), _const_spec(nw.shape),
        pl.BlockSpec((w_in.shape[0] // w_steps, w_in.shape[1]), lambda s: (jnp.minimum(s, w_steps - 1), 0)),
        _const_spec(wfvt.shape), _const_spec(ws.shape), _const_spec(conv_w.shape), _const_spec(prm.shape),
        _const_spec(sel.shape),
    ]
    return pl.pallas_call(
        functools.partial(_inproj_kernel, w_steps=w_steps, splits=splits,
                          tm=tm, ts=TS_IN, tiles_per_seq=tiles_per_seq),
        grid=grid, in_specs=in_specs, out_specs=out_specs, out_shape=out_shape,
        scratch_shapes=[pltpu.VMEM((tm + SUBLANES, 3 * GDN_WIDTH), F32), pltpu.VMEM((1, LANES), F32),
                        pltpu.VMEM((d, c0), BF16), pltpu.VMEM((d, c1 - c0), BF16),
                        pltpu.VMEM((d, 2 * FOX_WIDTH), BF16)],
        compiler_params=pltpu.CompilerParams(dimension_semantics=("arbitrary",), vmem_limit_bytes=VMEM_LIMIT),
        name="inproj",
    )(x2, nw, w_in, wfvt, ws, conv_w, prm, sel)


def _gdn_kernel(q_ref, k_ref, v_ref, z_ref, gates_ref, gates_t_ref, nw_ref, o_ref, state_ref, *, bg):
    @pl.when(pl.program_id(1) == 0)
    def _():
        state_ref[...] = jnp.zeros_like(state_ref)

    ri = lax.broadcasted_iota(jnp.int32, (PAIR, PAIR), 0)
    ci = lax.broadcasted_iota(jnp.int32, (PAIR, PAIR), 1)
    same_chunk = (ri // CHUNK) == (ci // CHUNK)
    lower = same_chunk & (ri >= ci)
    strict = same_chunk & (ri > ci)
    eye = (ri == ci).astype(F32)
    first_chunk_row = lax.broadcasted_iota(jnp.int32, (PAIR, LANES), 0) < CHUNK
    nw = nw_ref[...]

    chains = [(b, h) for b in range(bg) for h in range(GDN_HEADS)]
    cols_of = lambda h: slice(h * GDN_HEAD_DIM, (h + 1) * GDN_HEAD_DIM)
    col_of = lambda gt, lane: gt[:, lane:lane + 1]

    gts = [gates_ref[b] for b in range(bg)]
    gt_ts = [gates_t_ref[b] for b in range(bg)]
    k16s, a_mats, decays, e_gcs = [], [], [], []
    for b, h in chains:
        k = k_ref[b, :, cols_of(h)]
        gc = col_of(gts[b], GC_LANE + h)
        gc_row = gt_ts[b][GC_LANE + h:GC_LANE + h + 1, :]
        decay = jnp.where(lower, jnp.exp(jnp.where(lower, gc - gc_row, 0.0)), 0.0)
        k16 = k.astype(BF16)
        a_mats.append(jnp.where(strict, _dot_nt((k * col_of(gts[b], BETA_LANE + h)).astype(BF16), k16) * decay, 0.0))
        k16s.append(k16)
        decays.append(decay)
        e_gcs.append(jnp.exp(gc))

    invs = [eye - a for a in a_mats]
    pows = a_mats
    for _ in range(5):
        pows = [_dot_inv(a, a) for a in pows]
        invs = [_dot_inv(t, eye + a) for t, a in zip(invs, pows)]

    us, wqs, qks, kdts, decs = [], [], [], [], []
    for n, (b, h) in enumerate(chains):
        gc = col_of(gts[b], GC_LANE + h)
        beta = col_of(gts[b], BETA_LANE + h)
        q = q_ref[b, :, cols_of(h)]
        k = k_ref[b, :, cols_of(h)]
        inv16 = invs[n].astype(BF16)
        us.append(_dot(inv16, (v_ref[b, :, cols_of(h)] * beta).astype(BF16)))
        w = _dot(inv16, ((k * beta) * e_gcs[n]).astype(BF16)).astype(BF16)
        qe = (q * e_gcs[n]).astype(BF16)
        wqs.append([jnp.concatenate([w[c * CHUNK:(c + 1) * CHUNK], qe[c * CHUNK:(c + 1) * CHUNK]], axis=0)
                    for c in range(PAIR // CHUNK)])
        qks.append(jnp.where(lower, _dot_nt(q.astype(BF16), k16s[n]) * decays[n], 0.0).astype(BF16))
        g_last = jnp.where(first_chunk_row, gts[b][CHUNK - 1:CHUNK, :], gts[b][PAIR - 1:PAIR, :])
        g_last = col_of(g_last, GC_LANE + h)
        kdts.append((k * jnp.exp(g_last - gc)).T.astype(BF16))
        decs.append(jnp.exp(g_last))

    states = [state_ref[b * GDN_HEADS + h] for b, h in chains]
    zeros = jnp.zeros((CHUNK, GDN_HEAD_DIM), BF16)
    for c in range(PAIR // CHUNK):
        rows = slice(c * CHUNK, (c + 1) * CHUNK)
        rs = [_dot(wqs[n][c], states[n].astype(BF16)) for n in range(len(chains))]
        v_news = [(us[n][rows] - rs[n][0:CHUNK]).astype(BF16) for n in range(len(chains))]
        rhs = [jnp.concatenate([v, zeros] if c == 0 else [zeros, v], axis=0) for v in v_news]
        upds = [_dot(kdts[n], rhs[n]) for n in range(len(chains))]
        states = [states[n] * decs[n][(c + 1) * CHUNK - 1:(c + 1) * CHUNK, :] + upds[n] for n in range(len(chains))]
        for n, (b, h) in enumerate(chains):
            o = rs[n][CHUNK:2 * CHUNK] + _dot(qks[n][rows], rhs[n])
            o = o * lax.rsqrt(jnp.mean(o * o, axis=-1, keepdims=True) + EPS) * nw
            o_ref[b, rows, cols_of(h)] = (o * _silu(z_ref[b, rows, cols_of(h)])).astype(o_ref.dtype)
    for n, (b, h) in enumerate(chains):
        state_ref[b * GDN_HEADS + h] = states[n]


def _gdn(gq, gk, gv, gz, gates, gates_t, nw, *, batch, seq):
    m = gq.shape[0]
    bg = BG_GDN
    seq3 = lambda a: a.reshape(batch, seq, a.shape[-1])
    blk = lambda w: pl.BlockSpec((bg, PAIR, w), lambda bi, j: (bi, j, 0))
    o = pl.pallas_call(
        functools.partial(_gdn_kernel, bg=bg),
        grid=(batch // bg, seq // PAIR),
        in_specs=[blk(GDN_WIDTH), blk(GDN_WIDTH), blk(GDN_WIDTH), blk(GDN_WIDTH), blk(LANES),
                  pl.BlockSpec((bg, GATE_ROWS, PAIR), lambda bi, j: (bi, 0, j)), _const_spec(nw.shape)],
        out_specs=blk(GDN_WIDTH),
        out_shape=jax.ShapeDtypeStruct((batch, seq, GDN_WIDTH), BF16),
        scratch_shapes=[pltpu.VMEM((bg * GDN_HEADS, GDN_HEAD_DIM, GDN_HEAD_DIM), F32)],
        compiler_params=pltpu.CompilerParams(
            dimension_semantics=("arbitrary", "arbitrary"), vmem_limit_bytes=VMEM_LIMIT),
        name="gdn",
    )(seq3(gq), seq3(gk), seq3(gv), seq3(gz), seq3(gates), gates_t, nw)
    return o.reshape(m, GDN_WIDTH)


def _fox_kernel(qa_ref, ka_ref, vt_ref, o_ref, m_ref, acc_ref, *, tq):
    i = pl.program_id(2)
    heads = qa_ref.shape[1] // LANES
    qas = [qa_ref[:, hh * LANES:(hh + 1) * LANES] for hh in range(heads)]

    m_ref[...] = jnp.full(m_ref.shape, NEG_BIG, F32)
    acc_ref[...] = jnp.zeros(acc_ref.shape, F32)

    def chunk(start, size, mask_off):
        ka = ka_ref[pl.ds(start, size), :]
        sts = [_dot_nt(ka[:, hh * LANES:(hh + 1) * LANES], qas[hh]) for hh in range(heads)]
        if mask_off is not None:
            ri = lax.broadcasted_iota(jnp.int32, (size, tq), 0)
            ci = lax.broadcasted_iota(jnp.int32, (size, tq), 1)
            keep = (ri - ci) <= mask_off
            sts = [jnp.where(keep, s, NEG_BIG) for s in sts]
        m_olds = [m_ref[hh] for hh in range(heads)]
        m_news = [jnp.maximum(m, jnp.max(s, axis=0, keepdims=True)) for m, s in zip(m_olds, sts)]
        ps = [jnp.exp2(s - m).astype(BF16) for s, m in zip(sts, m_news)]
        pvs = [_dot(vt_ref[0, hh * FOX_VT_ROWS:(hh + 1) * FOX_VT_ROWS, pl.ds(start, size)], ps[hh])
               for hh in range(heads)]
        for hh in range(heads):
            alpha = jnp.exp2(m_olds[hh] - m_news[hh])
            rows = slice(hh * FOX_VT_ROWS, (hh + 1) * FOX_VT_ROWS)
            m_ref[hh] = m_news[hh]
            acc_ref[rows, :] = alpha * acc_ref[rows, :] + pvs[hh]

    def body(j, carry):
        chunk(pl.multiple_of(j * 2 * tq, 2 * tq), 2 * tq, None)
        return carry

    lax.fori_loop(0, i // 2, body, 0)

    @pl.when(i % 2 == 0)
    def _():
        chunk(pl.multiple_of(i * tq, tq), tq, 0)

    @pl.when(i % 2 == 1)
    def _():
        chunk(pl.multiple_of((i - 1) * tq, tq), 2 * tq, tq)

    out_t = jnp.concatenate(
        [acc_ref[hh * FOX_VT_ROWS:hh * FOX_VT_ROWS + FOX_HEAD_DIM, :]
         / acc_ref[hh * FOX_VT_ROWS + FOX_HEAD_DIM:hh * FOX_VT_ROWS + FOX_HEAD_DIM + 1, :]
         for hh in range(heads)], axis=0)
    o_ref[...] = out_t.T.astype(o_ref.dtype)


def _fox(fqa, fka, fvt, *, batch, seq):
    m = fqa.shape[0]
    tq = TQ_FOX
    steps = seq // tq
    heads = HEADS_FOX
    return pl.pallas_call(
        functools.partial(_fox_kernel, tq=tq),
        grid=(batch, FOX_HEADS // heads, steps),
        in_specs=[pl.BlockSpec((tq, heads * LANES), lambda b, p, i: (b * steps + i, p)),
                  pl.BlockSpec((seq, heads * LANES), lambda b, p, i: (b, p)),
                  pl.BlockSpec((1, heads * FOX_VT_ROWS, seq), lambda b, p, i: (b, p, 0))],
        out_specs=pl.BlockSpec((tq, heads * FOX_HEAD_DIM), lambda b, p, i: (b * steps + i, p)),
        out_shape=jax.ShapeDtypeStruct((m, FOX_WIDTH), BF16),
        scratch_shapes=[pltpu.VMEM((heads, 1, tq), F32), pltpu.VMEM((heads * FOX_VT_ROWS, tq), F32)],
        compiler_params=pltpu.CompilerParams(
            dimension_semantics=("arbitrary", "arbitrary", "arbitrary"), vmem_limit_bytes=VMEM_LIMIT),
        name="fox",
    )(fqa, fka, fvt)


def _post_kernel(x_ref, og_ref, of_ref, p_ref, wo_c, wgu_c, wdown_c, wpg_c, wpp_c,
                 ffn_nw_ref, ple_nw_ref, fin_nw_ref, out_ref,
                 wo_ref, wgate_ref, wup_ref, wdown_ref, wpg_ref, wpp_ref, act_ref, *, w_steps, ts, tf):
    step = pl.program_id(0)
    d_ff = wgate_ref.shape[1]

    @pl.when(step < w_steps)
    def _():
        def put(dst_ref, chunk):
            rows = chunk.shape[0]
            dst_ref[pl.ds(pl.multiple_of(step * rows, rows), rows), :] = chunk.astype(BF16)

        put(wo_ref, wo_c[...])
        put(wgate_ref, wgu_c[:, 0:d_ff])
        put(wup_ref, wgu_c[:, d_ff:2 * d_ff])
        put(wdown_ref, wdown_c[...])
        put(wpg_ref, wpg_c[...])
        put(wpp_ref, wpp_c[...])

    @pl.when(step >= w_steps)
    def _():
        tm = x_ref.shape[0]
        groups = [slice(r, r + ts) for r in range(0, tm, ts)]
        hs = [x_ref[g, :] + _dot(og_ref[g, :], wo_ref[0:GDN_WIDTH, :])
              + _dot(of_ref[g, :], wo_ref[GDN_WIDTH:GDN_WIDTH + FOX_WIDTH, :]) for g in groups]

        hns = [(h * _rms_scale(h) * ffn_nw_ref[...]).astype(BF16) for h in hs]
        for g, hn in zip(groups, hns):
            for c in range(d_ff // tf):
                cols = slice(c * tf, (c + 1) * tf)
                gate = _dot(hn, wgate_ref[:, cols])
                up = _dot(hn, wup_ref[:, cols])
                act_ref[g, cols] = (_silu(gate) * up).astype(BF16)
        hs = [h + _dot(act_ref[g, :], wdown_ref[...]) for g, h in zip(groups, hs)]

        hns = [(h * _rms_scale(h) * ple_nw_ref[...]).astype(BF16) for h in hs]
        ple_gates = [_sigmoid(_dot(hn, wpg_ref[...])) for hn in hns]
        hs = [h + pg * _dot(p_ref[g, :].astype(BF16), wpp_ref[...]) for g, h, pg in zip(groups, hs, ple_gates)]

        for g, h in zip(groups, hs):
            out_ref[g, :] = h * _rms_scale(h) * fin_nw_ref[...]


def _post(x2, o_g, o_f, p2, w_out, w_gate_up, w_down, w_ple_gate, w_ple_proj, ffn_nw, ple_nw, fin_nw):
    m, d = x2.shape
    tm = TM_POST
    w_steps = W_STEPS_POST
    d_ff = w_down.shape[0]
    tok = lambda w: pl.BlockSpec((tm, w), lambda s: (jnp.maximum(s - w_steps, 0), 0))
    wchunk = lambda w: pl.BlockSpec((w.shape[0] // w_steps, w.shape[1]), lambda s: (jnp.minimum(s, w_steps - 1), 0))
    weights = (w_out, w_gate_up, w_down, w_ple_gate, w_ple_proj)
    norms = (ffn_nw, ple_nw, fin_nw)
    in_specs = ([tok(d), tok(GDN_WIDTH), tok(FOX_WIDTH), tok(p2.shape[1])] + [wchunk(w) for w in weights]
                + [_const_spec(w.shape) for w in norms])
    return pl.pallas_call(
        functools.partial(_post_kernel, w_steps=w_steps, ts=TS_POST, tf=TF_POST),
        grid=(w_steps + m // tm,), in_specs=in_specs, out_specs=tok(d),
        out_shape=jax.ShapeDtypeStruct((m, d), F32),
        scratch_shapes=[pltpu.VMEM(w_out.shape, BF16), pltpu.VMEM((d, d_ff), BF16), pltpu.VMEM((d, d_ff), BF16),
                        pltpu.VMEM(w_down.shape, BF16), pltpu.VMEM(w_ple_gate.shape, BF16),
                        pltpu.VMEM(w_ple_proj.shape, BF16), pltpu.VMEM((tm, d_ff), BF16)],
        compiler_params=pltpu.CompilerParams(dimension_semantics=("arbitrary",), vmem_limit_bytes=VMEM_LIMIT),
        name="post",
    )(x2, o_g, o_f, p2, *weights, *norms)


def _layer(x, p_i, attn_norm_w, w_in, conv_w, a_log, dt_bias, gdn_norm_w, fox_f_bias, w_out,
           ffn_norm_w, w_gate_up, w_down, ple_norm_w, w_ple_gate, w_ple_proj, out_norm_w):
    batch, seq, d = x.shape
    m = batch * seq
    assert seq % TM_IN == 0 and seq % PAIR == 0 and seq % TQ_FOX == 0 and m % TM_POST == 0
    assert batch % BG_GDN == 0
    row = lambda w: w.reshape(1, -1).astype(F32)

    c0 = 3 * GDN_WIDTH
    c1 = c0 + GDN_WIDTH
    c2 = c1 + 2 * GDN_HEADS
    c3 = c2 + 3 * FOX_WIDTH
    wfvt = w_in[:, c2 + 2 * FOX_WIDTH:c3].T.astype(BF16)
    n_small = 2 * GDN_HEADS + FOX_HEADS
    ws = jnp.concatenate([w_in[:, c1:c2], w_in[:, c3:]], axis=1)
    ws = jnp.pad(ws, ((0, 0), (0, LANES - n_small))).astype(BF16)
    zeros4 = jnp.zeros((GDN_HEADS,), F32)
    pad = jnp.zeros((LANES - n_small,), F32)
    prm = jnp.stack([
        jnp.concatenate([dt_bias.astype(F32), zeros4, fox_f_bias.astype(F32), pad]),
        jnp.concatenate([a_log.astype(F32), zeros4, jnp.zeros((FOX_HEADS,), F32), pad]),
    ])
    prm = jnp.pad(prm, ((0, SUBLANES - prm.shape[0]), (0, 0)))

    x2 = x.reshape(m, d)
    gq, gk, gv, gz, fqa, fka, fvt, gates, gates_t = _inproj(
        x2, row(attn_norm_w), w_in.astype(F32), (c0, c1, c2), wfvt, ws, conv_w.astype(F32), prm,
        batch=batch, seq=seq)
    o_g = _gdn(gq, gk, gv, gz, gates, gates_t, row(gdn_norm_w), batch=batch, seq=seq)
    o_f = _fox(fqa, fka, fvt, batch=batch, seq=seq)

    return _post(
        x2, o_g, o_f, p_i.reshape(m, -1),
        w_out.astype(F32), w_gate_up.astype(F32), w_down.astype(F32), w_ple_gate.astype(F32),
        w_ple_proj.astype(F32), row(ffn_norm_w), row(ple_norm_w), row(out_norm_w),
    ).reshape(batch, seq, d)


def kernel(x, p, attn_norm_w, w_in, conv_w, a_log, dt_bias, gdn_norm_w, fox_f_bias, w_out, ffn_norm_w,
           w_gate_up, w_down, ple_norm_w, w_ple_gate, w_ple_proj, final_norm_w):
    assert p.shape[0] == 1, "single-layer problem"
    return _layer(x, p[0], attn_norm_w[0], w_in[0], conv_w[0], a_log[0], dt_bias[0], gdn_norm_w[0],
                  fox_f_bias[0], w_out[0], ffn_norm_w[0], w_gate_up[0], w_down[0], ple_norm_w[0],
                  w_ple_gate[0], w_ple_proj[0], final_norm_w)
```

```python
import functools

import jax
import jax.numpy as jnp
import numpy as np
from jax import lax
from jax.experimental import pallas as pl
from jax.experimental.pallas import tpu as pltpu

F32 = jnp.float32
BF16 = jnp.bfloat16

EPS = 1e-6
GDN_HEADS = 4
GDN_HEAD_DIM = 128
GDN_WIDTH = GDN_HEADS * GDN_HEAD_DIM
FOX_HEADS = 8
FOX_HEAD_DIM = 64
FOX_WIDTH = FOX_HEADS * FOX_HEAD_DIM
CONV_WIDTH = 4
CHUNK = 64
PAIR = 2 * CHUNK
LANES = 128
SUBLANES = 8
BF16_ROWS = 16
FOX_VT_ROWS = FOX_HEAD_DIM + BF16_ROWS
NEG_BIG = -1e30
LOG2E = 1.4426950408889634

GC_LANE = 0
BETA_LANE = 4
CF_LANE = 8
GATE_ROWS = 16

VMEM_LIMIT = 56 * 1024 * 1024

TM_IN = 512
TS_IN = 512
W_STEPS_IN = 8
BG_GDN = 4
TQ_FOX = 256
HEADS_FOX = 8
TM_POST = 512
TS_POST = 256
TF_POST = 256
W_STEPS_POST = 8


def _dot(a, b):
    return jnp.dot(a, b, preferred_element_type=F32)


def _dot_nt(a, b):
    return lax.dot_general(a, b, (((1,), (1,)), ((), ())), preferred_element_type=F32)


def _dot_tn(a, b):
    return lax.dot_general(a, b, (((0,), (0,)), ((), ())), preferred_element_type=F32)


def _dot_inv(a, b):
    return _dot(a.astype(BF16), b.astype(BF16))


def _sigmoid(x):
    return 0.5 * jnp.tanh(0.5 * x) + 0.5


def _silu(x):
    h = 0.5 * x
    return h * jnp.tanh(h) + h


def _rms_scale(x):
    return lax.rsqrt(jnp.mean(x * x, axis=-1, keepdims=True) + EPS)


def _const_spec(shape):
    nd = len(shape)
    return pl.BlockSpec(shape, lambda *_: (0,) * nd, pipeline_mode=pl.Buffered(1))


def _inproj_kernel(x_ref, nw_ref, win_c, ws_ref, conv_ref, prm_ref, sel_ref,
                   gq_ref, gk_ref, gv_ref, gz_ref, fqa_ref, fka_ref, fvt_ref, gates_ref, gates_t_ref,
                   buf_ref, carry_ref, wg_ref, wz_ref, wf_ref, wfvt_ref, *, w_steps, splits, **tile_params):
    step = pl.program_id(0)

    @pl.when(step < w_steps)
    def _():
        rows = win_c.shape[0]
        dst = pl.ds(pl.multiple_of(step * rows, rows), rows)
        c0, c1, c2 = splits
        c3 = c2 + wf_ref.shape[1]
        wg_ref[dst, :] = win_c[:, 0:c0].astype(BF16)
        wz_ref[dst, :] = win_c[:, c0:c1].astype(BF16)
        wf_ref[dst, :] = win_c[:, c2:c3].astype(BF16)
        v_t = win_c[:, c3:c3 + wfvt_ref.shape[0]].T.astype(BF16)
        for k in range(w_steps):
            @pl.when(step == k)
            def _():
                wfvt_ref[:, k * rows:(k + 1) * rows] = v_t

    @pl.when(step >= w_steps)
    def _():
        _inproj_tile(step - w_steps, x_ref, nw_ref, wg_ref, wz_ref, wf_ref, wfvt_ref, ws_ref, conv_ref, prm_ref,
                     sel_ref, gq_ref, gk_ref, gv_ref, gz_ref, fqa_ref, fka_ref, fvt_ref, gates_ref, gates_t_ref,
                     buf_ref, carry_ref, **tile_params)


def _inproj_tile(i, x_ref, nw_ref, wg_ref, wz_ref, wf_ref, wfvt_ref, ws_ref, conv_ref, prm_ref, sel_ref,
                 gq_ref, gk_ref, gv_ref, gz_ref, fqa_ref, fka_ref, fvt_ref, gates_ref, gates_t_ref,
                 buf_ref, carry_ref, *, tm, ts, tiles_per_seq):
    seq_start = (i % tiles_per_seq) == 0
    halo = SUBLANES

    @pl.when(seq_start)
    def _():
        buf_ref[0:halo, :] = jnp.zeros((halo, buf_ref.shape[1]), F32)
        carry_ref[...] = jnp.zeros_like(carry_ref)

    @pl.when(jnp.logical_not(seq_start))
    def _():
        buf_ref[0:halo, :] = buf_ref[tm:tm + halo, :]

    lane = lax.broadcasted_iota(jnp.int32, (ts, LANES), 1)
    row = lax.broadcasted_iota(jnp.int32, (ts, LANES), 0)
    is_gc = lane < BETA_LANE
    is_cf = (lane >= CF_LANE) & (lane < CF_LANE + FOX_HEADS)
    pos = jnp.where(is_gc, row % CHUNK, row)
    summed = is_gc | is_cf
    out_refs = (gq_ref, gk_ref, gv_ref)

    for r0 in range(0, tm, ts):
        rows = slice(r0, r0 + ts)
        x = x_ref[rows, :]
        xn = (x * _rms_scale(x) * nw_ref[...]).astype(BF16)

        buf_ref[halo + r0:halo + r0 + ts, :] = _dot(xn, wg_ref[...])
        for j in range(3 * GDN_HEADS):
            cols = slice(j * LANES, (j + 1) * LANES)
            acc = buf_ref[halo + r0:halo + r0 + ts, cols] * conv_ref[CONV_WIDTH - 1:CONV_WIDTH, cols]
            for t in range(CONV_WIDTH - 1):
                off = halo + r0 - (CONV_WIDTH - 1) + t
                acc = acc + buf_ref[off:off + ts, cols] * conv_ref[t:t + 1, cols]
            y = _silu(acc)
            kind, head = divmod(j, GDN_HEADS)
            if kind < 2:
                y = y * lax.rsqrt(jnp.sum(y * y, axis=-1, keepdims=True) + EPS)
            if kind == 0:
                y = y * (GDN_HEAD_DIM ** -0.5)
            out_refs[kind][rows, head * LANES:(head + 1) * LANES] = y

        gz_ref[rows, :] = _dot(xn, wz_ref[...])

        vt = _dot_nt(wfvt_ref[...], xn).astype(BF16)
        for h in range(FOX_HEADS):
            v0 = h * FOX_VT_ROWS
            fvt_ref[0, v0:v0 + FOX_HEAD_DIM, rows] = vt[h * FOX_HEAD_DIM:(h + 1) * FOX_HEAD_DIM]
            fvt_ref[0, v0 + FOX_HEAD_DIM:v0 + FOX_VT_ROWS, rows] = jnp.ones((FOX_VT_ROWS - FOX_HEAD_DIM, ts), BF16)

        z = _dot(xn, ws_ref[...]) + prm_ref[0:1, :]
        soft = jnp.log1p(jnp.exp(-jnp.abs(z)))
        g_decay = -jnp.exp(prm_ref[1:2, :]) * (jnp.maximum(z, 0.0) + soft)
        log_f = -(jnp.maximum(-z, 0.0) + soft)
        val = jnp.where(is_gc, g_decay, jnp.where(lane < CF_LANE, _sigmoid(z), jnp.where(is_cf, log_f, 0.0)))

        d = 1
        while d < ts:
            shifted = pltpu.roll(val, d, axis=0)
            val = val + jnp.where(summed & (pos >= d), shifted, 0.0)
            d *= 2
        val = val + jnp.where(is_cf, carry_ref[...], 0.0)
        carry_ref[...] = val[ts - 1:ts, :]

        gates_ref[rows, :] = val
        gates_t_ref[0, :, rows] = val.T[0:GATE_ROWS, :]

        f_qk = _dot(xn, wf_ref[...])
        c2 = jnp.where(is_cf, val * LOG2E, 0.0)
        c_hi = c2.astype(BF16).astype(F32)
        c_mid = (c2 - c_hi).astype(BF16).astype(F32)
        c_lo = c2 - c_hi - c_mid
        parts = (jnp.where(lane == 0, 1.0, c_hi) + pltpu.roll(c_mid, FOX_HEADS, axis=1)
                 + pltpu.roll(c_lo, 2 * FOX_HEADS, axis=1))
        bias = _dot(parts.astype(BF16), sel_ref[...])
        for h in range(FOX_HEADS):
            src, half = divmod(h, 2)
            in_half = (lane >= half * FOX_HEAD_DIM) & (lane < (half + 1) * FOX_HEAD_DIM)
            q_slab = f_qk[:, src * LANES:(src + 1) * LANES] * (FOX_HEAD_DIM ** -0.5 * LOG2E)
            k_slab = f_qk[:, FOX_WIDTH + src * LANES:FOX_WIDTH + (src + 1) * LANES]
            q_bias = bias[:, h * LANES:(h + 1) * LANES]
            k_bias = bias[:, (FOX_HEADS + h) * LANES:(FOX_HEADS + h + 1) * LANES]
            fqa_ref[rows, h * LANES:(h + 1) * LANES] = jnp.where(in_half, q_slab, q_bias).astype(BF16)
            fka_ref[rows, h * LANES:(h + 1) * LANES] = jnp.where(in_half, k_slab, k_bias).astype(BF16)


def _bias_selector():
    sel = np.zeros((LANES, 2 * FOX_HEADS * LANES), np.float32)
    for h in range(FOX_HEADS):
        base = (1 - h % 2) * FOX_HEAD_DIM
        q0 = h * LANES + base
        k0 = (FOX_HEADS + h) * LANES + base
        for part in range(3):
            src = CF_LANE + part * FOX_HEADS + h
            sel[src, q0 + part] = 1.0
            sel[0, q0 + 3 + part] = -1.0
            sel[0, k0 + part] = 1.0
            sel[src, k0 + 3 + part] = 1.0
    return jnp.asarray(sel, BF16)


def _inproj(x2, nw, w_in, splits, ws, conv_w, prm, *, batch, seq):
    sel = _bias_selector()
    m, d = x2.shape
    tm = TM_IN
    tiles_per_seq = seq // tm
    w_steps = W_STEPS_IN
    grid = (w_steps + m // tm,)
    tile = lambda s: jnp.maximum(s - w_steps, 0)
    tok = lambda w: pl.BlockSpec((tm, w), lambda s: (tile(s), 0))
    seq_t = lambda rows: pl.BlockSpec(
        (1, rows, tm), lambda s: (tile(s) // tiles_per_seq, 0, tile(s) % tiles_per_seq))
    c0, c1, c2 = splits
    out_shape = (
        jax.ShapeDtypeStruct((m, GDN_WIDTH), F32),
        jax.ShapeDtypeStruct((m, GDN_WIDTH), F32),
        jax.ShapeDtypeStruct((m, GDN_WIDTH), F32),
        jax.ShapeDtypeStruct((m, GDN_WIDTH), F32),
        jax.ShapeDtypeStruct((m, FOX_HEADS * LANES), BF16),
        jax.ShapeDtypeStruct((m, FOX_HEADS * LANES), BF16),
        jax.ShapeDtypeStruct((batch, FOX_HEADS * FOX_VT_ROWS, seq), BF16),
        jax.ShapeDtypeStruct((m, LANES), F32),
        jax.ShapeDtypeStruct((batch, GATE_ROWS, seq), F32),
    )
    out_specs = (
        tok(GDN_WIDTH), tok(GDN_WIDTH), tok(GDN_WIDTH), tok(GDN_WIDTH),
        tok(FOX_HEADS * LANES), tok(FOX_HEADS * LANES), seq_t(FOX_HEADS * FOX_VT_ROWS), tok(LANES),
        seq_t(GATE_ROWS),
    )
    in_specs = [
        tok(d), _const_spec(nw.shape),
        pl.BlockSpec((w_in.shape[0] // w_steps, w_in.shape[1]), lambda s: (jnp.minimum(s, w_steps - 1), 0)),
        _const_spec(ws.shape), _const_spec(conv_w.shape), _const_spec(prm.shape),
        _const_spec(sel.shape),
    ]
    return pl.pallas_call(
        functools.partial(_inproj_kernel, w_steps=w_steps, splits=splits,
                          tm=tm, ts=TS_IN, tiles_per_seq=tiles_per_seq),
        grid=grid, in_specs=in_specs, out_specs=out_specs, out_shape=out_shape,
        scratch_shapes=[pltpu.VMEM((tm + SUBLANES, 3 * GDN_WIDTH), F32), pltpu.VMEM((1, LANES), F32),
                        pltpu.VMEM((d, c0), BF16), pltpu.VMEM((d, c1 - c0), BF16),
                        pltpu.VMEM((d, 2 * FOX_WIDTH), BF16), pltpu.VMEM((FOX_WIDTH, d), BF16)],
        compiler_params=pltpu.CompilerParams(dimension_semantics=("arbitrary",), vmem_limit_bytes=VMEM_LIMIT),
        name="inproj",
    )(x2, nw, w_in, ws, conv_w, prm, sel)


def _gdn_kernel(q_ref, k_ref, v_ref, z_ref, gates_ref, gates_t_ref, nw_ref, o_ref, state_ref, *, bg):
    @pl.when(pl.program_id(1) == 0)
    def _():
        state_ref[...] = jnp.zeros_like(state_ref)

    ri = lax.broadcasted_iota(jnp.int32, (PAIR, PAIR), 0)
    ci = lax.broadcasted_iota(jnp.int32, (PAIR, PAIR), 1)
    same_chunk = (ri // CHUNK) == (ci // CHUNK)
    lower = same_chunk & (ri >= ci)
    strict = same_chunk & (ri > ci)
    eye = (ri == ci).astype(F32)
    first_chunk_row = lax.broadcasted_iota(jnp.int32, (PAIR, LANES), 0) < CHUNK
    nw = nw_ref[...]

    chains = [(b, h) for b in range(bg) for h in range(GDN_HEADS)]
    cols_of = lambda h: slice(h * GDN_HEAD_DIM, (h + 1) * GDN_HEAD_DIM)
    col_of = lambda gt, lane: gt[:, lane:lane + 1]

    gts = [gates_ref[b] for b in range(bg)]
    gt_ts = [gates_t_ref[b] for b in range(bg)]
    k16s, a_mats, decays, e_gcs = [], [], [], []
    for b, h in chains:
        k = k_ref[b, :, cols_of(h)]
        gc = col_of(gts[b], GC_LANE + h)
        gc_row = gt_ts[b][GC_LANE + h:GC_LANE + h + 1, :]
        decay = jnp.where(lower, jnp.exp(jnp.where(lower, gc - gc_row, 0.0)), 0.0)
        k16 = k.astype(BF16)
        a_mats.append(jnp.where(strict, _dot_nt((k * col_of(gts[b], BETA_LANE + h)).astype(BF16), k16) * decay, 0.0))
        k16s.append(k16)
        decays.append(decay)
        e_gcs.append(jnp.exp(gc))

    invs = [eye - a for a in a_mats]
    pows = a_mats
    for _ in range(5):
        pows = [_dot_inv(a, a) for a in pows]
        invs = [_dot_inv(t, eye + a) for t, a in zip(invs, pows)]

    us, wqs, qks, kdts, decs = [], [], [], [], []
    for n, (b, h) in enumerate(chains):
        gc = col_of(gts[b], GC_LANE + h)
        beta = col_of(gts[b], BETA_LANE + h)
        q = q_ref[b, :, cols_of(h)]
        k = k_ref[b, :, cols_of(h)]
        inv16 = invs[n].astype(BF16)
        us.append(_dot(inv16, (v_ref[b, :, cols_of(h)] * beta).astype(BF16)))
        w = _dot(inv16, ((k * beta) * e_gcs[n]).astype(BF16)).astype(BF16)
        qe = (q * e_gcs[n]).astype(BF16)
        wqs.append([jnp.concatenate([w[c * CHUNK:(c + 1) * CHUNK], qe[c * CHUNK:(c + 1) * CHUNK]], axis=0)
                    for c in range(PAIR // CHUNK)])
        qks.append(jnp.where(lower, _dot_nt(q.astype(BF16), k16s[n]) * decays[n], 0.0).astype(BF16))
        g_last = jnp.where(first_chunk_row, gts[b][CHUNK - 1:CHUNK, :], gts[b][PAIR - 1:PAIR, :])
        g_last = col_of(g_last, GC_LANE + h)
        kdts.append((k * jnp.exp(g_last - gc)).T.astype(BF16))
        decs.append(jnp.exp(g_last))

    states = [state_ref[b * GDN_HEADS + h] for b, h in chains]
    zeros = jnp.zeros((CHUNK, GDN_HEAD_DIM), BF16)
    for c in range(PAIR // CHUNK):
        rows = slice(c * CHUNK, (c + 1) * CHUNK)
        rs = [_dot(wqs[n][c], states[n].astype(BF16)) for n in range(len(chains))]
        v_news = [(us[n][rows] - rs[n][0:CHUNK]).astype(BF16) for n in range(len(chains))]
        rhs = [jnp.concatenate([v, zeros] if c == 0 else [zeros, v], axis=0) for v in v_news]
        upds = [_dot(kdts[n], rhs[n]) for n in range(len(chains))]
        states = [states[n] * decs[n][(c + 1) * CHUNK - 1:(c + 1) * CHUNK, :] + upds[n] for n in range(len(chains))]
        for n, (b, h) in enumerate(chains):
            o = rs[n][CHUNK:2 * CHUNK] + _dot(qks[n][rows], rhs[n])
            o = o * lax.rsqrt(jnp.mean(o * o, axis=-1, keepdims=True) + EPS) * nw
            o_ref[b, rows, cols_of(h)] = (o * _silu(z_ref[b, rows, cols_of(h)])).astype(o_ref.dtype)
    for n, (b, h) in enumerate(chains):
        state_ref[b * GDN_HEADS + h] = states[n]


def _gdn(gq, gk, gv, gz, gates, gates_t, nw, *, batch, seq):
    m = gq.shape[0]
    bg = BG_GDN
    seq3 = lambda a: a.reshape(batch, seq, a.shape[-1])
    blk = lambda w: pl.BlockSpec((bg, PAIR, w), lambda bi, j: (bi, j, 0))
    o = pl.pallas_call(
        functools.partial(_gdn_kernel, bg=bg),
        grid=(batch // bg, seq // PAIR),
        in_specs=[blk(GDN_WIDTH), blk(GDN_WIDTH), blk(GDN_WIDTH), blk(GDN_WIDTH), blk(LANES),
                  pl.BlockSpec((bg, GATE_ROWS, PAIR), lambda bi, j: (bi, 0, j)), _const_spec(nw.shape)],
        out_specs=blk(GDN_WIDTH),
        out_shape=jax.ShapeDtypeStruct((batch, seq, GDN_WIDTH), BF16),
        scratch_shapes=[pltpu.VMEM((bg * GDN_HEADS, GDN_HEAD_DIM, GDN_HEAD_DIM), F32)],
        compiler_params=pltpu.CompilerParams(
            dimension_semantics=("arbitrary", "arbitrary"), vmem_limit_bytes=VMEM_LIMIT),
        name="gdn",
    )(seq3(gq), seq3(gk), seq3(gv), seq3(gz), seq3(gates), gates_t, nw)
    return o.reshape(m, GDN_WIDTH)


def _fox_kernel(qa_ref, ka_ref, vt_ref, o_ref, m_ref, acc_ref, *, tq):
    i = pl.program_id(2)
    heads = qa_ref.shape[1] // LANES
    qas = [qa_ref[:, hh * LANES:(hh + 1) * LANES] for hh in range(heads)]

    m_ref[...] = jnp.full(m_ref.shape, NEG_BIG, F32)
    acc_ref[...] = jnp.zeros(acc_ref.shape, F32)

    def chunk(start, size, mask_off):
        ka = ka_ref[pl.ds(start, size), :]
        sts = [_dot_nt(ka[:, hh * LANES:(hh + 1) * LANES], qas[hh]) for hh in range(heads)]
        if mask_off is not None:
            ri = lax.broadcasted_iota(jnp.int32, (size, tq), 0)
            ci = lax.broadcasted_iota(jnp.int32, (size, tq), 1)
            keep = (ri - ci) <= mask_off
            sts = [jnp.where(keep, s, NEG_BIG) for s in sts]
        m_olds = [m_ref[hh] for hh in range(heads)]
        m_news = [jnp.maximum(m, jnp.max(s, axis=0, keepdims=True)) for m, s in zip(m_olds, sts)]
        ps = [jnp.exp2(s - m).astype(BF16) for s, m in zip(sts, m_news)]
        pvs = [_dot(vt_ref[0, hh * FOX_VT_ROWS:(hh + 1) * FOX_VT_ROWS, pl.ds(start, size)], ps[hh])
               for hh in range(heads)]
        for hh in range(heads):
            alpha = jnp.exp2(m_olds[hh] - m_news[hh])
            rows = slice(hh * FOX_VT_ROWS, (hh + 1) * FOX_VT_ROWS)
            m_ref[hh] = m_news[hh]
            acc_ref[rows, :] = alpha * acc_ref[rows, :] + pvs[hh]

    def body(j, carry):
        chunk(pl.multiple_of(j * 2 * tq, 2 * tq), 2 * tq, None)
        return carry

    lax.fori_loop(0, i // 2, body, 0)

    @pl.when(i % 2 == 0)
    def _():
        chunk(pl.multiple_of(i * tq, tq), tq, 0)

    @pl.when(i % 2 == 1)
    def _():
        chunk(pl.multiple_of((i - 1) * tq, tq), 2 * tq, tq)

    out_t = jnp.concatenate(
        [acc_ref[hh * FOX_VT_ROWS:hh * FOX_VT_ROWS + FOX_HEAD_DIM, :]
         / acc_ref[hh * FOX_VT_ROWS + FOX_HEAD_DIM:hh * FOX_VT_ROWS + FOX_HEAD_DIM + 1, :]
         for hh in range(heads)], axis=0)
    o_ref[...] = out_t.T.astype(o_ref.dtype)


def _fox(fqa, fka, fvt, *, batch, seq):
    m = fqa.shape[0]
    tq = TQ_FOX
    steps = seq // tq
    heads = HEADS_FOX
    return pl.pallas_call(
        functools.partial(_fox_kernel, tq=tq),
        grid=(batch, FOX_HEADS // heads, steps),
        in_specs=[pl.BlockSpec((tq, heads * LANES), lambda b, p, i: (b * steps + i, p)),
                  pl.BlockSpec((seq, heads * LANES), lambda b, p, i: (b, p)),
                  pl.BlockSpec((1, heads * FOX_VT_ROWS, seq), lambda b, p, i: (b, p, 0))],
        out_specs=pl.BlockSpec((tq, heads * FOX_HEAD_DIM), lambda b, p, i: (b * steps + i, p)),
        out_shape=jax.ShapeDtypeStruct((m, FOX_WIDTH), BF16),
        scratch_shapes=[pltpu.VMEM((heads, 1, tq), F32), pltpu.VMEM((heads * FOX_VT_ROWS, tq), F32)],
        compiler_params=pltpu.CompilerParams(
            dimension_semantics=("arbitrary", "arbitrary", "arbitrary"), vmem_limit_bytes=VMEM_LIMIT),
        name="fox",
    )(fqa, fka, fvt)


def _post_kernel(x_ref, og_ref, of_ref, p_ref, wo_c, wgu_c, wdown_c, wpg_c, wpp_c,
                 ffn_nw_ref, ple_nw_ref, fin_nw_ref, out_ref,
                 wo_ref, wgate_ref, wup_ref, wdown_ref, wpg_ref, wpp_ref, act_ref, *, w_steps, ts, tf):
    step = pl.program_id(0)
    d_ff = wgate_ref.shape[1]

    @pl.when(step < w_steps)
    def _():
        def put(dst_ref, chunk):
            rows = chunk.shape[0]
            dst_ref[pl.ds(pl.multiple_of(step * rows, rows), rows), :] = chunk.astype(BF16)

        put(wo_ref, wo_c[...])
        put(wgate_ref, wgu_c[:, 0:d_ff])
        put(wup_ref, wgu_c[:, d_ff:2 * d_ff])
        put(wdown_ref, wdown_c[...])
        put(wpg_ref, wpg_c[...])
        put(wpp_ref, wpp_c[...])

    @pl.when(step >= w_steps)
    def _():
        tm = x_ref.shape[0]
        groups = [slice(r, r + ts) for r in range(0, tm, ts)]
        hs = [x_ref[g, :] + _dot(og_ref[g, :], wo_ref[0:GDN_WIDTH, :])
              + _dot(of_ref[g, :], wo_ref[GDN_WIDTH:GDN_WIDTH + FOX_WIDTH, :]) for g in groups]

        hns = [(h * _rms_scale(h) * ffn_nw_ref[...]).astype(BF16) for h in hs]
        for g, hn in zip(groups, hns):
            for c in range(d_ff // tf):
                cols = slice(c * tf, (c + 1) * tf)
                gate = _dot(hn, wgate_ref[:, cols])
                up = _dot(hn, wup_ref[:, cols])
                act_ref[g, cols] = (_silu(gate) * up).astype(BF16)
        hs = [h + _dot(act_ref[g, :], wdown_ref[...]) for g, h in zip(groups, hs)]

        hns = [(h * _rms_scale(h) * ple_nw_ref[...]).astype(BF16) for h in hs]
        ple_gates = [_sigmoid(_dot(hn, wpg_ref[...])) for hn in hns]
        hs = [h + pg * _dot(p_ref[g, :].astype(BF16), wpp_ref[...]) for g, h, pg in zip(groups, hs, ple_gates)]

        for g, h in zip(groups, hs):
            out_ref[g, :] = h * _rms_scale(h) * fin_nw_ref[...]


def _post(x2, o_g, o_f, p2, w_out, w_gate_up, w_down, w_ple_gate, w_ple_proj, ffn_nw, ple_nw, fin_nw):
    m, d = x2.shape
    tm = TM_POST
    w_steps = W_STEPS_POST
    d_ff = w_down.shape[0]
    tok = lambda w: pl.BlockSpec((tm, w), lambda s: (jnp.maximum(s - w_steps, 0), 0))
    wchunk = lambda w: pl.BlockSpec((w.shape[0] // w_steps, w.shape[1]), lambda s: (jnp.minimum(s, w_steps - 1), 0))
    weights = (w_out, w_gate_up, w_down, w_ple_gate, w_ple_proj)
    norms = (ffn_nw, ple_nw, fin_nw)
    in_specs = ([tok(d), tok(GDN_WIDTH), tok(FOX_WIDTH), tok(p2.shape[1])] + [wchunk(w) for w in weights]
                + [_const_spec(w.shape) for w in norms])
    return pl.pallas_call(
        functools.partial(_post_kernel, w_steps=w_steps, ts=TS_POST, tf=TF_POST),
        grid=(w_steps + m // tm,), in_specs=in_specs, out_specs=tok(d),
        out_shape=jax.ShapeDtypeStruct((m, d), F32),
        scratch_shapes=[pltpu.VMEM(w_out.shape, BF16), pltpu.VMEM((d, d_ff), BF16), pltpu.VMEM((d, d_ff), BF16),
                        pltpu.VMEM(w_down.shape, BF16), pltpu.VMEM(w_ple_gate.shape, BF16),
                        pltpu.VMEM(w_ple_proj.shape, BF16), pltpu.VMEM((tm, d_ff), BF16)],
        compiler_params=pltpu.CompilerParams(dimension_semantics=("arbitrary",), vmem_limit_bytes=VMEM_LIMIT),
        name="post",
    )(x2, o_g, o_f, p2, *weights, *norms)


def _layer(x, p_i, attn_norm_w, w_in, conv_w, a_log, dt_bias, gdn_norm_w, fox_f_bias, w_out,
           ffn_norm_w, w_gate_up, w_down, ple_norm_w, w_ple_gate, w_ple_proj, out_norm_w):
    batch, seq, d = x.shape
    m = batch * seq
    assert seq % TM_IN == 0 and seq % PAIR == 0 and seq % TQ_FOX == 0 and m % TM_POST == 0
    assert batch % BG_GDN == 0
    row = lambda w: w.reshape(1, -1).astype(F32)

    c0 = 3 * GDN_WIDTH
    c1 = c0 + GDN_WIDTH
    c2 = c1 + 2 * GDN_HEADS
    c3 = c2 + 3 * FOX_WIDTH
    n_small = 2 * GDN_HEADS + FOX_HEADS
    ws = jnp.concatenate([w_in[:, c1:c2], w_in[:, c3:]], axis=1)
    ws = jnp.pad(ws, ((0, 0), (0, LANES - n_small))).astype(BF16)
    zeros4 = jnp.zeros((GDN_HEADS,), F32)
    pad = jnp.zeros((LANES - n_small,), F32)
    prm = jnp.stack([
        jnp.concatenate([dt_bias.astype(F32), zeros4, fox_f_bias.astype(F32), pad]),
        jnp.concatenate([a_log.astype(F32), zeros4, jnp.zeros((FOX_HEADS,), F32), pad]),
    ])
    prm = jnp.pad(prm, ((0, SUBLANES - prm.shape[0]), (0, 0)))

    x2 = x.reshape(m, d)
    gq, gk, gv, gz, fqa, fka, fvt, gates, gates_t = _inproj(
        x2, row(attn_norm_w), w_in.astype(F32), (c0, c1, c2), ws, conv_w.astype(F32), prm,
        batch=batch, seq=seq)
    o_g = _gdn(gq, gk, gv, gz, gates, gates_t, row(gdn_norm_w), batch=batch, seq=seq)
    o_f = _fox(fqa, fka, fvt, batch=batch, seq=seq)

    return _post(
        x2, o_g, o_f, p_i.reshape(m, -1),
        w_out.astype(F32), w_gate_up.astype(F32), w_down.astype(F32), w_ple_gate.astype(F32),
        w_ple_proj.astype(F32), row(ffn_norm_w), row(ple_norm_w), row(out_norm_w),
    ).reshape(batch, seq, d)


def kernel(x, p, attn_norm_w, w_in, conv_w, a_log, dt_bias, gdn_norm_w, fox_f_bias, w_out, ffn_norm_w,
           w_gate_up, w_down, ple_norm_w, w_ple_gate, w_ple_proj, final_norm_w):
    assert p.shape[0] == 1, "single-layer problem"
    return _layer(x, p[0], attn_norm_w[0], w_in[0], conv_w[0], a_log[0], dt_bias[0], gdn_norm_w[0],
                  fox_f_bias[0], w_out[0], ffn_norm_w[0], w_gate_up[0], w_down[0], ple_norm_w[0],
                  w_ple_gate[0], w_ple_proj[0], final_norm_w)
```

```python
import functools

import jax
import jax.numpy as jnp
import numpy as np
from jax import lax
from jax.experimental import pallas as pl
from jax.experimental.pallas import tpu as pltpu

F32 = jnp.float32
BF16 = jnp.bfloat16

EPS = 1e-6
GDN_HEADS = 4
GDN_HEAD_DIM = 128
GDN_WIDTH = GDN_HEADS * GDN_HEAD_DIM
FOX_HEADS = 8
FOX_HEAD_DIM = 64
FOX_WIDTH = FOX_HEADS * FOX_HEAD_DIM
CONV_WIDTH = 4
CHUNK = 64
PAIR = 2 * CHUNK
LANES = 128
SUBLANES = 8
BF16_ROWS = 16
FOX_VT_ROWS = FOX_HEAD_DIM + BF16_ROWS
NEG_BIG = -1e30
LOG2E = 1.4426950408889634

GC_LANE = 0
BETA_LANE = 4
CF_LANE = 8
GATE_ROWS = 16

VMEM_LIMIT = 56 * 1024 * 1024

TM_IN = 512
TS_IN = 512
BG_GDN = 4
TQ_FOX = 256
HEADS_FOX = 8
TM_POST = 512
TS_POST = 256
TF_POST = 256
W_STEPS_POST = 8


def _dot(a, b):
    return jnp.dot(a, b, preferred_element_type=F32)


def _dot_nt(a, b):
    return lax.dot_general(a, b, (((1,), (1,)), ((), ())), preferred_element_type=F32)


def _dot_tn(a, b):
    return lax.dot_general(a, b, (((0,), (0,)), ((), ())), preferred_element_type=F32)


def _dot_inv(a, b):
    return _dot(a.astype(BF16), b.astype(BF16))


def _sigmoid(x):
    return 0.5 * jnp.tanh(0.5 * x) + 0.5


def _silu(x):
    h = 0.5 * x
    return h * jnp.tanh(h) + h


def _rms_scale(x):
    return lax.rsqrt(jnp.mean(x * x, axis=-1, keepdims=True) + EPS)


def _const_spec(shape):
    nd = len(shape)
    return pl.BlockSpec(shape, lambda *_: (0,) * nd, pipeline_mode=pl.Buffered(1))


def _inproj_kernel(x_ref, nw_ref, wt_ref, conv_ref, prm_ref, sel_ref,
                   gq_ref, gk_ref, gv_ref, gz_ref, fqa_ref, fka_ref, fvt_ref, gates_ref, gates_t_ref,
                   buf_ref, carry_ref, wg_ref, wz_ref, wf_ref, wfvt_ref, ws_ref, *, splits, **tile_params):
    step = pl.program_id(0)

    @pl.when(step == 0)
    def _():
        c0, c1, c2 = splits
        c3 = c2 + wf_ref.shape[0]
        c4 = c3 + wfvt_ref.shape[0]
        wg_ref[...] = wt_ref[0:c0, :].astype(BF16)
        wz_ref[...] = wt_ref[c0:c1, :].astype(BF16)
        wf_ref[...] = wt_ref[c2:c3, :].astype(BF16)
        wfvt_ref[...] = wt_ref[c3:c4, :].astype(BF16)
        small = jnp.concatenate([wt_ref[c1:c2, :], wt_ref[c4:wt_ref.shape[0], :]], axis=0)
        ws_ref[...] = jnp.zeros(ws_ref.shape, BF16)
        ws_ref[0:small.shape[0], :] = small.astype(BF16)

    @pl.when(step > 0)
    def _():
        _inproj_tile(step - 1, x_ref, nw_ref, wg_ref, wz_ref, wf_ref, wfvt_ref, ws_ref, conv_ref, prm_ref,
                     sel_ref, gq_ref, gk_ref, gv_ref, gz_ref, fqa_ref, fka_ref, fvt_ref, gates_ref, gates_t_ref,
                     buf_ref, carry_ref, **tile_params)


def _inproj_tile(i, x_ref, nw_ref, wg_ref, wz_ref, wf_ref, wfvt_ref, ws_ref, conv_ref, prm_ref, sel_ref,
                 gq_ref, gk_ref, gv_ref, gz_ref, fqa_ref, fka_ref, fvt_ref, gates_ref, gates_t_ref,
                 buf_ref, carry_ref, *, tm, ts, tiles_per_seq):
    seq_start = (i % tiles_per_seq) == 0
    halo = SUBLANES

    @pl.when(seq_start)
    def _():
        buf_ref[0:halo, :] = jnp.zeros((halo, buf_ref.shape[1]), F32)
        carry_ref[...] = jnp.zeros_like(carry_ref)

    @pl.when(jnp.logical_not(seq_start))
    def _():
        buf_ref[0:halo, :] = buf_ref[tm:tm + halo, :]

    lane = lax.broadcasted_iota(jnp.int32, (ts, LANES), 1)
    row = lax.broadcasted_iota(jnp.int32, (ts, LANES), 0)
    is_gc = lane < BETA_LANE
    is_cf = (lane >= CF_LANE) & (lane < CF_LANE + FOX_HEADS)
    pos = jnp.where(is_gc, row % CHUNK, row)
    summed = is_gc | is_cf
    out_refs = (gq_ref, gk_ref, gv_ref)

    for r0 in range(0, tm, ts):
        rows = slice(r0, r0 + ts)
        x = x_ref[rows, :]
        xn = (x * _rms_scale(x) * nw_ref[...]).astype(BF16)

        buf_ref[halo + r0:halo + r0 + ts, :] = _dot_nt(xn, wg_ref[...])
        for j in range(3 * GDN_HEADS):
            cols = slice(j * LANES, (j + 1) * LANES)
            acc = buf_ref[halo + r0:halo + r0 + ts, cols] * conv_ref[CONV_WIDTH - 1:CONV_WIDTH, cols]
            for t in range(CONV_WIDTH - 1):
                off = halo + r0 - (CONV_WIDTH - 1) + t
                acc = acc + buf_ref[off:off + ts, cols] * conv_ref[t:t + 1, cols]
            y = _silu(acc)
            kind, head = divmod(j, GDN_HEADS)
            if kind < 2:
                y = y * lax.rsqrt(jnp.sum(y * y, axis=-1, keepdims=True) + EPS)
            if kind == 0:
                y = y * (GDN_HEAD_DIM ** -0.5)
            out_refs[kind][rows, head * LANES:(head + 1) * LANES] = y

        gz_ref[rows, :] = _dot_nt(xn, wz_ref[...])

        vt = _dot_nt(wfvt_ref[...], xn).astype(BF16)
        for h in range(FOX_HEADS):
            v0 = h * FOX_VT_ROWS
            fvt_ref[0, v0:v0 + FOX_HEAD_DIM, rows] = vt[h * FOX_HEAD_DIM:(h + 1) * FOX_HEAD_DIM]
            fvt_ref[0, v0 + FOX_HEAD_DIM:v0 + FOX_VT_ROWS, rows] = jnp.ones((FOX_VT_ROWS - FOX_HEAD_DIM, ts), BF16)

        z = _dot_nt(xn, ws_ref[...]) + prm_ref[0:1, :]
        soft = jnp.log1p(jnp.exp(-jnp.abs(z)))
        g_decay = -jnp.exp(prm_ref[1:2, :]) * (jnp.maximum(z, 0.0) + soft)
        log_f = -(jnp.maximum(-z, 0.0) + soft)
        val = jnp.where(is_gc, g_decay, jnp.where(lane < CF_LANE, _sigmoid(z), jnp.where(is_cf, log_f, 0.0)))

        d = 1
        while d < ts:
            shifted = pltpu.roll(val, d, axis=0)
            val = val + jnp.where(summed & (pos >= d), shifted, 0.0)
            d *= 2
        val = val + jnp.where(is_cf, carry_ref[...], 0.0)
        carry_ref[...] = val[ts - 1:ts, :]

        gates_ref[rows, :] = val
        gates_t_ref[0, :, rows] = val.T[0:GATE_ROWS, :]

        f_qk = _dot_nt(xn, wf_ref[...])
        c2 = jnp.where(is_cf, val * LOG2E, 0.0)
        c_hi = c2.astype(BF16).astype(F32)
        c_mid = (c2 - c_hi).astype(BF16).astype(F32)
        c_lo = c2 - c_hi - c_mid
        parts = (jnp.where(lane == 0, 1.0, c_hi) + pltpu.roll(c_mid, FOX_HEADS, axis=1)
                 + pltpu.roll(c_lo, 2 * FOX_HEADS, axis=1))
        bias = _dot(parts.astype(BF16), sel_ref[...])
        for h in range(FOX_HEADS):
            src, half = divmod(h, 2)
            in_half = (lane >= half * FOX_HEAD_DIM) & (lane < (half + 1) * FOX_HEAD_DIM)
            q_slab = f_qk[:, src * LANES:(src + 1) * LANES] * (FOX_HEAD_DIM ** -0.5 * LOG2E)
            k_slab = f_qk[:, FOX_WIDTH + src * LANES:FOX_WIDTH + (src + 1) * LANES]
            q_bias = bias[:, h * LANES:(h + 1) * LANES]
            k_bias = bias[:, (FOX_HEADS + h) * LANES:(FOX_HEADS + h + 1) * LANES]
            fqa_ref[rows, h * LANES:(h + 1) * LANES] = jnp.where(in_half, q_slab, q_bias).astype(BF16)
            fka_ref[rows, h * LANES:(h + 1) * LANES] = jnp.where(in_half, k_slab, k_bias).astype(BF16)


def _bias_selector():
    sel = np.zeros((LANES, 2 * FOX_HEADS * LANES), np.float32)
    for h in range(FOX_HEADS):
        base = (1 - h % 2) * FOX_HEAD_DIM
        q0 = h * LANES + base
        k0 = (FOX_HEADS + h) * LANES + base
        for part in range(3):
            src = CF_LANE + part * FOX_HEADS + h
            sel[src, q0 + part] = 1.0
            sel[0, q0 + 3 + part] = -1.0
            sel[0, k0 + part] = 1.0
            sel[src, k0 + 3 + part] = 1.0
    return jnp.asarray(sel, BF16)


def _inproj(x2, nw, w_t, splits, conv_w, prm, *, batch, seq):
    sel = _bias_selector()
    m, d = x2.shape
    tm = TM_IN
    tiles_per_seq = seq // tm
    grid = (1 + m // tm,)
    tile = lambda s: jnp.maximum(s - 1, 0)
    tok = lambda w: pl.BlockSpec((tm, w), lambda s: (tile(s), 0))
    seq_t = lambda rows: pl.BlockSpec(
        (1, rows, tm), lambda s: (tile(s) // tiles_per_seq, 0, tile(s) % tiles_per_seq))
    c0, c1, c2 = splits
    out_shape = (
        jax.ShapeDtypeStruct((m, GDN_WIDTH), F32),
        jax.ShapeDtypeStruct((m, GDN_WIDTH), F32),
        jax.ShapeDtypeStruct((m, GDN_WIDTH), F32),
        jax.ShapeDtypeStruct((m, GDN_WIDTH), F32),
        jax.ShapeDtypeStruct((m, FOX_HEADS * LANES), BF16),
        jax.ShapeDtypeStruct((m, FOX_HEADS * LANES), BF16),
        jax.ShapeDtypeStruct((batch, FOX_HEADS * FOX_VT_ROWS, seq), BF16),
        jax.ShapeDtypeStruct((m, LANES), F32),
        jax.ShapeDtypeStruct((batch, GATE_ROWS, seq), F32),
    )
    out_specs = (
        tok(GDN_WIDTH), tok(GDN_WIDTH), tok(GDN_WIDTH), tok(GDN_WIDTH),
        tok(FOX_HEADS * LANES), tok(FOX_HEADS * LANES), seq_t(FOX_HEADS * FOX_VT_ROWS), tok(LANES),
        seq_t(GATE_ROWS),
    )
    in_specs = [
        tok(d), _const_spec(nw.shape), _const_spec(w_t.shape), _const_spec(conv_w.shape),
        _const_spec(prm.shape), _const_spec(sel.shape),
    ]
    return pl.pallas_call(
        functools.partial(_inproj_kernel, splits=splits, tm=tm, ts=TS_IN, tiles_per_seq=tiles_per_seq),
        grid=grid, in_specs=in_specs, out_specs=out_specs, out_shape=out_shape,
        scratch_shapes=[pltpu.VMEM((tm + SUBLANES, 3 * GDN_WIDTH), F32), pltpu.VMEM((1, LANES), F32),
                        pltpu.VMEM((c0, d), BF16), pltpu.VMEM((c1 - c0, d), BF16),
                        pltpu.VMEM((2 * FOX_WIDTH, d), BF16), pltpu.VMEM((FOX_WIDTH, d), BF16),
                        pltpu.VMEM((LANES, d), BF16)],
        compiler_params=pltpu.CompilerParams(dimension_semantics=("arbitrary",), vmem_limit_bytes=VMEM_LIMIT),
        name="inproj",
    )(x2, nw, w_t, conv_w, prm, sel)


def _gdn_kernel(q_ref, k_ref, v_ref, z_ref, gates_ref, gates_t_ref, nw_ref, o_ref, state_ref, *, bg):
    @pl.when(pl.program_id(1) == 0)
    def _():
        state_ref[...] = jnp.zeros_like(state_ref)

    ri = lax.broadcasted_iota(jnp.int32, (PAIR, PAIR), 0)
    ci = lax.broadcasted_iota(jnp.int32, (PAIR, PAIR), 1)
    same_chunk = (ri // CHUNK) == (ci // CHUNK)
    lower = same_chunk & (ri >= ci)
    strict = same_chunk & (ri > ci)
    eye = (ri == ci).astype(F32)
    first_chunk_row = lax.broadcasted_iota(jnp.int32, (PAIR, LANES), 0) < CHUNK
    nw = nw_ref[...]

    chains = [(b, h) for b in range(bg) for h in range(GDN_HEADS)]
    cols_of = lambda h: slice(h * GDN_HEAD_DIM, (h + 1) * GDN_HEAD_DIM)
    col_of = lambda gt, lane: gt[:, lane:lane + 1]

    gts = [gates_ref[b] for b in range(bg)]
    gt_ts = [gates_t_ref[b] for b in range(bg)]
    k16s, a_mats, decays, e_gcs = [], [], [], []
    for b, h in chains:
        k = k_ref[b, :, cols_of(h)]
        gc = col_of(gts[b], GC_LANE + h)
        gc_row = gt_ts[b][GC_LANE + h:GC_LANE + h + 1, :]
        decay = jnp.where(lower, jnp.exp(jnp.where(lower, gc - gc_row, 0.0)), 0.0)
        k16 = k.astype(BF16)
        a_mats.append(jnp.where(strict, _dot_nt((k * col_of(gts[b], BETA_LANE + h)).astype(BF16), k16) * decay, 0.0))
        k16s.append(k16)
        decays.append(decay)
        e_gcs.append(jnp.exp(gc))

    invs = [eye - a for a in a_mats]
    pows = a_mats
    for _ in range(5):
        pows = [_dot_inv(a, a) for a in pows]
        invs = [_dot_inv(t, eye + a) for t, a in zip(invs, pows)]

    us, wqs, qks, kdts, decs = [], [], [], [], []
    for n, (b, h) in enumerate(chains):
        gc = col_of(gts[b], GC_LANE + h)
        beta = col_of(gts[b], BETA_LANE + h)
        q = q_ref[b, :, cols_of(h)]
        k = k_ref[b, :, cols_of(h)]
        inv16 = invs[n].astype(BF16)
        us.append(_dot(inv16, (v_ref[b, :, cols_of(h)] * beta).astype(BF16)))
        w = _dot(inv16, ((k * beta) * e_gcs[n]).astype(BF16)).astype(BF16)
        qe = (q * e_gcs[n]).astype(BF16)
        wqs.append([jnp.concatenate([w[c * CHUNK:(c + 1) * CHUNK], qe[c * CHUNK:(c + 1) * CHUNK]], axis=0)
                    for c in range(PAIR // CHUNK)])
        qks.append(jnp.where(lower, _dot_nt(q.astype(BF16), k16s[n]) * decays[n], 0.0).astype(BF16))
        g_last = jnp.where(first_chunk_row, gts[b][CHUNK - 1:CHUNK, :], gts[b][PAIR - 1:PAIR, :])
        g_last = col_of(g_last, GC_LANE + h)
        kdts.append((k * jnp.exp(g_last - gc)).T.astype(BF16))
        decs.append(jnp.exp(g_last))

    states = [state_ref[b * GDN_HEADS + h] for b, h in chains]
    zeros = jnp.zeros((CHUNK, GDN_HEAD_DIM), BF16)
    for c in range(PAIR // CHUNK):
        rows = slice(c * CHUNK, (c + 1) * CHUNK)
        rs = [_dot(wqs[n][c], states[n].astype(BF16)) for n in range(len(chains))]
        v_news = [(us[n][rows] - rs[n][0:CHUNK]).astype(BF16) for n in range(len(chains))]
        rhs = [jnp.concatenate([v, zeros] if c == 0 else [zeros, v], axis=0) for v in v_news]
        upds = [_dot(kdts[n], rhs[n]) for n in range(len(chains))]
        states = [states[n] * decs[n][(c + 1) * CHUNK - 1:(c + 1) * CHUNK, :] + upds[n] for n in range(len(chains))]
        for n, (b, h) in enumerate(chains):
            o = rs[n][CHUNK:2 * CHUNK] + _dot(qks[n][rows], rhs[n])
            o = o * lax.rsqrt(jnp.mean(o * o, axis=-1, keepdims=True) + EPS) * nw
            o_ref[b, rows, cols_of(h)] = (o * _silu(z_ref[b, rows, cols_of(h)])).astype(o_ref.dtype)
    for n, (b, h) in enumerate(chains):
        state_ref[b * GDN_HEADS + h] = states[n]


def _gdn(gq, gk, gv, gz, gates, gates_t, nw, *, batch, seq):
    m = gq.shape[0]
    bg = BG_GDN
    seq3 = lambda a: a.reshape(batch, seq, a.shape[-1])
    blk = lambda w: pl.BlockSpec((bg, PAIR, w), lambda bi, j: (bi, j, 0))
    o = pl.pallas_call(
        functools.partial(_gdn_kernel, bg=bg),
        grid=(batch // bg, seq // PAIR),
        in_specs=[blk(GDN_WIDTH), blk(GDN_WIDTH), blk(GDN_WIDTH), blk(GDN_WIDTH), blk(LANES),
                  pl.BlockSpec((bg, GATE_ROWS, PAIR), lambda bi, j: (bi, 0, j)), _const_spec(nw.shape)],
        out_specs=blk(GDN_WIDTH),
        out_shape=jax.ShapeDtypeStruct((batch, seq, GDN_WIDTH), BF16),
        scratch_shapes=[pltpu.VMEM((bg * GDN_HEADS, GDN_HEAD_DIM, GDN_HEAD_DIM), F32)],
        compiler_params=pltpu.CompilerParams(
            dimension_semantics=("arbitrary", "arbitrary"), vmem_limit_bytes=VMEM_LIMIT),
        name="gdn",
    )(seq3(gq), seq3(gk), seq3(gv), seq3(gz), seq3(gates), gates_t, nw)
    return o.reshape(m, GDN_WIDTH)


def _fox_kernel(qa_ref, ka_ref, vt_ref, o_ref, m_ref, acc_ref, *, tq):
    i = pl.program_id(2)
    heads = qa_ref.shape[1] // LANES
    qas = [qa_ref[:, hh * LANES:(hh + 1) * LANES] for hh in range(heads)]

    m_ref[...] = jnp.full(m_ref.shape, NEG_BIG, F32)
    acc_ref[...] = jnp.zeros(acc_ref.shape, F32)

    def chunk(start, size, mask_off):
        ka = ka_ref[pl.ds(start, size), :]
        sts = [_dot_nt(ka[:, hh * LANES:(hh + 1) * LANES], qas[hh]) for hh in range(heads)]
        if mask_off is not None:
            ri = lax.broadcasted_iota(jnp.int32, (size, tq), 0)
            ci = lax.broadcasted_iota(jnp.int32, (size, tq), 1)
            keep = (ri - ci) <= mask_off
            sts = [jnp.where(keep, s, NEG_BIG) for s in sts]
        m_olds = [m_ref[hh] for hh in range(heads)]
        m_news = [jnp.maximum(m, jnp.max(s, axis=0, keepdims=True)) for m, s in zip(m_olds, sts)]
        ps = [jnp.exp2(s - m).astype(BF16) for s, m in zip(sts, m_news)]
        pvs = [_dot(vt_ref[0, hh * FOX_VT_ROWS:(hh + 1) * FOX_VT_ROWS, pl.ds(start, size)], ps[hh])
               for hh in range(heads)]
        for hh in range(heads):
            alpha = jnp.exp2(m_olds[hh] - m_news[hh])
            rows = slice(hh * FOX_VT_ROWS, (hh + 1) * FOX_VT_ROWS)
            m_ref[hh] = m_news[hh]
            acc_ref[rows, :] = alpha * acc_ref[rows, :] + pvs[hh]

    def body(j, carry):
        chunk(pl.multiple_of(j * 2 * tq, 2 * tq), 2 * tq, None)
        return carry

    lax.fori_loop(0, i // 2, body, 0)

    @pl.when(i % 2 == 0)
    def _():
        chunk(pl.multiple_of(i * tq, tq), tq, 0)

    @pl.when(i % 2 == 1)
    def _():
        chunk(pl.multiple_of((i - 1) * tq, tq), 2 * tq, tq)

    out_t = jnp.concatenate(
        [acc_ref[hh * FOX_VT_ROWS:hh * FOX_VT_ROWS + FOX_HEAD_DIM, :]
         / acc_ref[hh * FOX_VT_ROWS + FOX_HEAD_DIM:hh * FOX_VT_ROWS + FOX_HEAD_DIM + 1, :]
         for hh in range(heads)], axis=0)
    o_ref[...] = out_t.T.astype(o_ref.dtype)


def _fox(fqa, fka, fvt, *, batch, seq):
    m = fqa.shape[0]
    tq = TQ_FOX
    steps = seq // tq
    heads = HEADS_FOX
    return pl.pallas_call(
        functools.partial(_fox_kernel, tq=tq),
        grid=(batch, FOX_HEADS // heads, steps),
        in_specs=[pl.BlockSpec((tq, heads * LANES), lambda b, p, i: (b * steps + i, p)),
                  pl.BlockSpec((seq, heads * LANES), lambda b, p, i: (b, p)),
                  pl.BlockSpec((1, heads * FOX_VT_ROWS, seq), lambda b, p, i: (b, p, 0))],
        out_specs=pl.BlockSpec((tq, heads * FOX_HEAD_DIM), lambda b, p, i: (b * steps + i, p)),
        out_shape=jax.ShapeDtypeStruct((m, FOX_WIDTH), BF16),
        scratch_shapes=[pltpu.VMEM((heads, 1, tq), F32), pltpu.VMEM((heads * FOX_VT_ROWS, tq), F32)],
        compiler_params=pltpu.CompilerParams(
            dimension_semantics=("arbitrary", "arbitrary", "arbitrary"), vmem_limit_bytes=VMEM_LIMIT),
        name="fox",
    )(fqa, fka, fvt)


def _post_kernel(x_ref, og_ref, of_ref, p_ref, wo_c, wgu_c, wdown_c, wpg_c, wpp_c,
                 ffn_nw_ref, ple_nw_ref, fin_nw_ref, out_ref,
                 wo_ref, wgate_ref, wup_ref, wdown_ref, wpg_ref, wpp_ref, act_ref, *, w_steps, ts, tf):
    step = pl.program_id(0)
    d_ff = wgate_ref.shape[1]

    @pl.when(step < w_steps)
    def _():
        def put(dst_ref, chunk):
            rows = chunk.shape[0]
            dst_ref[pl.ds(pl.multiple_of(step * rows, rows), rows), :] = chunk.astype(BF16)

        put(wo_ref, wo_c[...])
        put(wgate_ref, wgu_c[:, 0:d_ff])
        put(wup_ref, wgu_c[:, d_ff:2 * d_ff])
        put(wdown_ref, wdown_c[...])
        put(wpg_ref, wpg_c[...])
        put(wpp_ref, wpp_c[...])

    @pl.when(step >= w_steps)
    def _():
        tm = x_ref.shape[0]
        groups = [slice(r, r + ts) for r in range(0, tm, ts)]
        hs = [x_ref[g, :] + _dot(og_ref[g, :], wo_ref[0:GDN_WIDTH, :])
              + _dot(of_ref[g, :], wo_ref[GDN_WIDTH:GDN_WIDTH + FOX_WIDTH, :]) for g in groups]

        hns = [(h * _rms_scale(h) * ffn_nw_ref[...]).astype(BF16) for h in hs]
        for g, hn in zip(groups, hns):
            for c in range(d_ff // tf):
                cols = slice(c * tf, (c + 1) * tf)
                gate = _dot(hn, wgate_ref[:, cols])
                up = _dot(hn, wup_ref[:, cols])
                act_ref[g, cols] = (_silu(gate) * up).astype(BF16)
        hs = [h + _dot(act_ref[g, :], wdown_ref[...]) for g, h in zip(groups, hs)]

        hns = [(h * _rms_scale(h) * ple_nw_ref[...]).astype(BF16) for h in hs]
        ple_gates = [_sigmoid(_dot(hn, wpg_ref[...])) for hn in hns]
        hs = [h + pg * _dot(p_ref[g, :].astype(BF16), wpp_ref[...]) for g, h, pg in zip(groups, hs, ple_gates)]

        for g, h in zip(groups, hs):
            out_ref[g, :] = h * _rms_scale(h) * fin_nw_ref[...]


def _post(x2, o_g, o_f, p2, w_out, w_gate_up, w_down, w_ple_gate, w_ple_proj, ffn_nw, ple_nw, fin_nw):
    m, d = x2.shape
    tm = TM_POST
    w_steps = W_STEPS_POST
    d_ff = w_down.shape[0]
    tok = lambda w: pl.BlockSpec((tm, w), lambda s: (jnp.maximum(s - w_steps, 0), 0))
    wchunk = lambda w: pl.BlockSpec((w.shape[0] // w_steps, w.shape[1]), lambda s: (jnp.minimum(s, w_steps - 1), 0))
    weights = (w_out, w_gate_up, w_down, w_ple_gate, w_ple_proj)
    norms = (ffn_nw, ple_nw, fin_nw)
    in_specs = ([tok(d), tok(GDN_WIDTH), tok(FOX_WIDTH), tok(p2.shape[1])] + [wchunk(w) for w in weights]
                + [_const_spec(w.shape) for w in norms])
    return pl.pallas_call(
        functools.partial(_post_kernel, w_steps=w_steps, ts=TS_POST, tf=TF_POST),
        grid=(w_steps + m // tm,), in_specs=in_specs, out_specs=tok(d),
        out_shape=jax.ShapeDtypeStruct((m, d), F32),
        scratch_shapes=[pltpu.VMEM(w_out.shape, BF16), pltpu.VMEM((d, d_ff), BF16), pltpu.VMEM((d, d_ff), BF16),
                        pltpu.VMEM(w_down.shape, BF16), pltpu.VMEM(w_ple_gate.shape, BF16),
                        pltpu.VMEM(w_ple_proj.shape, BF16), pltpu.VMEM((tm, d_ff), BF16)],
        compiler_params=pltpu.CompilerParams(dimension_semantics=("arbitrary",), vmem_limit_bytes=VMEM_LIMIT),
        name="post",
    )(x2, o_g, o_f, p2, *weights, *norms)


def _layer(x, p_i, attn_norm_w, w_in, conv_w, a_log, dt_bias, gdn_norm_w, fox_f_bias, w_out,
           ffn_norm_w, w_gate_up, w_down, ple_norm_w, w_ple_gate, w_ple_proj, out_norm_w):
    batch, seq, d = x.shape
    m = batch * seq
    assert seq % TM_IN == 0 and seq % PAIR == 0 and seq % TQ_FOX == 0 and m % TM_POST == 0
    assert batch % BG_GDN == 0
    row = lambda w: w.reshape(1, -1).astype(F32)

    c0 = 3 * GDN_WIDTH
    c1 = c0 + GDN_WIDTH
    c2 = c1 + 2 * GDN_HEADS
    n_small = 2 * GDN_HEADS + FOX_HEADS
    zeros4 = jnp.zeros((GDN_HEADS,), F32)
    pad = jnp.zeros((LANES - n_small,), F32)
    prm = jnp.stack([
        jnp.concatenate([dt_bias.astype(F32), zeros4, fox_f_bias.astype(F32), pad]),
        jnp.concatenate([a_log.astype(F32), zeros4, jnp.zeros((FOX_HEADS,), F32), pad]),
    ])
    prm = jnp.pad(prm, ((0, SUBLANES - prm.shape[0]), (0, 0)))

    x2 = x.reshape(m, d)
    gq, gk, gv, gz, fqa, fka, fvt, gates, gates_t = _inproj(
        x2, row(attn_norm_w), w_in.astype(F32).T, (c0, c1, c2), conv_w.astype(F32), prm, batch=batch, seq=seq)
    o_g = _gdn(gq, gk, gv, gz, gates, gates_t, row(gdn_norm_w), batch=batch, seq=seq)
    o_f = _fox(fqa, fka, fvt, batch=batch, seq=seq)

    return _post(
        x2, o_g, o_f, p_i.reshape(m, -1),
        w_out.astype(F32), w_gate_up.astype(F32), w_down.astype(F32), w_ple_gate.astype(F32),
        w_ple_proj.astype(F32), row(ffn_norm_w), row(ple_norm_w), row(out_norm_w),
    ).reshape(batch, seq, d)


def kernel(x, p, attn_norm_w, w_in, conv_w, a_log, dt_bias, gdn_norm_w, fox_f_bias, w_out, ffn_norm_w,
           w_gate_up, w_down, ple_norm_w, w_ple_gate, w_ple_proj, final_norm_w):
    assert p.shape[0] == 1, "single-layer problem"
    return _layer(x, p[0], attn_norm_w[0], w_in[0], conv_w[0], a_log[0], dt_bias[0], gdn_norm_w[0],
                  fox_f_bias[0], w_out[0], ffn_norm_w[0], w_gate_up[0], w_down[0], ple_norm_w[0],
                  w_ple_gate[0], w_ple_proj[0], final_norm_w)
```

```python
import functools

import jax
import jax.numpy as jnp
import numpy as np
from jax import lax
from jax.experimental import pallas as pl
from jax.experimental.pallas import tpu as pltpu

F32 = jnp.float32
BF16 = jnp.bfloat16

EPS = 1e-6
GDN_HEADS = 4
GDN_HEAD_DIM = 128
GDN_WIDTH = GDN_HEADS * GDN_HEAD_DIM
FOX_HEADS = 8
FOX_HEAD_DIM = 64
FOX_WIDTH = FOX_HEADS * FOX_HEAD_DIM
CONV_WIDTH = 4
CHUNK = 64
PAIR = 2 * CHUNK
LANES = 128
SUBLANES = 8
BF16_ROWS = 16
FOX_VT_ROWS = FOX_HEAD_DIM + BF16_ROWS
NEG_BIG = -1e30
LOG2E = 1.4426950408889634

GC_LANE = 0
BETA_LANE = 4
CF_LANE = 8
GATE_ROWS = 16

VMEM_LIMIT = 56 * 1024 * 1024

TM_IN = 512
TS_IN = 512
BG_GDN = 4
TQ_FOX = 256
HEADS_FOX = 8
TM_POST = 512
TS_POST = 256
TF_POST = 256
W_STEPS_POST = 8


def _dot(a, b):
    return jnp.dot(a, b, preferred_element_type=F32)


def _dot_nt(a, b):
    return lax.dot_general(a, b, (((1,), (1,)), ((), ())), preferred_element_type=F32)


def _dot_tn(a, b):
    return lax.dot_general(a, b, (((0,), (0,)), ((), ())), preferred_element_type=F32)


def _dot_inv(a, b):
    return _dot(a.astype(BF16), b.astype(BF16))


def _sigmoid(x):
    return 0.5 * jnp.tanh(0.5 * x) + 0.5


def _silu(x):
    h = 0.5 * x
    return h * jnp.tanh(h) + h


def _rms_scale(x):
    return lax.rsqrt(jnp.mean(x * x, axis=-1, keepdims=True) + EPS)


def _const_spec(shape):
    nd = len(shape)
    return pl.BlockSpec(shape, lambda *_: (0,) * nd, pipeline_mode=pl.Buffered(1))


def _inproj_kernel(x_ref, nw_ref, wt_ref, conv_ref, prm_ref, sel_ref,
                   gq_ref, gk_ref, gv_ref, gz_ref, fqa_ref, fka_ref, fvt_ref, gates_ref, gates_t_ref,
                   buf_ref, carry_ref, wg_ref, wz_ref, wf_ref, wfvt_ref, ws_ref, *, splits, **tile_params):
    step = pl.program_id(0)

    @pl.when(step == 0)
    def _():
        c0, c1, c2 = splits
        c3 = c2 + wf_ref.shape[0]
        c4 = c3 + wfvt_ref.shape[0]
        wg_ref[...] = wt_ref[0:c0, :].astype(BF16)
        wz_ref[...] = wt_ref[c0:c1, :].astype(BF16)
        wf_ref[...] = wt_ref[c2:c3, :].astype(BF16)
        wfvt_ref[...] = wt_ref[c3:c4, :].astype(BF16)
        small = jnp.concatenate([wt_ref[c1:c2, :], wt_ref[c4:wt_ref.shape[0], :]], axis=0)
        ws_ref[...] = jnp.zeros(ws_ref.shape, BF16)
        ws_ref[0:small.shape[0], :] = small.astype(BF16)

    @pl.when(step > 0)
    def _():
        _inproj_tile(step - 1, x_ref, nw_ref, wg_ref, wz_ref, wf_ref, wfvt_ref, ws_ref, conv_ref, prm_ref,
                     sel_ref, gq_ref, gk_ref, gv_ref, gz_ref, fqa_ref, fka_ref, fvt_ref, gates_ref, gates_t_ref,
                     buf_ref, carry_ref, **tile_params)


def _inproj_tile(i, x_ref, nw_ref, wg_ref, wz_ref, wf_ref, wfvt_ref, ws_ref, conv_ref, prm_ref, sel_ref,
                 gq_ref, gk_ref, gv_ref, gz_ref, fqa_ref, fka_ref, fvt_ref, gates_ref, gates_t_ref,
                 buf_ref, carry_ref, *, tm, ts, tiles_per_seq):
    seq_start = (i % tiles_per_seq) == 0
    halo = SUBLANES

    @pl.when(seq_start)
    def _():
        buf_ref[0:halo, :] = jnp.zeros((halo, buf_ref.shape[1]), F32)
        carry_ref[...] = jnp.zeros_like(carry_ref)

    @pl.when(jnp.logical_not(seq_start))
    def _():
        buf_ref[0:halo, :] = buf_ref[tm:tm + halo, :]

    lane = lax.broadcasted_iota(jnp.int32, (ts, LANES), 1)
    row = lax.broadcasted_iota(jnp.int32, (ts, LANES), 0)
    is_gc = lane < BETA_LANE
    is_cf = (lane >= CF_LANE) & (lane < CF_LANE + FOX_HEADS)
    pos = jnp.where(is_gc, row % CHUNK, row)
    summed = is_gc | is_cf
    out_refs = (gq_ref, gk_ref, gv_ref)

    for r0 in range(0, tm, ts):
        rows = slice(r0, r0 + ts)
        x = x_ref[rows, :]
        xn = (x * _rms_scale(x) * nw_ref[...]).astype(BF16)

        buf_ref[halo + r0:halo + r0 + ts, :] = _dot_nt(xn, wg_ref[...])
        for j in range(3 * GDN_HEADS):
            cols = slice(j * LANES, (j + 1) * LANES)
            acc = buf_ref[halo + r0:halo + r0 + ts, cols] * conv_ref[CONV_WIDTH - 1:CONV_WIDTH, cols]
            for t in range(CONV_WIDTH - 1):
                off = halo + r0 - (CONV_WIDTH - 1) + t
                acc = acc + buf_ref[off:off + ts, cols] * conv_ref[t:t + 1, cols]
            y = _silu(acc)
            kind, head = divmod(j, GDN_HEADS)
            if kind < 2:
                y = y * lax.rsqrt(jnp.sum(y * y, axis=-1, keepdims=True) + EPS)
            if kind == 0:
                y = y * (GDN_HEAD_DIM ** -0.5)
            out_refs[kind][rows, head * LANES:(head + 1) * LANES] = y.astype(BF16)

        gz_ref[rows, :] = _dot_nt(xn, wz_ref[...]).astype(BF16)

        vt = _dot_nt(wfvt_ref[...], xn).astype(BF16)
        for h in range(FOX_HEADS):
            v0 = h * FOX_VT_ROWS
            fvt_ref[0, v0:v0 + FOX_HEAD_DIM, rows] = vt[h * FOX_HEAD_DIM:(h + 1) * FOX_HEAD_DIM]
            fvt_ref[0, v0 + FOX_HEAD_DIM:v0 + FOX_VT_ROWS, rows] = jnp.ones((FOX_VT_ROWS - FOX_HEAD_DIM, ts), BF16)

        z = _dot_nt(xn, ws_ref[...]) + prm_ref[0:1, :]
        soft = jnp.log1p(jnp.exp(-jnp.abs(z)))
        g_decay = -jnp.exp(prm_ref[1:2, :]) * (jnp.maximum(z, 0.0) + soft)
        log_f = -(jnp.maximum(-z, 0.0) + soft)
        val = jnp.where(is_gc, g_decay, jnp.where(lane < CF_LANE, _sigmoid(z), jnp.where(is_cf, log_f, 0.0)))

        d = 1
        while d < ts:
            shifted = pltpu.roll(val, d, axis=0)
            val = val + jnp.where(summed & (pos >= d), shifted, 0.0)
            d *= 2
        val = val + jnp.where(is_cf, carry_ref[...], 0.0)
        carry_ref[...] = val[ts - 1:ts, :]

        gates_ref[rows, :] = val
        gates_t_ref[0, :, rows] = val.T[0:GATE_ROWS, :]

        f_qk = _dot_nt(xn, wf_ref[...])
        c2 = jnp.where(is_cf, val * LOG2E, 0.0)
        c_hi = c2.astype(BF16).astype(F32)
        c_mid = (c2 - c_hi).astype(BF16).astype(F32)
        c_lo = c2 - c_hi - c_mid
        parts = (jnp.where(lane == 0, 1.0, c_hi) + pltpu.roll(c_mid, FOX_HEADS, axis=1)
                 + pltpu.roll(c_lo, 2 * FOX_HEADS, axis=1))
        bias = _dot(parts.astype(BF16), sel_ref[...])
        for h in range(FOX_HEADS):
            src, half = divmod(h, 2)
            in_half = (lane >= half * FOX_HEAD_DIM) & (lane < (half + 1) * FOX_HEAD_DIM)
            q_slab = f_qk[:, src * LANES:(src + 1) * LANES] * (FOX_HEAD_DIM ** -0.5 * LOG2E)
            k_slab = f_qk[:, FOX_WIDTH + src * LANES:FOX_WIDTH + (src + 1) * LANES]
            q_bias = bias[:, h * LANES:(h + 1) * LANES]
            k_bias = bias[:, (FOX_HEADS + h) * LANES:(FOX_HEADS + h + 1) * LANES]
            fqa_ref[rows, h * LANES:(h + 1) * LANES] = jnp.where(in_half, q_slab, q_bias).astype(BF16)
            fka_ref[rows, h * LANES:(h + 1) * LANES] = jnp.where(in_half, k_slab, k_bias).astype(BF16)


def _bias_selector():
    sel = np.zeros((LANES, 2 * FOX_HEADS * LANES), np.float32)
    for h in range(FOX_HEADS):
        base = (1 - h % 2) * FOX_HEAD_DIM
        q0 = h * LANES + base
        k0 = (FOX_HEADS + h) * LANES + base
        for part in range(3):
            src = CF_LANE + part * FOX_HEADS + h
            sel[src, q0 + part] = 1.0
            sel[0, q0 + 3 + part] = -1.0
            sel[0, k0 + part] = 1.0
            sel[src, k0 + 3 + part] = 1.0
    return jnp.asarray(sel, BF16)


def _inproj(x2, nw, w_t, splits, conv_w, prm, *, batch, seq):
    sel = _bias_selector()
    m, d = x2.shape
    tm = TM_IN
    tiles_per_seq = seq // tm
    grid = (1 + m // tm,)
    tile = lambda s: jnp.maximum(s - 1, 0)
    tok = lambda w: pl.BlockSpec((tm, w), lambda s: (tile(s), 0))
    seq_t = lambda rows: pl.BlockSpec(
        (1, rows, tm), lambda s: (tile(s) // tiles_per_seq, 0, tile(s) % tiles_per_seq))
    c0, c1, c2 = splits
    out_shape = (
        jax.ShapeDtypeStruct((m, GDN_WIDTH), BF16),
        jax.ShapeDtypeStruct((m, GDN_WIDTH), BF16),
        jax.ShapeDtypeStruct((m, GDN_WIDTH), BF16),
        jax.ShapeDtypeStruct((m, GDN_WIDTH), BF16),
        jax.ShapeDtypeStruct((m, FOX_HEADS * LANES), BF16),
        jax.ShapeDtypeStruct((m, FOX_HEADS * LANES), BF16),
        jax.ShapeDtypeStruct((batch, FOX_HEADS * FOX_VT_ROWS, seq), BF16),
        jax.ShapeDtypeStruct((m, LANES), F32),
        jax.ShapeDtypeStruct((batch, GATE_ROWS, seq), F32),
    )
    out_specs = (
        tok(GDN_WIDTH), tok(GDN_WIDTH), tok(GDN_WIDTH), tok(GDN_WIDTH),
        tok(FOX_HEADS * LANES), tok(FOX_HEADS * LANES), seq_t(FOX_HEADS * FOX_VT_ROWS), tok(LANES),
        seq_t(GATE_ROWS),
    )
    in_specs = [
        tok(d), _const_spec(nw.shape), _const_spec(w_t.shape), _const_spec(conv_w.shape),
        _const_spec(prm.shape), _const_spec(sel.shape),
    ]
    return pl.pallas_call(
        functools.partial(_inproj_kernel, splits=splits, tm=tm, ts=TS_IN, tiles_per_seq=tiles_per_seq),
        grid=grid, in_specs=in_specs, out_specs=out_specs, out_shape=out_shape,
        scratch_shapes=[pltpu.VMEM((tm + SUBLANES, 3 * GDN_WIDTH), F32), pltpu.VMEM((1, LANES), F32),
                        pltpu.VMEM((c0, d), BF16), pltpu.VMEM((c1 - c0, d), BF16),
                        pltpu.VMEM((2 * FOX_WIDTH, d), BF16), pltpu.VMEM((FOX_WIDTH, d), BF16),
                        pltpu.VMEM((LANES, d), BF16)],
        compiler_params=pltpu.CompilerParams(dimension_semantics=("arbitrary",), vmem_limit_bytes=VMEM_LIMIT),
        name="inproj",
    )(x2, nw, w_t, conv_w, prm, sel)


def _gdn_kernel(q_ref, k_ref, v_ref, z_ref, gates_ref, gates_t_ref, nw_ref, o_ref, state_ref, *, bg):
    @pl.when(pl.program_id(1) == 0)
    def _():
        state_ref[...] = jnp.zeros_like(state_ref)

    ri = lax.broadcasted_iota(jnp.int32, (PAIR, PAIR), 0)
    ci = lax.broadcasted_iota(jnp.int32, (PAIR, PAIR), 1)
    same_chunk = (ri // CHUNK) == (ci // CHUNK)
    lower = same_chunk & (ri >= ci)
    strict = same_chunk & (ri > ci)
    eye = (ri == ci).astype(F32)
    first_chunk_row = lax.broadcasted_iota(jnp.int32, (PAIR, LANES), 0) < CHUNK
    nw = nw_ref[...]

    chains = [(b, h) for b in range(bg) for h in range(GDN_HEADS)]
    cols_of = lambda h: slice(h * GDN_HEAD_DIM, (h + 1) * GDN_HEAD_DIM)
    col_of = lambda gt, lane: gt[:, lane:lane + 1]

    gts = [gates_ref[b] for b in range(bg)]
    gt_ts = [gates_t_ref[b] for b in range(bg)]
    k16s, a_mats, decays, e_gcs = [], [], [], []
    for b, h in chains:
        k16 = k_ref[b, :, cols_of(h)]
        k = k16.astype(F32)
        gc = col_of(gts[b], GC_LANE + h)
        gc_row = gt_ts[b][GC_LANE + h:GC_LANE + h + 1, :]
        decay = jnp.where(lower, jnp.exp(jnp.where(lower, gc - gc_row, 0.0)), 0.0)
        a_mats.append(jnp.where(strict, _dot_nt((k * col_of(gts[b], BETA_LANE + h)).astype(BF16), k16) * decay, 0.0))
        k16s.append(k16)
        decays.append(decay)
        e_gcs.append(jnp.exp(gc))

    invs = [eye - a for a in a_mats]
    pows = a_mats
    for _ in range(5):
        pows = [_dot_inv(a, a) for a in pows]
        invs = [_dot_inv(t, eye + a) for t, a in zip(invs, pows)]

    us, wqs, qks, kdts, decs = [], [], [], [], []
    for n, (b, h) in enumerate(chains):
        gc = col_of(gts[b], GC_LANE + h)
        beta = col_of(gts[b], BETA_LANE + h)
        q16 = q_ref[b, :, cols_of(h)]
        q = q16.astype(F32)
        k = k16s[n].astype(F32)
        inv16 = invs[n].astype(BF16)
        us.append(_dot(inv16, (v_ref[b, :, cols_of(h)].astype(F32) * beta).astype(BF16)))
        w = _dot(inv16, ((k * beta) * e_gcs[n]).astype(BF16)).astype(BF16)
        qe = (q * e_gcs[n]).astype(BF16)
        wqs.append([jnp.concatenate([w[c * CHUNK:(c + 1) * CHUNK], qe[c * CHUNK:(c + 1) * CHUNK]], axis=0)
                    for c in range(PAIR // CHUNK)])
        qks.append(jnp.where(lower, _dot_nt(q16, k16s[n]) * decays[n], 0.0).astype(BF16))
        g_last = jnp.where(first_chunk_row, gts[b][CHUNK - 1:CHUNK, :], gts[b][PAIR - 1:PAIR, :])
        g_last = col_of(g_last, GC_LANE + h)
        kdts.append((k * jnp.exp(g_last - gc)).T.astype(BF16))
        decs.append(jnp.exp(g_last))

    states = [state_ref[b * GDN_HEADS + h] for b, h in chains]
    zeros = jnp.zeros((CHUNK, GDN_HEAD_DIM), BF16)
    for c in range(PAIR // CHUNK):
        rows = slice(c * CHUNK, (c + 1) * CHUNK)
        rs = [_dot(wqs[n][c], states[n].astype(BF16)) for n in range(len(chains))]
        v_news = [(us[n][rows] - rs[n][0:CHUNK]).astype(BF16) for n in range(len(chains))]
        rhs = [jnp.concatenate([v, zeros] if c == 0 else [zeros, v], axis=0) for v in v_news]
        upds = [_dot(kdts[n], rhs[n]) for n in range(len(chains))]
        states = [states[n] * decs[n][(c + 1) * CHUNK - 1:(c + 1) * CHUNK, :] + upds[n] for n in range(len(chains))]
        for n, (b, h) in enumerate(chains):
            o = rs[n][CHUNK:2 * CHUNK] + _dot(qks[n][rows], rhs[n])
            o = o * lax.rsqrt(jnp.mean(o * o, axis=-1, keepdims=True) + EPS) * nw
            o_ref[b, rows, cols_of(h)] = (o * _silu(z_ref[b, rows, cols_of(h)].astype(F32))).astype(o_ref.dtype)
    for n, (b, h) in enumerate(chains):
        state_ref[b * GDN_HEADS + h] = states[n]


def _gdn(gq, gk, gv, gz, gates, gates_t, nw, *, batch, seq):
    m = gq.shape[0]
    bg = BG_GDN
    seq3 = lambda a: a.reshape(batch, seq, a.shape[-1])
    blk = lambda w: pl.BlockSpec((bg, PAIR, w), lambda bi, j: (bi, j, 0))
    o = pl.pallas_call(
        functools.partial(_gdn_kernel, bg=bg),
        grid=(batch // bg, seq // PAIR),
        in_specs=[blk(GDN_WIDTH), blk(GDN_WIDTH), blk(GDN_WIDTH), blk(GDN_WIDTH), blk(LANES),
                  pl.BlockSpec((bg, GATE_ROWS, PAIR), lambda bi, j: (bi, 0, j)), _const_spec(nw.shape)],
        out_specs=blk(GDN_WIDTH),
        out_shape=jax.ShapeDtypeStruct((batch, seq, GDN_WIDTH), BF16),
        scratch_shapes=[pltpu.VMEM((bg * GDN_HEADS, GDN_HEAD_DIM, GDN_HEAD_DIM), F32)],
        compiler_params=pltpu.CompilerParams(
            dimension_semantics=("arbitrary", "arbitrary"), vmem_limit_bytes=VMEM_LIMIT),
        name="gdn",
    )(seq3(gq), seq3(gk), seq3(gv), seq3(gz), seq3(gates), gates_t, nw)
    return o.reshape(m, GDN_WIDTH)


def _fox_kernel(qa_ref, ka_ref, vt_ref, o_ref, m_ref, acc_ref, *, tq):
    i = pl.program_id(2)
    heads = qa_ref.shape[1] // LANES
    qas = [qa_ref[:, hh * LANES:(hh + 1) * LANES] for hh in range(heads)]

    m_ref[...] = jnp.full(m_ref.shape, NEG_BIG, F32)
    acc_ref[...] = jnp.zeros(acc_ref.shape, F32)

    def chunk(start, size, mask_off):
        ka = ka_ref[pl.ds(start, size), :]
        sts = [_dot_nt(ka[:, hh * LANES:(hh + 1) * LANES], qas[hh]) for hh in range(heads)]
        if mask_off is not None:
            ri = lax.broadcasted_iota(jnp.int32, (size, tq), 0)
            ci = lax.broadcasted_iota(jnp.int32, (size, tq), 1)
            keep = (ri - ci) <= mask_off
            sts = [jnp.where(keep, s, NEG_BIG) for s in sts]
        m_olds = [m_ref[hh] for hh in range(heads)]
        m_news = [jnp.maximum(m, jnp.max(s, axis=0, keepdims=True)) for m, s in zip(m_olds, sts)]
        ps = [jnp.exp2(s - m).astype(BF16) for s, m in zip(sts, m_news)]
        pvs = [_dot(vt_ref[0, hh * FOX_VT_ROWS:(hh + 1) * FOX_VT_ROWS, pl.ds(start, size)], ps[hh])
               for hh in range(heads)]
        for hh in range(heads):
            alpha = jnp.exp2(m_olds[hh] - m_news[hh])
            rows = slice(hh * FOX_VT_ROWS, (hh + 1) * FOX_VT_ROWS)
            m_ref[hh] = m_news[hh]
            acc_ref[rows, :] = alpha * acc_ref[rows, :] + pvs[hh]

    def body(j, carry):
        chunk(pl.multiple_of(j * 2 * tq, 2 * tq), 2 * tq, None)
        return carry

    lax.fori_loop(0, i // 2, body, 0)

    @pl.when(i % 2 == 0)
    def _():
        chunk(pl.multiple_of(i * tq, tq), tq, 0)

    @pl.when(i % 2 == 1)
    def _():
        chunk(pl.multiple_of((i - 1) * tq, tq), 2 * tq, tq)

    out_t = jnp.concatenate(
        [acc_ref[hh * FOX_VT_ROWS:hh * FOX_VT_ROWS + FOX_HEAD_DIM, :]
         / acc_ref[hh * FOX_VT_ROWS + FOX_HEAD_DIM:hh * FOX_VT_ROWS + FOX_HEAD_DIM + 1, :]
         for hh in range(heads)], axis=0)
    o_ref[...] = out_t.T.astype(o_ref.dtype)


def _fox(fqa, fka, fvt, *, batch, seq):
    m = fqa.shape[0]
    tq = TQ_FOX
    steps = seq // tq
    heads = HEADS_FOX
    return pl.pallas_call(
        functools.partial(_fox_kernel, tq=tq),
        grid=(batch, FOX_HEADS // heads, steps),
        in_specs=[pl.BlockSpec((tq, heads * LANES), lambda b, p, i: (b * steps + i, p)),
                  pl.BlockSpec((seq, heads * LANES), lambda b, p, i: (b, p)),
                  pl.BlockSpec((1, heads * FOX_VT_ROWS, seq), lambda b, p, i: (b, p, 0))],
        out_specs=pl.BlockSpec((tq, heads * FOX_HEAD_DIM), lambda b, p, i: (b * steps + i, p)),
        out_shape=jax.ShapeDtypeStruct((m, FOX_WIDTH), BF16),
        scratch_shapes=[pltpu.VMEM((heads, 1, tq), F32), pltpu.VMEM((heads * FOX_VT_ROWS, tq), F32)],
        compiler_params=pltpu.CompilerParams(
            dimension_semantics=("arbitrary", "arbitrary", "arbitrary"), vmem_limit_bytes=VMEM_LIMIT),
        name="fox",
    )(fqa, fka, fvt)


def _post_kernel(x_ref, og_ref, of_ref, p_ref, wo_c, wgu_c, wdown_c, wpg_c, wpp_c,
                 ffn_nw_ref, ple_nw_ref, fin_nw_ref, out_ref,
                 wo_ref, wgate_ref, wup_ref, wdown_ref, wpg_ref, wpp_ref, act_ref, *, w_steps, ts, tf):
    step = pl.program_id(0)
    d_ff = wgate_ref.shape[1]

    @pl.when(step < w_steps)
    def _():
        def put(dst_ref, chunk):
            rows = chunk.shape[0]
            dst_ref[pl.ds(pl.multiple_of(step * rows, rows), rows), :] = chunk.astype(BF16)

        put(wo_ref, wo_c[...])
        put(wgate_ref, wgu_c[:, 0:d_ff])
        put(wup_ref, wgu_c[:, d_ff:2 * d_ff])
        put(wdown_ref, wdown_c[...])
        put(wpg_ref, wpg_c[...])
        put(wpp_ref, wpp_c[...])

    @pl.when(step >= w_steps)
    def _():
        tm = x_ref.shape[0]
        groups = [slice(r, r + ts) for r in range(0, tm, ts)]
        hs = [x_ref[g, :] + _dot(og_ref[g, :], wo_ref[0:GDN_WIDTH, :])
              + _dot(of_ref[g, :], wo_ref[GDN_WIDTH:GDN_WIDTH + FOX_WIDTH, :]) for g in groups]

        hns = [(h * _rms_scale(h) * ffn_nw_ref[...]).astype(BF16) for h in hs]
        for g, hn in zip(groups, hns):
            for c in range(d_ff // tf):
                cols = slice(c * tf, (c + 1) * tf)
                gate = _dot(hn, wgate_ref[:, cols])
                up = _dot(hn, wup_ref[:, cols])
                act_ref[g, cols] = (_silu(gate) * up).astype(BF16)
        hs = [h + _dot(act_ref[g, :], wdown_ref[...]) for g, h in zip(groups, hs)]

        hns = [(h * _rms_scale(h) * ple_nw_ref[...]).astype(BF16) for h in hs]
        ple_gates = [_sigmoid(_dot(hn, wpg_ref[...])) for hn in hns]
        hs = [h + pg * _dot(p_ref[g, :].astype(BF16), wpp_ref[...]) for g, h, pg in zip(groups, hs, ple_gates)]

        for g, h in zip(groups, hs):
            out_ref[g, :] = h * _rms_scale(h) * fin_nw_ref[...]


def _post(x2, o_g, o_f, p2, w_out, w_gate_up, w_down, w_ple_gate, w_ple_proj, ffn_nw, ple_nw, fin_nw):
    m, d = x2.shape
    tm = TM_POST
    w_steps = W_STEPS_POST
    d_ff = w_down.shape[0]
    tok = lambda w: pl.BlockSpec((tm, w), lambda s: (jnp.maximum(s - w_steps, 0), 0))
    wchunk = lambda w: pl.BlockSpec((w.shape[0] // w_steps, w.shape[1]), lambda s: (jnp.minimum(s, w_steps - 1), 0))
    weights = (w_out, w_gate_up, w_down, w_ple_gate, w_ple_proj)
    norms = (ffn_nw, ple_nw, fin_nw)
    in_specs = ([tok(d), tok(GDN_WIDTH), tok(FOX_WIDTH), tok(p2.shape[1])] + [wchunk(w) for w in weights]
                + [_const_spec(w.shape) for w in norms])
    return pl.pallas_call(
        functools.partial(_post_kernel, w_steps=w_steps, ts=TS_POST, tf=TF_POST),
        grid=(w_steps + m // tm,), in_specs=in_specs, out_specs=tok(d),
        out_shape=jax.ShapeDtypeStruct((m, d), F32),
        scratch_shapes=[pltpu.VMEM(w_out.shape, BF16), pltpu.VMEM((d, d_ff), BF16), pltpu.VMEM((d, d_ff), BF16),
                        pltpu.VMEM(w_down.shape, BF16), pltpu.VMEM(w_ple_gate.shape, BF16),
                        pltpu.VMEM(w_ple_proj.shape, BF16), pltpu.VMEM((tm, d_ff), BF16)],
        compiler_params=pltpu.CompilerParams(dimension_semantics=("arbitrary",), vmem_limit_bytes=VMEM_LIMIT),
        name="post",
    )(x2, o_g, o_f, p2, *weights, *norms)


def _layer(x, p_i, attn_norm_w, w_in, conv_w, a_log, dt_bias, gdn_norm_w, fox_f_bias, w_out,
           ffn_norm_w, w_gate_up, w_down, ple_norm_w, w_ple_gate, w_ple_proj, out_norm_w):
    batch, seq, d = x.shape
    m = batch * seq
    assert seq % TM_IN == 0 and seq % PAIR == 0 and seq % TQ_FOX == 0 and m % TM_POST == 0
    assert batch % BG_GDN == 0
    row = lambda w: w.reshape(1, -1).astype(F32)

    c0 = 3 * GDN_WIDTH
    c1 = c0 + GDN_WIDTH
    c2 = c1 + 2 * GDN_HEADS
    n_small = 2 * GDN_HEADS + FOX_HEADS
    zeros4 = jnp.zeros((GDN_HEADS,), F32)
    pad = jnp.zeros((LANES - n_small,), F32)
    prm = jnp.stack([
        jnp.concatenate([dt_bias.astype(F32), zeros4, fox_f_bias.astype(F32), pad]),
        jnp.concatenate([a_log.astype(F32), zeros4, jnp.zeros((FOX_HEADS,), F32), pad]),
    ])
    prm = jnp.pad(prm, ((0, SUBLANES - prm.shape[0]), (0, 0)))

    x2 = x.reshape(m, d)
    gq, gk, gv, gz, fqa, fka, fvt, gates, gates_t = _inproj(
        x2, row(attn_norm_w), w_in.astype(F32).T, (c0, c1, c2), conv_w.astype(F32), prm, batch=batch, seq=seq)
    o_g = _gdn(gq, gk, gv, gz, gates, gates_t, row(gdn_norm_w), batch=batch, seq=seq)
    o_f = _fox(fqa, fka, fvt, batch=batch, seq=seq)

    return _post(
        x2, o_g, o_f, p_i.reshape(m, -1),
        w_out.astype(F32), w_gate_up.astype(F32), w_down.astype(F32), w_ple_gate.astype(F32),
        w_ple_proj.astype(F32), row(ffn_norm_w), row(ple_norm_w), row(out_norm_w),
    ).reshape(batch, seq, d)


def kernel(x, p, attn_norm_w, w_in, conv_w, a_log, dt_bias, gdn_norm_w, fox_f_bias, w_out, ffn_norm_w,
           w_gate_up, w_down, ple_norm_w, w_ple_gate, w_ple_proj, final_norm_w):
    assert p.shape[0] == 1, "single-layer problem"
    return _layer(x, p[0], attn_norm_w[0], w_in[0], conv_w[0], a_log[0], dt_bias[0], gdn_norm_w[0],
                  fox_f_bias[0], w_out[0], ffn_norm_w[0], w_gate_up[0], w_down[0], ple_norm_w[0],
                  w_ple_gate[0], w_ple_proj[0], final_norm_w)
```

```python
import functools

import jax
import jax.numpy as jnp
import numpy as np
from jax import lax
from jax.experimental import pallas as pl
from jax.experimental.pallas import tpu as pltpu

F32 = jnp.float32
BF16 = jnp.bfloat16

EPS = 1e-6
GDN_HEADS = 4
GDN_HEAD_DIM = 128
GDN_WIDTH = GDN_HEADS * GDN_HEAD_DIM
FOX_HEADS = 8
FOX_HEAD_DIM = 64
FOX_WIDTH = FOX_HEADS * FOX_HEAD_DIM
CONV_WIDTH = 4
CHUNK = 64
PAIR = 2 * CHUNK
LANES = 128
SUBLANES = 8
BF16_ROWS = 16
FOX_VT_ROWS = FOX_HEAD_DIM + BF16_ROWS
NEG_BIG = -1e30
LOG2E = 1.4426950408889634

GC_LANE = 0
BETA_LANE = 4
CF_LANE = 8
GATE_ROWS = 16

VMEM_LIMIT = 56 * 1024 * 1024

TM_IN = 512
TS_IN = 512
BG_GDN = 4
TQ_FOX = 256
HEADS_FOX = 8
TM_POST = 512
TS_POST = 256
TF_POST = 256
W_STEPS_POST = 8


def _dot(a, b):
    return jnp.dot(a, b, preferred_element_type=F32)


def _dot_nt(a, b):
    return lax.dot_general(a, b, (((1,), (1,)), ((), ())), preferred_element_type=F32)


def _dot_tn(a, b):
    return lax.dot_general(a, b, (((0,), (0,)), ((), ())), preferred_element_type=F32)


def _dot_inv(a, b):
    return _dot(a.astype(BF16), b.astype(BF16))


def _sigmoid(x):
    return 0.5 * jnp.tanh(0.5 * x) + 0.5


def _silu(x):
    h = 0.5 * x
    return h * jnp.tanh(h) + h


def _rms_scale(x):
    return lax.rsqrt(jnp.mean(x * x, axis=-1, keepdims=True) + EPS)


def _const_spec(shape):
    nd = len(shape)
    return pl.BlockSpec(shape, lambda *_: (0,) * nd, pipeline_mode=pl.Buffered(1))


def _inproj_kernel(x_ref, nw_ref, wt_ref, conv_ref, prm_ref, sel_ref,
                   gq_ref, gk_ref, gv_ref, gz_ref, fqa_ref, fka_ref, fvt_ref, gates_ref, gates_t_ref,
                   buf_ref, carry_ref, wg_ref, wz_ref, wf_ref, wfvt_ref, ws_ref, *, splits, **tile_params):
    step = pl.program_id(0)

    @pl.when(step == 0)
    def _():
        c0, c1, c2 = splits
        c3 = c2 + wf_ref.shape[0]
        c4 = c3 + wfvt_ref.shape[0]
        wg_ref[...] = wt_ref[0:c0, :].astype(BF16)
        wz_ref[...] = wt_ref[c0:c1, :].astype(BF16)
        wf_ref[...] = wt_ref[c2:c3, :].astype(BF16)
        wfvt_ref[...] = wt_ref[c3:c4, :].astype(BF16)
        small = jnp.concatenate([wt_ref[c1:c2, :], wt_ref[c4:wt_ref.shape[0], :]], axis=0)
        ws_ref[...] = jnp.zeros(ws_ref.shape, BF16)
        ws_ref[0:small.shape[0], :] = small.astype(BF16)

    @pl.when(step > 0)
    def _():
        _inproj_tile(step - 1, x_ref, nw_ref, wg_ref, wz_ref, wf_ref, wfvt_ref, ws_ref, conv_ref, prm_ref,
                     sel_ref, gq_ref, gk_ref, gv_ref, gz_ref, fqa_ref, fka_ref, fvt_ref, gates_ref, gates_t_ref,
                     buf_ref, carry_ref, **tile_params)


def _inproj_tile(i, x_ref, nw_ref, wg_ref, wz_ref, wf_ref, wfvt_ref, ws_ref, conv_ref, prm_ref, sel_ref,
                 gq_ref, gk_ref, gv_ref, gz_ref, fqa_ref, fka_ref, fvt_ref, gates_ref, gates_t_ref,
                 buf_ref, carry_ref, *, tm, ts, tiles_per_seq):
    seq_start = (i % tiles_per_seq) == 0
    halo = SUBLANES

    @pl.when(seq_start)
    def _():
        buf_ref[0:halo, :] = jnp.zeros((halo, buf_ref.shape[1]), F32)
        carry_ref[...] = jnp.zeros_like(carry_ref)

    @pl.when(jnp.logical_not(seq_start))
    def _():
        buf_ref[0:halo, :] = buf_ref[tm:tm + halo, :]

    lane = lax.broadcasted_iota(jnp.int32, (ts, LANES), 1)
    row = lax.broadcasted_iota(jnp.int32, (ts, LANES), 0)
    is_gc = lane < BETA_LANE
    is_cf = (lane >= CF_LANE) & (lane < CF_LANE + FOX_HEADS)
    pos = jnp.where(is_gc, row % CHUNK, row)
    summed = is_gc | is_cf
    out_refs = (gq_ref, gk_ref, gv_ref)

    for r0 in range(0, tm, ts):
        rows = slice(r0, r0 + ts)
        x = x_ref[rows, :]
        xn = (x * _rms_scale(x) * nw_ref[...]).astype(BF16)

        buf_ref[halo + r0:halo + r0 + ts, :] = _dot_nt(xn, wg_ref[...])
        for j in range(3 * GDN_HEADS):
            cols = slice(j * LANES, (j + 1) * LANES)
            acc = buf_ref[halo + r0:halo + r0 + ts, cols] * conv_ref[CONV_WIDTH - 1:CONV_WIDTH, cols]
            for t in range(CONV_WIDTH - 1):
                off = halo + r0 - (CONV_WIDTH - 1) + t
                acc = acc + buf_ref[off:off + ts, cols] * conv_ref[t:t + 1, cols]
            y = _silu(acc)
            kind, head = divmod(j, GDN_HEADS)
            if kind < 2:
                y = y * lax.rsqrt(jnp.sum(y * y, axis=-1, keepdims=True) + EPS)
            if kind == 0:
                y = y * (GDN_HEAD_DIM ** -0.5)
            out_refs[kind][rows, head * LANES:(head + 1) * LANES] = y

        gz_ref[rows, :] = _dot_nt(xn, wz_ref[...])

        vt = _dot_nt(wfvt_ref[...], xn).astype(BF16)
        for h in range(FOX_HEADS):
            v0 = h * FOX_VT_ROWS
            fvt_ref[0, v0:v0 + FOX_HEAD_DIM, rows] = vt[h * FOX_HEAD_DIM:(h + 1) * FOX_HEAD_DIM]
            fvt_ref[0, v0 + FOX_HEAD_DIM:v0 + FOX_VT_ROWS, rows] = jnp.ones((FOX_VT_ROWS - FOX_HEAD_DIM, ts), BF16)

        z = _dot_nt(xn, ws_ref[...]) + prm_ref[0:1, :]
        soft = jnp.log1p(jnp.exp(-jnp.abs(z)))
        g_decay = -jnp.exp(prm_ref[1:2, :]) * (jnp.maximum(z, 0.0) + soft)
        log_f = -(jnp.maximum(-z, 0.0) + soft)
        val = jnp.where(is_gc, g_decay, jnp.where(lane < CF_LANE, _sigmoid(z), jnp.where(is_cf, log_f, 0.0)))

        d = 1
        while d < ts:
            shifted = pltpu.roll(val, d, axis=0)
            val = val + jnp.where(summed & (pos >= d), shifted, 0.0)
            d *= 2
        val = val + jnp.where(is_cf, carry_ref[...], 0.0)
        carry_ref[...] = val[ts - 1:ts, :]

        gates_ref[rows, :] = val
        gates_t_ref[0, :, rows] = val.T[0:GATE_ROWS, :]

        f_qk = _dot_nt(xn, wf_ref[...])
        c2 = jnp.where(is_cf, val * LOG2E, 0.0)
        c_hi = c2.astype(BF16).astype(F32)
        c_mid = (c2 - c_hi).astype(BF16).astype(F32)
        c_lo = c2 - c_hi - c_mid
        parts = (jnp.where(lane == 0, 1.0, c_hi) + pltpu.roll(c_mid, FOX_HEADS, axis=1)
                 + pltpu.roll(c_lo, 2 * FOX_HEADS, axis=1))
        bias = _dot(parts.astype(BF16), sel_ref[...])
        for h in range(FOX_HEADS):
            src, half = divmod(h, 2)
            in_half = (lane >= half * FOX_HEAD_DIM) & (lane < (half + 1) * FOX_HEAD_DIM)
            q_slab = f_qk[:, src * LANES:(src + 1) * LANES] * (FOX_HEAD_DIM ** -0.5 * LOG2E)
            k_slab = f_qk[:, FOX_WIDTH + src * LANES:FOX_WIDTH + (src + 1) * LANES]
            q_bias = bias[:, h * LANES:(h + 1) * LANES]
            k_bias = bias[:, (FOX_HEADS + h) * LANES:(FOX_HEADS + h + 1) * LANES]
            fqa_ref[rows, h * LANES:(h + 1) * LANES] = jnp.where(in_half, q_slab, q_bias).astype(BF16)
            fka_ref[rows, h * LANES:(h + 1) * LANES] = jnp.where(in_half, k_slab, k_bias).astype(BF16)


def _bias_selector():
    sel = np.zeros((LANES, 2 * FOX_HEADS * LANES), np.float32)
    for h in range(FOX_HEADS):
        base = (1 - h % 2) * FOX_HEAD_DIM
        q0 = h * LANES + base
        k0 = (FOX_HEADS + h) * LANES + base
        for part in range(3):
            src = CF_LANE + part * FOX_HEADS + h
            sel[src, q0 + part] = 1.0
            sel[0, q0 + 3 + part] = -1.0
            sel[0, k0 + part] = 1.0
            sel[src, k0 + 3 + part] = 1.0
    return jnp.asarray(sel, BF16)


def _inproj(x2, nw, w_t, splits, conv_w, prm, *, batch, seq):
    sel = _bias_selector()
    m, d = x2.shape
    tm = TM_IN
    tiles_per_seq = seq // tm
    grid = (1 + m // tm,)
    tile = lambda s: jnp.maximum(s - 1, 0)
    tok = lambda w: pl.BlockSpec((tm, w), lambda s: (tile(s), 0))
    seq_t = lambda rows: pl.BlockSpec(
        (1, rows, tm), lambda s: (tile(s) // tiles_per_seq, 0, tile(s) % tiles_per_seq))
    c0, c1, c2 = splits
    out_shape = (
        jax.ShapeDtypeStruct((m, GDN_WIDTH), F32),
        jax.ShapeDtypeStruct((m, GDN_WIDTH), F32),
        jax.ShapeDtypeStruct((m, GDN_WIDTH), F32),
        jax.ShapeDtypeStruct((m, GDN_WIDTH), F32),
        jax.ShapeDtypeStruct((m, FOX_HEADS * LANES), BF16),
        jax.ShapeDtypeStruct((m, FOX_HEADS * LANES), BF16),
        jax.ShapeDtypeStruct((batch, FOX_HEADS * FOX_VT_ROWS, seq), BF16),
        jax.ShapeDtypeStruct((m, LANES), F32),
        jax.ShapeDtypeStruct((batch, GATE_ROWS, seq), F32),
    )
    out_specs = (
        tok(GDN_WIDTH), tok(GDN_WIDTH), tok(GDN_WIDTH), tok(GDN_WIDTH),
        tok(FOX_HEADS * LANES), tok(FOX_HEADS * LANES), seq_t(FOX_HEADS * FOX_VT_ROWS), tok(LANES),
        seq_t(GATE_ROWS),
    )
    in_specs = [
        tok(d), _const_spec(nw.shape), _const_spec(w_t.shape), _const_spec(conv_w.shape),
        _const_spec(prm.shape), _const_spec(sel.shape),
    ]
    return pl.pallas_call(
        functools.partial(_inproj_kernel, splits=splits, tm=tm, ts=TS_IN, tiles_per_seq=tiles_per_seq),
        grid=grid, in_specs=in_specs, out_specs=out_specs, out_shape=out_shape,
        scratch_shapes=[pltpu.VMEM((tm + SUBLANES, 3 * GDN_WIDTH), F32), pltpu.VMEM((1, LANES), F32),
                        pltpu.VMEM((c0, d), BF16), pltpu.VMEM((c1 - c0, d), BF16),
                        pltpu.VMEM((2 * FOX_WIDTH, d), BF16), pltpu.VMEM((FOX_WIDTH, d), BF16),
                        pltpu.VMEM((LANES, d), BF16)],
        compiler_params=pltpu.CompilerParams(dimension_semantics=("arbitrary",), vmem_limit_bytes=VMEM_LIMIT),
        name="inproj",
    )(x2, nw, w_t, conv_w, prm, sel)


def _gdn_kernel(q_ref, k_ref, v_ref, z_ref, gates_ref, gates_t_ref, nw_ref, o_ref, state_ref, *, bg):
    @pl.when(pl.program_id(1) == 0)
    def _():
        state_ref[...] = jnp.zeros_like(state_ref)

    ri = lax.broadcasted_iota(jnp.int32, (PAIR, PAIR), 0)
    ci = lax.broadcasted_iota(jnp.int32, (PAIR, PAIR), 1)
    same_chunk = (ri // CHUNK) == (ci // CHUNK)
    lower = same_chunk & (ri >= ci)
    strict = same_chunk & (ri > ci)
    eye = (ri == ci).astype(F32)
    first_chunk_row = lax.broadcasted_iota(jnp.int32, (PAIR, LANES), 0) < CHUNK
    nw = nw_ref[...]

    chains = [(b, h) for b in range(bg) for h in range(GDN_HEADS)]
    cols_of = lambda h: slice(h * GDN_HEAD_DIM, (h + 1) * GDN_HEAD_DIM)
    col_of = lambda gt, lane: gt[:, lane:lane + 1]

    gts = [gates_ref[b] for b in range(bg)]
    gt_ts = [gates_t_ref[b] for b in range(bg)]
    k16s, a_mats, decays, e_gcs = [], [], [], []
    for b, h in chains:
        k = k_ref[b, :, cols_of(h)]
        gc = col_of(gts[b], GC_LANE + h)
        gc_row = gt_ts[b][GC_LANE + h:GC_LANE + h + 1, :]
        decay = jnp.where(lower, jnp.exp(jnp.where(lower, gc - gc_row, 0.0)), 0.0)
        k16 = k.astype(BF16)
        a_mats.append(jnp.where(strict, _dot_nt((k * col_of(gts[b], BETA_LANE + h)).astype(BF16), k16) * decay, 0.0))
        k16s.append(k16)
        decays.append(decay)
        e_gcs.append(jnp.exp(gc))

    invs = [eye - a for a in a_mats]
    pows = a_mats
    for _ in range(5):
        pows = [_dot_inv(a, a) for a in pows]
        invs = [_dot_inv(t, eye + a) for t, a in zip(invs, pows)]

    us, wqs, qks, kdts, decs = [], [], [], [], []
    for n, (b, h) in enumerate(chains):
        gc = col_of(gts[b], GC_LANE + h)
        beta = col_of(gts[b], BETA_LANE + h)
        q = q_ref[b, :, cols_of(h)]
        k = k_ref[b, :, cols_of(h)]
        inv16 = invs[n].astype(BF16)
        us.append(_dot(inv16, (v_ref[b, :, cols_of(h)] * beta).astype(BF16)))
        w = _dot(inv16, ((k * beta) * e_gcs[n]).astype(BF16)).astype(BF16)
        qe = (q * e_gcs[n]).astype(BF16)
        wqs.append([jnp.concatenate([w[c * CHUNK:(c + 1) * CHUNK], qe[c * CHUNK:(c + 1) * CHUNK]], axis=0)
                    for c in range(PAIR // CHUNK)])
        qks.append(jnp.where(lower, _dot_nt(q.astype(BF16), k16s[n]) * decays[n], 0.0).astype(BF16))
        g_last = jnp.where(first_chunk_row, gts[b][CHUNK - 1:CHUNK, :], gts[b][PAIR - 1:PAIR, :])
        g_last = col_of(g_last, GC_LANE + h)
        kdts.append((k * jnp.exp(g_last - gc)).T.astype(BF16))
        decs.append(jnp.exp(g_last))

    states = [state_ref[b * GDN_HEADS + h] for b, h in chains]
    zeros = jnp.zeros((CHUNK, GDN_HEAD_DIM), BF16)
    for c in range(PAIR // CHUNK):
        rows = slice(c * CHUNK, (c + 1) * CHUNK)
        rs = [_dot(wqs[n][c], states[n].astype(BF16)) for n in range(len(chains))]
        v_news = [(us[n][rows] - rs[n][0:CHUNK]).astype(BF16) for n in range(len(chains))]
        rhs = [jnp.concatenate([v, zeros] if c == 0 else [zeros, v], axis=0) for v in v_news]
        upds = [_dot(kdts[n], rhs[n]) for n in range(len(chains))]
        states = [states[n] * decs[n][(c + 1) * CHUNK - 1:(c + 1) * CHUNK, :] + upds[n] for n in range(len(chains))]
        for n, (b, h) in enumerate(chains):
            o = rs[n][CHUNK:2 * CHUNK] + _dot(qks[n][rows], rhs[n])
            o = o * lax.rsqrt(jnp.mean(o * o, axis=-1, keepdims=True) + EPS) * nw
            o_ref[b, rows, cols_of(h)] = (o * _silu(z_ref[b, rows, cols_of(h)])).astype(o_ref.dtype)
    for n, (b, h) in enumerate(chains):
        state_ref[b * GDN_HEADS + h] = states[n]


def _gdn(gq, gk, gv, gz, gates, gates_t, nw, *, batch, seq):
    m = gq.shape[0]
    bg = BG_GDN
    seq3 = lambda a: a.reshape(batch, seq, a.shape[-1])
    blk = lambda w: pl.BlockSpec((bg, PAIR, w), lambda bi, j: (bi, j, 0))
    o = pl.pallas_call(
        functools.partial(_gdn_kernel, bg=bg),
        grid=(batch // bg, seq // PAIR),
        in_specs=[blk(GDN_WIDTH), blk(GDN_WIDTH), blk(GDN_WIDTH), blk(GDN_WIDTH), blk(LANES),
                  pl.BlockSpec((bg, GATE_ROWS, PAIR), lambda bi, j: (bi, 0, j)), _const_spec(nw.shape)],
        out_specs=blk(GDN_WIDTH),
        out_shape=jax.ShapeDtypeStruct((batch, seq, GDN_WIDTH), BF16),
        scratch_shapes=[pltpu.VMEM((bg * GDN_HEADS, GDN_HEAD_DIM, GDN_HEAD_DIM), F32)],
        compiler_params=pltpu.CompilerParams(
            dimension_semantics=("arbitrary", "arbitrary"), vmem_limit_bytes=VMEM_LIMIT),
        name="gdn",
    )(seq3(gq), seq3(gk), seq3(gv), seq3(gz), seq3(gates), gates_t, nw)
    return o.reshape(m, GDN_WIDTH)


def _fox_kernel(qa_ref, ka_ref, vt_ref, o_ref, m_ref, acc_ref, *, tq):
    i = pl.program_id(2)
    heads = qa_ref.shape[1] // LANES
    qas = [qa_ref[:, hh * LANES:(hh + 1) * LANES] for hh in range(heads)]

    m_ref[...] = jnp.full(m_ref.shape, NEG_BIG, F32)
    acc_ref[...] = jnp.zeros(acc_ref.shape, F32)

    def chunk(start, size, mask_off):
        ka = ka_ref[pl.ds(start, size), :]
        sts = [_dot_nt(ka[:, hh * LANES:(hh + 1) * LANES], qas[hh]) for hh in range(heads)]
        if mask_off is not None:
            ri = lax.broadcasted_iota(jnp.int32, (size, tq), 0)
            ci = lax.broadcasted_iota(jnp.int32, (size, tq), 1)
            keep = (ri - ci) <= mask_off
            sts = [jnp.where(keep, s, NEG_BIG) for s in sts]
        m_olds = [m_ref[hh] for hh in range(heads)]
        m_news = [jnp.maximum(m, jnp.max(s, axis=0, keepdims=True)) for m, s in zip(m_olds, sts)]
        ps = [jnp.exp2(s - m).astype(BF16) for s, m in zip(sts, m_news)]
        pvs = [_dot(vt_ref[0, hh * FOX_VT_ROWS:(hh + 1) * FOX_VT_ROWS, pl.ds(start, size)], ps[hh])
               for hh in range(heads)]
        for hh in range(heads):
            alpha = jnp.exp2(m_olds[hh] - m_news[hh])
            rows = slice(hh * FOX_VT_ROWS, (hh + 1) * FOX_VT_ROWS)
            m_ref[hh] = m_news[hh]
            acc_ref[rows, :] = alpha * acc_ref[rows, :] + pvs[hh]

    def body(j, carry):
        chunk(pl.multiple_of(j * 2 * tq, 2 * tq), 2 * tq, None)
        return carry

    lax.fori_loop(0, i // 2, body, 0)

    @pl.when(i % 2 == 0)
    def _():
        chunk(pl.multiple_of(i * tq, tq), tq, 0)

    @pl.when(i % 2 == 1)
    def _():
        chunk(pl.multiple_of((i - 1) * tq, tq), 2 * tq, tq)

    out_t = jnp.concatenate(
        [acc_ref[hh * FOX_VT_ROWS:hh * FOX_VT_ROWS + FOX_HEAD_DIM, :]
         / acc_ref[hh * FOX_VT_ROWS + FOX_HEAD_DIM:hh * FOX_VT_ROWS + FOX_HEAD_DIM + 1, :]
         for hh in range(heads)], axis=0)
    o_ref[0] = out_t.astype(o_ref.dtype)


def _fox(fqa, fka, fvt, *, batch, seq):
    m = fqa.shape[0]
    tq = TQ_FOX
    steps = seq // tq
    heads = HEADS_FOX
    return pl.pallas_call(
        functools.partial(_fox_kernel, tq=tq),
        grid=(batch, FOX_HEADS // heads, steps),
        in_specs=[pl.BlockSpec((tq, heads * LANES), lambda b, p, i: (b * steps + i, p)),
                  pl.BlockSpec((seq, heads * LANES), lambda b, p, i: (b, p)),
                  pl.BlockSpec((1, heads * FOX_VT_ROWS, seq), lambda b, p, i: (b, p, 0))],
        out_specs=pl.BlockSpec((1, heads * FOX_HEAD_DIM, tq), lambda b, p, i: (b, p, i)),
        out_shape=jax.ShapeDtypeStruct((batch, FOX_WIDTH, seq), BF16),
        scratch_shapes=[pltpu.VMEM((heads, 1, tq), F32), pltpu.VMEM((heads * FOX_VT_ROWS, tq), F32)],
        compiler_params=pltpu.CompilerParams(
            dimension_semantics=("arbitrary", "arbitrary", "arbitrary"), vmem_limit_bytes=VMEM_LIMIT),
        name="fox",
    )(fqa, fka, fvt)


def _post_kernel(x_ref, og_ref, of_ref, p_ref, wo_c, wgu_c, wdown_c, wpg_c, wpp_c,
                 ffn_nw_ref, ple_nw_ref, fin_nw_ref, out_ref,
                 wo_ref, wgate_ref, wup_ref, wdown_ref, wpg_ref, wpp_ref, act_ref, *, w_steps, ts, tf):
    step = pl.program_id(0)
    d_ff = wgate_ref.shape[1]

    @pl.when(step < w_steps)
    def _():
        def put(dst_ref, chunk):
            rows = chunk.shape[0]
            dst_ref[pl.ds(pl.multiple_of(step * rows, rows), rows), :] = chunk.astype(BF16)

        put(wo_ref, wo_c[...])
        put(wgate_ref, wgu_c[:, 0:d_ff])
        put(wup_ref, wgu_c[:, d_ff:2 * d_ff])
        put(wdown_ref, wdown_c[...])
        put(wpg_ref, wpg_c[...])
        put(wpp_ref, wpp_c[...])

    @pl.when(step >= w_steps)
    def _():
        tm = x_ref.shape[0]
        groups = [slice(r, r + ts) for r in range(0, tm, ts)]
        hs = [x_ref[g, :] + _dot(og_ref[g, :], wo_ref[0:GDN_WIDTH, :])
              + _dot_tn(of_ref[0, :, g], wo_ref[GDN_WIDTH:GDN_WIDTH + FOX_WIDTH, :]) for g in groups]

        hns = [(h * _rms_scale(h) * ffn_nw_ref[...]).astype(BF16) for h in hs]
        for g, hn in zip(groups, hns):
            for c in range(d_ff // tf):
                cols = slice(c * tf, (c + 1) * tf)
                gate = _dot(hn, wgate_ref[:, cols])
                up = _dot(hn, wup_ref[:, cols])
                act_ref[g, cols] = (_silu(gate) * up).astype(BF16)
        hs = [h + _dot(act_ref[g, :], wdown_ref[...]) for g, h in zip(groups, hs)]

        hns = [(h * _rms_scale(h) * ple_nw_ref[...]).astype(BF16) for h in hs]
        ple_gates = [_sigmoid(_dot(hn, wpg_ref[...])) for hn in hns]
        hs = [h + pg * _dot(p_ref[g, :].astype(BF16), wpp_ref[...]) for g, h, pg in zip(groups, hs, ple_gates)]

        for g, h in zip(groups, hs):
            out_ref[g, :] = h * _rms_scale(h) * fin_nw_ref[...]


def _post(x2, o_g, o_f, p2, w_out, w_gate_up, w_down, w_ple_gate, w_ple_proj, ffn_nw, ple_nw, fin_nw):
    m, d = x2.shape
    tm = TM_POST
    w_steps = W_STEPS_POST
    d_ff = w_down.shape[0]
    tok = lambda w: pl.BlockSpec((tm, w), lambda s: (jnp.maximum(s - w_steps, 0), 0))
    wchunk = lambda w: pl.BlockSpec((w.shape[0] // w_steps, w.shape[1]), lambda s: (jnp.minimum(s, w_steps - 1), 0))
    weights = (w_out, w_gate_up, w_down, w_ple_gate, w_ple_proj)
    norms = (ffn_nw, ple_nw, fin_nw)
    tiles_per_seq = o_f.shape[2] // tm
    of_spec = pl.BlockSpec((1, FOX_WIDTH, tm), lambda s: (jnp.maximum(s - w_steps, 0) // tiles_per_seq, 0,
                                                           jnp.maximum(s - w_steps, 0) % tiles_per_seq))
    in_specs = ([tok(d), tok(GDN_WIDTH), of_spec, tok(p2.shape[1])] + [wchunk(w) for w in weights]
                + [_const_spec(w.shape) for w in norms])
    return pl.pallas_call(
        functools.partial(_post_kernel, w_steps=w_steps, ts=TS_POST, tf=TF_POST),
        grid=(w_steps + m // tm,), in_specs=in_specs, out_specs=tok(d),
        out_shape=jax.ShapeDtypeStruct((m, d), F32),
        scratch_shapes=[pltpu.VMEM(w_out.shape, BF16), pltpu.VMEM((d, d_ff), BF16), pltpu.VMEM((d, d_ff), BF16),
                        pltpu.VMEM(w_down.shape, BF16), pltpu.VMEM(w_ple_gate.shape, BF16),
                        pltpu.VMEM(w_ple_proj.shape, BF16), pltpu.VMEM((tm, d_ff), BF16)],
        compiler_params=pltpu.CompilerParams(dimension_semantics=("arbitrary",), vmem_limit_bytes=VMEM_LIMIT),
        name="post",
    )(x2, o_g, o_f, p2, *weights, *norms)


def _layer(x, p_i, attn_norm_w, w_in, conv_w, a_log, dt_bias, gdn_norm_w, fox_f_bias, w_out,
           ffn_norm_w, w_gate_up, w_down, ple_norm_w, w_ple_gate, w_ple_proj, out_norm_w):
    batch, seq, d = x.shape
    m = batch * seq
    assert seq % TM_IN == 0 and seq % PAIR == 0 and seq % TQ_FOX == 0 and seq % TM_POST == 0
    assert batch % BG_GDN == 0
    row = lambda w: w.reshape(1, -1).astype(F32)

    c0 = 3 * GDN_WIDTH
    c1 = c0 + GDN_WIDTH
    c2 = c1 + 2 * GDN_HEADS
    n_small = 2 * GDN_HEADS + FOX_HEADS
    zeros4 = jnp.zeros((GDN_HEADS,), F32)
    pad = jnp.zeros((LANES - n_small,), F32)
    prm = jnp.stack([
        jnp.concatenate([dt_bias.astype(F32), zeros4, fox_f_bias.astype(F32), pad]),
        jnp.concatenate([a_log.astype(F32), zeros4, jnp.zeros((FOX_HEADS,), F32), pad]),
    ])
    prm = jnp.pad(prm, ((0, SUBLANES - prm.shape[0]), (0, 0)))

    x2 = x.reshape(m, d)
    gq, gk, gv, gz, fqa, fka, fvt, gates, gates_t = _inproj(
        x2, row(attn_norm_w), w_in.astype(F32).T, (c0, c1, c2), conv_w.astype(F32), prm, batch=batch, seq=seq)
    o_g = _gdn(gq, gk, gv, gz, gates, gates_t, row(gdn_norm_w), batch=batch, seq=seq)
    o_f = _fox(fqa, fka, fvt, batch=batch, seq=seq)

    return _post(
        x2, o_g, o_f, p_i.reshape(m, -1),
        w_out.astype(F32), w_gate_up.astype(F32), w_down.astype(F32), w_ple_gate.astype(F32),
        w_ple_proj.astype(F32), row(ffn_norm_w), row(ple_norm_w), row(out_norm_w),
    ).reshape(batch, seq, d)


def kernel(x, p, attn_norm_w, w_in, conv_w, a_log, dt_bias, gdn_norm_w, fox_f_bias, w_out, ffn_norm_w,
           w_gate_up, w_down, ple_norm_w, w_ple_gate, w_ple_proj, final_norm_w):
    assert p.shape[0] == 1, "single-layer problem"
    return _layer(x, p[0], attn_norm_w[0], w_in[0], conv_w[0], a_log[0], dt_bias[0], gdn_norm_w[0],
                  fox_f_bias[0], w_out[0], ffn_norm_w[0], w_gate_up[0], w_down[0], ple_norm_w[0],
                  w_ple_gate[0], w_ple_proj[0], final_norm_w)
```

```python
import functools

import jax
import jax.numpy as jnp
import numpy as np
from jax import lax
from jax.experimental import pallas as pl
from jax.experimental.pallas import tpu as pltpu

F32 = jnp.float32
BF16 = jnp.bfloat16

EPS = 1e-6
GDN_HEADS = 4
GDN_HEAD_DIM = 128
GDN_WIDTH = GDN_HEADS * GDN_HEAD_DIM
FOX_HEADS = 8
FOX_HEAD_DIM = 64
FOX_WIDTH = FOX_HEADS * FOX_HEAD_DIM
CONV_WIDTH = 4
CHUNK = 64
PAIR = 2 * CHUNK
LANES = 128
SUBLANES = 8
BF16_ROWS = 16
FOX_VT_ROWS = FOX_HEAD_DIM + BF16_ROWS
NEG_BIG = -1e30
LOG2E = 1.4426950408889634

GC_LANE = 0
BETA_LANE = 4
CF_LANE = 8
GATE_ROWS = 16

VMEM_LIMIT = 56 * 1024 * 1024

TM_IN = 512
TS_IN = 512
BG_GDN = 4
TQ_FOX = 512
HEADS_FOX = 8
TM_POST = 512
TS_POST = 256
TF_POST = 256
W_STEPS_POST = 8


def _dot(a, b):
    return jnp.dot(a, b, preferred_element_type=F32)


def _dot_nt(a, b):
    return lax.dot_general(a, b, (((1,), (1,)), ((), ())), preferred_element_type=F32)


def _dot_tn(a, b):
    return lax.dot_general(a, b, (((0,), (0,)), ((), ())), preferred_element_type=F32)


def _dot_inv(a, b):
    return _dot(a.astype(BF16), b.astype(BF16))


def _sigmoid(x):
    return 0.5 * jnp.tanh(0.5 * x) + 0.5


def _silu(x):
    h = 0.5 * x
    return h * jnp.tanh(h) + h


def _rms_scale(x):
    return lax.rsqrt(jnp.mean(x * x, axis=-1, keepdims=True) + EPS)


def _const_spec(shape):
    nd = len(shape)
    return pl.BlockSpec(shape, lambda *_: (0,) * nd, pipeline_mode=pl.Buffered(1))


def _inproj_kernel(x_ref, nw_ref, wt_ref, conv_ref, prm_ref, sel_ref,
                   gq_ref, gk_ref, gv_ref, gz_ref, fqa_ref, fka_ref, fvt_ref, gates_ref, gates_t_ref,
                   buf_ref, carry_ref, wg_ref, wz_ref, wf_ref, wfvt_ref, ws_ref, *, splits, **tile_params):
    step = pl.program_id(0)

    @pl.when(step == 0)
    def _():
        c0, c1, c2 = splits
        c3 = c2 + wf_ref.shape[0]
        c4 = c3 + wfvt_ref.shape[0]
        wg_ref[...] = wt_ref[0:c0, :].astype(BF16)
        wz_ref[...] = wt_ref[c0:c1, :].astype(BF16)
        wf_ref[...] = wt_ref[c2:c3, :].astype(BF16)
        wfvt_ref[...] = wt_ref[c3:c4, :].astype(BF16)
        small = jnp.concatenate([wt_ref[c1:c2, :], wt_ref[c4:wt_ref.shape[0], :]], axis=0)
        ws_ref[...] = jnp.zeros(ws_ref.shape, BF16)
        ws_ref[0:small.shape[0], :] = small.astype(BF16)

    @pl.when(step > 0)
    def _():
        _inproj_tile(step - 1, x_ref, nw_ref, wg_ref, wz_ref, wf_ref, wfvt_ref, ws_ref, conv_ref, prm_ref,
                     sel_ref, gq_ref, gk_ref, gv_ref, gz_ref, fqa_ref, fka_ref, fvt_ref, gates_ref, gates_t_ref,
                     buf_ref, carry_ref, **tile_params)


def _inproj_tile(i, x_ref, nw_ref, wg_ref, wz_ref, wf_ref, wfvt_ref, ws_ref, conv_ref, prm_ref, sel_ref,
                 gq_ref, gk_ref, gv_ref, gz_ref, fqa_ref, fka_ref, fvt_ref, gates_ref, gates_t_ref,
                 buf_ref, carry_ref, *, tm, ts, tiles_per_seq):
    seq_start = (i % tiles_per_seq) == 0
    halo = SUBLANES

    @pl.when(seq_start)
    def _():
        buf_ref[0:halo, :] = jnp.zeros((halo, buf_ref.shape[1]), F32)
        carry_ref[...] = jnp.zeros_like(carry_ref)

    @pl.when(jnp.logical_not(seq_start))
    def _():
        buf_ref[0:halo, :] = buf_ref[tm:tm + halo, :]

    lane = lax.broadcasted_iota(jnp.int32, (ts, LANES), 1)
    row = lax.broadcasted_iota(jnp.int32, (ts, LANES), 0)
    is_gc = lane < BETA_LANE
    is_cf = (lane >= CF_LANE) & (lane < CF_LANE + FOX_HEADS)
    pos = jnp.where(is_gc, row % CHUNK, row)
    summed = is_gc | is_cf
    out_refs = (gq_ref, gk_ref, gv_ref)

    for r0 in range(0, tm, ts):
        rows = slice(r0, r0 + ts)
        x = x_ref[rows, :]
        xn = (x * _rms_scale(x) * nw_ref[...]).astype(BF16)

        buf_ref[halo + r0:halo + r0 + ts, :] = _dot_nt(xn, wg_ref[...])
        for j in range(3 * GDN_HEADS):
            cols = slice(j * LANES, (j + 1) * LANES)
            acc = buf_ref[halo + r0:halo + r0 + ts, cols] * conv_ref[CONV_WIDTH - 1:CONV_WIDTH, cols]
            for t in range(CONV_WIDTH - 1):
                off = halo + r0 - (CONV_WIDTH - 1) + t
                acc = acc + buf_ref[off:off + ts, cols] * conv_ref[t:t + 1, cols]
            y = _silu(acc)
            kind, head = divmod(j, GDN_HEADS)
            if kind < 2:
                y = y * lax.rsqrt(jnp.sum(y * y, axis=-1, keepdims=True) + EPS)
            if kind == 0:
                y = y * (GDN_HEAD_DIM ** -0.5)
            out_refs[kind][rows, head * LANES:(head + 1) * LANES] = y

        gz_ref[rows, :] = _dot_nt(xn, wz_ref[...])

        vt = _dot_nt(wfvt_ref[...], xn).astype(BF16)
        for h in range(FOX_HEADS):
            v0 = h * FOX_VT_ROWS
            fvt_ref[0, v0:v0 + FOX_HEAD_DIM, rows] = vt[h * FOX_HEAD_DIM:(h + 1) * FOX_HEAD_DIM]
            fvt_ref[0, v0 + FOX_HEAD_DIM:v0 + FOX_VT_ROWS, rows] = jnp.ones((FOX_VT_ROWS - FOX_HEAD_DIM, ts), BF16)

        z = _dot_nt(xn, ws_ref[...]) + prm_ref[0:1, :]
        soft = jnp.log1p(jnp.exp(-jnp.abs(z)))
        g_decay = -jnp.exp(prm_ref[1:2, :]) * (jnp.maximum(z, 0.0) + soft)
        log_f = -(jnp.maximum(-z, 0.0) + soft)
        val = jnp.where(is_gc, g_decay, jnp.where(lane < CF_LANE, _sigmoid(z), jnp.where(is_cf, log_f, 0.0)))

        d = 1
        while d < ts:
            shifted = pltpu.roll(val, d, axis=0)
            val = val + jnp.where(summed & (pos >= d), shifted, 0.0)
            d *= 2
        val = val + jnp.where(is_cf, carry_ref[...], 0.0)
        carry_ref[...] = val[ts - 1:ts, :]

        gates_ref[rows, :] = val
        gates_t_ref[0, :, rows] = val.T[0:GATE_ROWS, :]

        f_qk = _dot_nt(xn, wf_ref[...])
        c2 = jnp.where(is_cf, val * LOG2E, 0.0)
        c_hi = c2.astype(BF16).astype(F32)
        c_mid = (c2 - c_hi).astype(BF16).astype(F32)
        c_lo = c2 - c_hi - c_mid
        parts = (jnp.where(lane == 0, 1.0, c_hi) + pltpu.roll(c_mid, FOX_HEADS, axis=1)
                 + pltpu.roll(c_lo, 2 * FOX_HEADS, axis=1))
        bias = _dot(parts.astype(BF16), sel_ref[...])
        for h in range(FOX_HEADS):
            src, half = divmod(h, 2)
            in_half = (lane >= half * FOX_HEAD_DIM) & (lane < (half + 1) * FOX_HEAD_DIM)
            q_slab = f_qk[:, src * LANES:(src + 1) * LANES] * (FOX_HEAD_DIM ** -0.5 * LOG2E)
            k_slab = f_qk[:, FOX_WIDTH + src * LANES:FOX_WIDTH + (src + 1) * LANES]
            q_bias = bias[:, h * LANES:(h + 1) * LANES]
            k_bias = bias[:, (FOX_HEADS + h) * LANES:(FOX_HEADS + h + 1) * LANES]
            fqa_ref[rows, h * LANES:(h + 1) * LANES] = jnp.where(in_half, q_slab, q_bias).astype(BF16)
            fka_ref[rows, h * LANES:(h + 1) * LANES] = jnp.where(in_half, k_slab, k_bias).astype(BF16)


def _bias_selector():
    sel = np.zeros((LANES, 2 * FOX_HEADS * LANES), np.float32)
    for h in range(FOX_HEADS):
        base = (1 - h % 2) * FOX_HEAD_DIM
        q0 = h * LANES + base
        k0 = (FOX_HEADS + h) * LANES + base
        for part in range(3):
            src = CF_LANE + part * FOX_HEADS + h
            sel[src, q0 + part] = 1.0
            sel[0, q0 + 3 + part] = -1.0
            sel[0, k0 + part] = 1.0
            sel[src, k0 + 3 + part] = 1.0
    return jnp.asarray(sel, BF16)


def _inproj(x2, nw, w_t, splits, conv_w, prm, *, batch, seq):
    sel = _bias_selector()
    m, d = x2.shape
    tm = TM_IN
    tiles_per_seq = seq // tm
    grid = (1 + m // tm,)
    tile = lambda s: jnp.maximum(s - 1, 0)
    tok = lambda w: pl.BlockSpec((tm, w), lambda s: (tile(s), 0))
    seq_t = lambda rows: pl.BlockSpec(
        (1, rows, tm), lambda s: (tile(s) // tiles_per_seq, 0, tile(s) % tiles_per_seq))
    c0, c1, c2 = splits
    out_shape = (
        jax.ShapeDtypeStruct((m, GDN_WIDTH), F32),
        jax.ShapeDtypeStruct((m, GDN_WIDTH), F32),
        jax.ShapeDtypeStruct((m, GDN_WIDTH), F32),
        jax.ShapeDtypeStruct((m, GDN_WIDTH), F32),
        jax.ShapeDtypeStruct((m, FOX_HEADS * LANES), BF16),
        jax.ShapeDtypeStruct((m, FOX_HEADS * LANES), BF16),
        jax.ShapeDtypeStruct((batch, FOX_HEADS * FOX_VT_ROWS, seq), BF16),
        jax.ShapeDtypeStruct((m, LANES), F32),
        jax.ShapeDtypeStruct((batch, GATE_ROWS, seq), F32),
    )
    out_specs = (
        tok(GDN_WIDTH), tok(GDN_WIDTH), tok(GDN_WIDTH), tok(GDN_WIDTH),
        tok(FOX_HEADS * LANES), tok(FOX_HEADS * LANES), seq_t(FOX_HEADS * FOX_VT_ROWS), tok(LANES),
        seq_t(GATE_ROWS),
    )
    in_specs = [
        tok(d), _const_spec(nw.shape), _const_spec(w_t.shape), _const_spec(conv_w.shape),
        _const_spec(prm.shape), _const_spec(sel.shape),
    ]
    return pl.pallas_call(
        functools.partial(_inproj_kernel, splits=splits, tm=tm, ts=TS_IN, tiles_per_seq=tiles_per_seq),
        grid=grid, in_specs=in_specs, out_specs=out_specs, out_shape=out_shape,
        scratch_shapes=[pltpu.VMEM((tm + SUBLANES, 3 * GDN_WIDTH), F32), pltpu.VMEM((1, LANES), F32),
                        pltpu.VMEM((c0, d), BF16), pltpu.VMEM((c1 - c0, d), BF16),
                        pltpu.VMEM((2 * FOX_WIDTH, d), BF16), pltpu.VMEM((FOX_WIDTH, d), BF16),
                        pltpu.VMEM((LANES, d), BF16)],
        compiler_params=pltpu.CompilerParams(dimension_semantics=("arbitrary",), vmem_limit_bytes=VMEM_LIMIT),
        name="inproj",
    )(x2, nw, w_t, conv_w, prm, sel)


def _gdn_kernel(q_ref, k_ref, v_ref, z_ref, gates_ref, gates_t_ref, nw_ref, o_ref, state_ref, *, bg):
    @pl.when(pl.program_id(1) == 0)
    def _():
        state_ref[...] = jnp.zeros_like(state_ref)

    ri = lax.broadcasted_iota(jnp.int32, (PAIR, PAIR), 0)
    ci = lax.broadcasted_iota(jnp.int32, (PAIR, PAIR), 1)
    same_chunk = (ri // CHUNK) == (ci // CHUNK)
    lower = same_chunk & (ri >= ci)
    strict = same_chunk & (ri > ci)
    eye = (ri == ci).astype(F32)
    first_chunk_row = lax.broadcasted_iota(jnp.int32, (PAIR, LANES), 0) < CHUNK
    nw = nw_ref[...]

    chains = [(b, h) for b in range(bg) for h in range(GDN_HEADS)]
    cols_of = lambda h: slice(h * GDN_HEAD_DIM, (h + 1) * GDN_HEAD_DIM)
    col_of = lambda gt, lane: gt[:, lane:lane + 1]

    gts = [gates_ref[b] for b in range(bg)]
    gt_ts = [gates_t_ref[b] for b in range(bg)]
    k16s, a_mats, decays, e_gcs = [], [], [], []
    for b, h in chains:
        k = k_ref[b, :, cols_of(h)]
        gc = col_of(gts[b], GC_LANE + h)
        gc_row = gt_ts[b][GC_LANE + h:GC_LANE + h + 1, :]
        decay = jnp.where(lower, jnp.exp(jnp.where(lower, gc - gc_row, 0.0)), 0.0)
        k16 = k.astype(BF16)
        a_mats.append(jnp.where(strict, _dot_nt((k * col_of(gts[b], BETA_LANE + h)).astype(BF16), k16) * decay, 0.0))
        k16s.append(k16)
        decays.append(decay)
        e_gcs.append(jnp.exp(gc))

    invs = [eye - a for a in a_mats]
    pows = a_mats
    for _ in range(5):
        pows = [_dot_inv(a, a) for a in pows]
        invs = [_dot_inv(t, eye + a) for t, a in zip(invs, pows)]

    us, wqs, qks, kdts, decs = [], [], [], [], []
    for n, (b, h) in enumerate(chains):
        gc = col_of(gts[b], GC_LANE + h)
        beta = col_of(gts[b], BETA_LANE + h)
        q = q_ref[b, :, cols_of(h)]
        k = k_ref[b, :, cols_of(h)]
        inv16 = invs[n].astype(BF16)
        us.append(_dot(inv16, (v_ref[b, :, cols_of(h)] * beta).astype(BF16)))
        w = _dot(inv16, ((k * beta) * e_gcs[n]).astype(BF16)).astype(BF16)
        qe = (q * e_gcs[n]).astype(BF16)
        wqs.append([jnp.concatenate([w[c * CHUNK:(c + 1) * CHUNK], qe[c * CHUNK:(c + 1) * CHUNK]], axis=0)
                    for c in range(PAIR // CHUNK)])
        qks.append(jnp.where(lower, _dot_nt(q.astype(BF16), k16s[n]) * decays[n], 0.0).astype(BF16))
        g_last = jnp.where(first_chunk_row, gts[b][CHUNK - 1:CHUNK, :], gts[b][PAIR - 1:PAIR, :])
        g_last = col_of(g_last, GC_LANE + h)
        kdts.append((k * jnp.exp(g_last - gc)).T.astype(BF16))
        decs.append(jnp.exp(g_last))

    states = [state_ref[b * GDN_HEADS + h] for b, h in chains]
    zeros = jnp.zeros((CHUNK, GDN_HEAD_DIM), BF16)
    for c in range(PAIR // CHUNK):
        rows = slice(c * CHUNK, (c + 1) * CHUNK)
        rs = [_dot(wqs[n][c], states[n].astype(BF16)) for n in range(len(chains))]
        v_news = [(us[n][rows] - rs[n][0:CHUNK]).astype(BF16) for n in range(len(chains))]
        rhs = [jnp.concatenate([v, zeros] if c == 0 else [zeros, v], axis=0) for v in v_news]
        upds = [_dot(kdts[n], rhs[n]) for n in range(len(chains))]
        states = [states[n] * decs[n][(c + 1) * CHUNK - 1:(c + 1) * CHUNK, :] + upds[n] for n in range(len(chains))]
        for n, (b, h) in enumerate(chains):
            o = rs[n][CHUNK:2 * CHUNK] + _dot(qks[n][rows], rhs[n])
            o = o * lax.rsqrt(jnp.mean(o * o, axis=-1, keepdims=True) + EPS) * nw
            o_ref[b, rows, cols_of(h)] = (o * _silu(z_ref[b, rows, cols_of(h)])).astype(o_ref.dtype)
    for n, (b, h) in enumerate(chains):
        state_ref[b * GDN_HEADS + h] = states[n]


def _gdn(gq, gk, gv, gz, gates, gates_t, nw, *, batch, seq):
    m = gq.shape[0]
    bg = BG_GDN
    seq3 = lambda a: a.reshape(batch, seq, a.shape[-1])
    blk = lambda w: pl.BlockSpec((bg, PAIR, w), lambda bi, j: (bi, j, 0))
    o = pl.pallas_call(
        functools.partial(_gdn_kernel, bg=bg),
        grid=(batch // bg, seq // PAIR),
        in_specs=[blk(GDN_WIDTH), blk(GDN_WIDTH), blk(GDN_WIDTH), blk(GDN_WIDTH), blk(LANES),
                  pl.BlockSpec((bg, GATE_ROWS, PAIR), lambda bi, j: (bi, 0, j)), _const_spec(nw.shape)],
        out_specs=blk(GDN_WIDTH),
        out_shape=jax.ShapeDtypeStruct((batch, seq, GDN_WIDTH), BF16),
        scratch_shapes=[pltpu.VMEM((bg * GDN_HEADS, GDN_HEAD_DIM, GDN_HEAD_DIM), F32)],
        compiler_params=pltpu.CompilerParams(
            dimension_semantics=("arbitrary", "arbitrary"), vmem_limit_bytes=VMEM_LIMIT),
        name="gdn",
    )(seq3(gq), seq3(gk), seq3(gv), seq3(gz), seq3(gates), gates_t, nw)
    return o.reshape(m, GDN_WIDTH)


def _fox_kernel(qa_ref, ka_ref, vt_ref, o_ref, m_ref, acc_ref, *, tq):
    i = pl.program_id(2)
    heads = qa_ref.shape[1] // LANES

    m_ref[...] = jnp.full(m_ref.shape, NEG_BIG, F32)
    acc_ref[...] = jnp.zeros(acc_ref.shape, F32)

    def chunk(start, size, q0, nq, masked):
        ka = ka_ref[pl.ds(start, size), :]
        sts = [_dot_nt(ka[:, hh * LANES:(hh + 1) * LANES], qa_ref[q0:q0 + nq, hh * LANES:(hh + 1) * LANES])
               for hh in range(heads)]
        if masked:
            ri = lax.broadcasted_iota(jnp.int32, (size, nq), 0)
            ci = lax.broadcasted_iota(jnp.int32, (size, nq), 1)
            keep = ri <= ci
            sts = [jnp.where(keep, s, NEG_BIG) for s in sts]
        qs = slice(q0, q0 + nq)
        m_olds = [m_ref[hh][:, qs] for hh in range(heads)]
        m_news = [jnp.maximum(m, jnp.max(s, axis=0, keepdims=True)) for m, s in zip(m_olds, sts)]
        ps = [jnp.exp2(s - m).astype(BF16) for s, m in zip(sts, m_news)]
        pvs = [_dot(vt_ref[0, hh * FOX_VT_ROWS:(hh + 1) * FOX_VT_ROWS, pl.ds(start, size)], ps[hh])
               for hh in range(heads)]
        for hh in range(heads):
            alpha = jnp.exp2(m_olds[hh] - m_news[hh])
            rows = slice(hh * FOX_VT_ROWS, (hh + 1) * FOX_VT_ROWS)
            m_ref[hh, :, qs] = m_news[hh]
            acc_ref[rows, qs] = alpha * acc_ref[rows, qs] + pvs[hh]

    def body(j, carry):
        chunk(pl.multiple_of(j * tq, tq), tq, 0, tq, False)
        return carry

    lax.fori_loop(0, i, body, 0)
    half = tq // 2
    chunk(pl.multiple_of(i * tq, tq), half, 0, tq, True)
    chunk(pl.multiple_of(i * tq + half, half), half, half, half, True)

    out_t = jnp.concatenate(
        [acc_ref[hh * FOX_VT_ROWS:hh * FOX_VT_ROWS + FOX_HEAD_DIM, :]
         / acc_ref[hh * FOX_VT_ROWS + FOX_HEAD_DIM:hh * FOX_VT_ROWS + FOX_HEAD_DIM + 1, :]
         for hh in range(heads)], axis=0)
    o_ref[0] = out_t.astype(o_ref.dtype)


def _fox(fqa, fka, fvt, *, batch, seq):
    m = fqa.shape[0]
    tq = TQ_FOX
    steps = seq // tq
    heads = HEADS_FOX
    return pl.pallas_call(
        functools.partial(_fox_kernel, tq=tq),
        grid=(batch, FOX_HEADS // heads, steps),
        in_specs=[pl.BlockSpec((tq, heads * LANES), lambda b, p, i: (b * steps + i, p)),
                  pl.BlockSpec((seq, heads * LANES), lambda b, p, i: (b, p)),
                  pl.BlockSpec((1, heads * FOX_VT_ROWS, seq), lambda b, p, i: (b, p, 0))],
        out_specs=pl.BlockSpec((1, heads * FOX_HEAD_DIM, tq), lambda b, p, i: (b, p, i)),
        out_shape=jax.ShapeDtypeStruct((batch, FOX_WIDTH, seq), BF16),
        scratch_shapes=[pltpu.VMEM((heads, 1, tq), F32), pltpu.VMEM((heads * FOX_VT_ROWS, tq), F32)],
        compiler_params=pltpu.CompilerParams(
            dimension_semantics=("arbitrary", "arbitrary", "arbitrary"), vmem_limit_bytes=VMEM_LIMIT),
        name="fox",
    )(fqa, fka, fvt)


def _post_kernel(x_ref, og_ref, of_ref, p_ref, wo_c, wgu_c, wdown_c, wpg_c, wpp_c,
                 ffn_nw_ref, ple_nw_ref, fin_nw_ref, out_ref,
                 wo_ref, wgate_ref, wup_ref, wdown_ref, wpg_ref, wpp_ref, act_ref, *, w_steps, ts, tf):
    step = pl.program_id(0)
    d_ff = wgate_ref.shape[1]

    @pl.when(step < w_steps)
    def _():
        def put(dst_ref, chunk):
            rows = chunk.shape[0]
            dst_ref[pl.ds(pl.multiple_of(step * rows, rows), rows), :] = chunk.astype(BF16)

        put(wo_ref, wo_c[...])
        put(wgate_ref, wgu_c[:, 0:d_ff])
        put(wup_ref, wgu_c[:, d_ff:2 * d_ff])
        put(wdown_ref, wdown_c[...])
        put(wpg_ref, wpg_c[...])
        put(wpp_ref, wpp_c[...])

    @pl.when(step >= w_steps)
    def _():
        tm = x_ref.shape[0]
        groups = [slice(r, r + ts) for r in range(0, tm, ts)]
        hs = [x_ref[g, :] + _dot(og_ref[g, :], wo_ref[0:GDN_WIDTH, :])
              + _dot_tn(of_ref[0, :, g], wo_ref[GDN_WIDTH:GDN_WIDTH + FOX_WIDTH, :]) for g in groups]

        hns = [(h * _rms_scale(h) * ffn_nw_ref[...]).astype(BF16) for h in hs]
        for g, hn in zip(groups, hns):
            for c in range(d_ff // tf):
                cols = slice(c * tf, (c + 1) * tf)
                gate = _dot(hn, wgate_ref[:, cols])
                up = _dot(hn, wup_ref[:, cols])
                act_ref[g, cols] = (_silu(gate) * up).astype(BF16)
        hs = [h + _dot(act_ref[g, :], wdown_ref[...]) for g, h in zip(groups, hs)]

        hns = [(h * _rms_scale(h) * ple_nw_ref[...]).astype(BF16) for h in hs]
        ple_gates = [_sigmoid(_dot(hn, wpg_ref[...])) for hn in hns]
        hs = [h + pg * _dot(p_ref[g, :].astype(BF16), wpp_ref[...]) for g, h, pg in zip(groups, hs, ple_gates)]

        for g, h in zip(groups, hs):
            out_ref[g, :] = h * _rms_scale(h) * fin_nw_ref[...]


def _post(x2, o_g, o_f, p2, w_out, w_gate_up, w_down, w_ple_gate, w_ple_proj, ffn_nw, ple_nw, fin_nw):
    m, d = x2.shape
    tm = TM_POST
    w_steps = W_STEPS_POST
    d_ff = w_down.shape[0]
    tok = lambda w: pl.BlockSpec((tm, w), lambda s: (jnp.maximum(s - w_steps, 0), 0))
    wchunk = lambda w: pl.BlockSpec((w.shape[0] // w_steps, w.shape[1]), lambda s: (jnp.minimum(s, w_steps - 1), 0))
    weights = (w_out, w_gate_up, w_down, w_ple_gate, w_ple_proj)
    norms = (ffn_nw, ple_nw, fin_nw)
    tiles_per_seq = o_f.shape[2] // tm
    of_spec = pl.BlockSpec((1, FOX_WIDTH, tm), lambda s: (jnp.maximum(s - w_steps, 0) // tiles_per_seq, 0,
                                                           jnp.maximum(s - w_steps, 0) % tiles_per_seq))
    in_specs = ([tok(d), tok(GDN_WIDTH), of_spec, tok(p2.shape[1])] + [wchunk(w) for w in weights]
                + [_const_spec(w.shape) for w in norms])
    return pl.pallas_call(
        functools.partial(_post_kernel, w_steps=w_steps, ts=TS_POST, tf=TF_POST),
        grid=(w_steps + m // tm,), in_specs=in_specs, out_specs=tok(d),
        out_shape=jax.ShapeDtypeStruct((m, d), F32),
        scratch_shapes=[pltpu.VMEM(w_out.shape, BF16), pltpu.VMEM((d, d_ff), BF16), pltpu.VMEM((d, d_ff), BF16),
                        pltpu.VMEM(w_down.shape, BF16), pltpu.VMEM(w_ple_gate.shape, BF16),
                        pltpu.VMEM(w_ple_proj.shape, BF16), pltpu.VMEM((tm, d_ff), BF16)],
        compiler_params=pltpu.CompilerParams(dimension_semantics=("arbitrary",), vmem_limit_bytes=VMEM_LIMIT),
        name="post",
    )(x2, o_g, o_f, p2, *weights, *norms)


def _layer(x, p_i, attn_norm_w, w_in, conv_w, a_log, dt_bias, gdn_norm_w, fox_f_bias, w_out,
           ffn_norm_w, w_gate_up, w_down, ple_norm_w, w_ple_gate, w_ple_proj, out_norm_w):
    batch, seq, d = x.shape
    m = batch * seq
    assert seq % TM_IN == 0 and seq % PAIR == 0 and seq % TQ_FOX == 0 and seq % TM_POST == 0
    assert batch % BG_GDN == 0
    row = lambda w: w.reshape(1, -1).astype(F32)

    c0 = 3 * GDN_WIDTH
    c1 = c0 + GDN_WIDTH
    c2 = c1 + 2 * GDN_HEADS
    n_small = 2 * GDN_HEADS + FOX_HEADS
    zeros4 = jnp.zeros((GDN_HEADS,), F32)
    pad = jnp.zeros((LANES - n_small,), F32)
    prm = jnp.stack([
        jnp.concatenate([dt_bias.astype(F32), zeros4, fox_f_bias.astype(F32), pad]),
        jnp.concatenate([a_log.astype(F32), zeros4, jnp.zeros((FOX_HEADS,), F32), pad]),
    ])
    prm = jnp.pad(prm, ((0, SUBLANES - prm.shape[0]), (0, 0)))

    x2 = x.reshape(m, d)
    gq, gk, gv, gz, fqa, fka, fvt, gates, gates_t = _inproj(
        x2, row(attn_norm_w), w_in.astype(F32).T, (c0, c1, c2), conv_w.astype(F32), prm, batch=batch, seq=seq)
    o_g = _gdn(gq, gk, gv, gz, gates, gates_t, row(gdn_norm_w), batch=batch, seq=seq)
    o_f = _fox(fqa, fka, fvt, batch=batch, seq=seq)

    return _post(
        x2, o_g, o_f, p_i.reshape(m, -1),
        w_out.astype(F32), w_gate_up.astype(F32), w_down.astype(F32), w_ple_gate.astype(F32),
        w_ple_proj.astype(F32), row(ffn_norm_w), row(ple_norm_w), row(out_norm_w),
    ).reshape(batch, seq, d)


def kernel(x, p, attn_norm_w, w_in, conv_w, a_log, dt_bias, gdn_norm_w, fox_f_bias, w_out, ffn_norm_w,
           w_gate_up, w_down, ple_norm_w, w_ple_gate, w_ple_proj, final_norm_w):
    assert p.shape[0] == 1, "single-layer problem"
    return _layer(x, p[0], attn_norm_w[0], w_in[0], conv_w[0], a_log[0], dt_bias[0], gdn_norm_w[0],
                  fox_f_bias[0], w_out[0], ffn_norm_w[0], w_gate_up[0], w_down[0], ple_norm_w[0],
                  w_ple_gate[0], w_ple_proj[0], final_norm_w)
```

```python
import functools

import jax
import jax.numpy as jnp
import numpy as np
from jax import lax
from jax.experimental import pallas as pl
from jax.experimental.pallas import tpu as pltpu

F32 = jnp.float32
BF16 = jnp.bfloat16

EPS = 1e-6
GDN_HEADS = 4
GDN_HEAD_DIM = 128
GDN_WIDTH = GDN_HEADS * GDN_HEAD_DIM
FOX_HEADS = 8
FOX_HEAD_DIM = 64
FOX_WIDTH = FOX_HEADS * FOX_HEAD_DIM
CONV_WIDTH = 4
CHUNK = 64
PAIR = 2 * CHUNK
LANES = 128
SUBLANES = 8
BF16_ROWS = 16
FOX_VT_ROWS = FOX_HEAD_DIM + BF16_ROWS
NEG_BIG = -1e30
LOG2E = 1.4426950408889634

GC_LANE = 0
BETA_LANE = 4
CF_LANE = 8
GATE_ROWS = 16

VMEM_LIMIT = 56 * 1024 * 1024

TM_IN = 512
TS_IN = 512
BG_GDN = 4
NPAIR_GDN = 2
TQ_FOX = 512
HEADS_FOX = 8
TM_POST = 512
TS_POST = 256
TF_POST = 256
W_STEPS_POST = 8


def _dot(a, b):
    return jnp.dot(a, b, preferred_element_type=F32)


def _dot_nt(a, b):
    return lax.dot_general(a, b, (((1,), (1,)), ((), ())), preferred_element_type=F32)


def _dot_tn(a, b):
    return lax.dot_general(a, b, (((0,), (0,)), ((), ())), preferred_element_type=F32)


def _dot_inv(a, b):
    return _dot(a.astype(BF16), b.astype(BF16))


def _sigmoid(x):
    return 0.5 * jnp.tanh(0.5 * x) + 0.5


def _silu(x):
    h = 0.5 * x
    return h * jnp.tanh(h) + h


def _rms_scale(x):
    return lax.rsqrt(jnp.mean(x * x, axis=-1, keepdims=True) + EPS)


def _const_spec(shape):
    nd = len(shape)
    return pl.BlockSpec(shape, lambda *_: (0,) * nd, pipeline_mode=pl.Buffered(1))


def _inproj_kernel(x_ref, nw_ref, wt_ref, conv_ref, prm_ref, sel_ref,
                   gq_ref, gk_ref, gv_ref, gz_ref, fqa_ref, fka_ref, fvt_ref, gates_ref, gates_t_ref,
                   buf_ref, carry_ref, wg_ref, wz_ref, wf_ref, wfvt_ref, ws_ref, *, splits, **tile_params):
    step = pl.program_id(0)

    @pl.when(step == 0)
    def _():
        c0, c1, c2 = splits
        c3 = c2 + wf_ref.shape[0]
        c4 = c3 + wfvt_ref.shape[0]
        wg_ref[...] = wt_ref[0:c0, :].astype(BF16)
        wz_ref[...] = wt_ref[c0:c1, :].astype(BF16)
        wf_ref[...] = wt_ref[c2:c3, :].astype(BF16)
        wfvt_ref[...] = wt_ref[c3:c4, :].astype(BF16)
        small = jnp.concatenate([wt_ref[c1:c2, :], wt_ref[c4:wt_ref.shape[0], :]], axis=0)
        ws_ref[...] = jnp.zeros(ws_ref.shape, BF16)
        ws_ref[0:small.shape[0], :] = small.astype(BF16)

    @pl.when(step > 0)
    def _():
        _inproj_tile(step - 1, x_ref, nw_ref, wg_ref, wz_ref, wf_ref, wfvt_ref, ws_ref, conv_ref, prm_ref,
                     sel_ref, gq_ref, gk_ref, gv_ref, gz_ref, fqa_ref, fka_ref, fvt_ref, gates_ref, gates_t_ref,
                     buf_ref, carry_ref, **tile_params)


def _inproj_tile(i, x_ref, nw_ref, wg_ref, wz_ref, wf_ref, wfvt_ref, ws_ref, conv_ref, prm_ref, sel_ref,
                 gq_ref, gk_ref, gv_ref, gz_ref, fqa_ref, fka_ref, fvt_ref, gates_ref, gates_t_ref,
                 buf_ref, carry_ref, *, tm, ts, tiles_per_seq):
    seq_start = (i % tiles_per_seq) == 0
    halo = SUBLANES

    @pl.when(seq_start)
    def _():
        buf_ref[0:halo, :] = jnp.zeros((halo, buf_ref.shape[1]), F32)
        carry_ref[...] = jnp.zeros_like(carry_ref)

    @pl.when(jnp.logical_not(seq_start))
    def _():
        buf_ref[0:halo, :] = buf_ref[tm:tm + halo, :]

    lane = lax.broadcasted_iota(jnp.int32, (ts, LANES), 1)
    row = lax.broadcasted_iota(jnp.int32, (ts, LANES), 0)
    is_gc = lane < BETA_LANE
    is_cf = (lane >= CF_LANE) & (lane < CF_LANE + FOX_HEADS)
    pos = jnp.where(is_gc, row % CHUNK, row)
    summed = is_gc | is_cf
    out_refs = (gq_ref, gk_ref, gv_ref)

    for r0 in range(0, tm, ts):
        rows = slice(r0, r0 + ts)
        x = x_ref[rows, :]
        xn = (x * _rms_scale(x) * nw_ref[...]).astype(BF16)

        buf_ref[halo + r0:halo + r0 + ts, :] = _dot_nt(xn, wg_ref[...])
        for j in range(3 * GDN_HEADS):
            cols = slice(j * LANES, (j + 1) * LANES)
            acc = buf_ref[halo + r0:halo + r0 + ts, cols] * conv_ref[CONV_WIDTH - 1:CONV_WIDTH, cols]
            for t in range(CONV_WIDTH - 1):
                off = halo + r0 - (CONV_WIDTH - 1) + t
                acc = acc + buf_ref[off:off + ts, cols] * conv_ref[t:t + 1, cols]
            y = _silu(acc)
            kind, head = divmod(j, GDN_HEADS)
            if kind < 2:
                y = y * lax.rsqrt(jnp.sum(y * y, axis=-1, keepdims=True) + EPS)
            if kind == 0:
                y = y * (GDN_HEAD_DIM ** -0.5)
            out_refs[kind][rows, head * LANES:(head + 1) * LANES] = y

        gz_ref[rows, :] = _dot_nt(xn, wz_ref[...])

        vt = _dot_nt(wfvt_ref[...], xn).astype(BF16)
        for h in range(FOX_HEADS):
            v0 = h * FOX_VT_ROWS
            fvt_ref[0, v0:v0 + FOX_HEAD_DIM, rows] = vt[h * FOX_HEAD_DIM:(h + 1) * FOX_HEAD_DIM]
            fvt_ref[0, v0 + FOX_HEAD_DIM:v0 + FOX_VT_ROWS, rows] = jnp.ones((FOX_VT_ROWS - FOX_HEAD_DIM, ts), BF16)

        z = _dot_nt(xn, ws_ref[...]) + prm_ref[0:1, :]
        soft = jnp.log1p(jnp.exp(-jnp.abs(z)))
        g_decay = -jnp.exp(prm_ref[1:2, :]) * (jnp.maximum(z, 0.0) + soft)
        log_f = -(jnp.maximum(-z, 0.0) + soft)
        val = jnp.where(is_gc, g_decay, jnp.where(lane < CF_LANE, _sigmoid(z), jnp.where(is_cf, log_f, 0.0)))

        d = 1
        while d < ts:
            shifted = pltpu.roll(val, d, axis=0)
            val = val + jnp.where(summed & (pos >= d), shifted, 0.0)
            d *= 2
        val = val + jnp.where(is_cf, carry_ref[...], 0.0)
        carry_ref[...] = val[ts - 1:ts, :]

        gates_ref[rows, :] = val
        gates_t_ref[0, :, rows] = val.T[0:GATE_ROWS, :]

        f_qk = _dot_nt(xn, wf_ref[...])
        c2 = jnp.where(is_cf, val * LOG2E, 0.0)
        c_hi = c2.astype(BF16).astype(F32)
        c_mid = (c2 - c_hi).astype(BF16).astype(F32)
        c_lo = c2 - c_hi - c_mid
        parts = (jnp.where(lane == 0, 1.0, c_hi) + pltpu.roll(c_mid, FOX_HEADS, axis=1)
                 + pltpu.roll(c_lo, 2 * FOX_HEADS, axis=1))
        bias = _dot(parts.astype(BF16), sel_ref[...])
        for h in range(FOX_HEADS):
            src, half = divmod(h, 2)
            in_half = (lane >= half * FOX_HEAD_DIM) & (lane < (half + 1) * FOX_HEAD_DIM)
            q_slab = f_qk[:, src * LANES:(src + 1) * LANES] * (FOX_HEAD_DIM ** -0.5 * LOG2E)
            k_slab = f_qk[:, FOX_WIDTH + src * LANES:FOX_WIDTH + (src + 1) * LANES]
            q_bias = bias[:, h * LANES:(h + 1) * LANES]
            k_bias = bias[:, (FOX_HEADS + h) * LANES:(FOX_HEADS + h + 1) * LANES]
            fqa_ref[rows, h * LANES:(h + 1) * LANES] = jnp.where(in_half, q_slab, q_bias).astype(BF16)
            fka_ref[rows, h * LANES:(h + 1) * LANES] = jnp.where(in_half, k_slab, k_bias).astype(BF16)


def _bias_selector():
    sel = np.zeros((LANES, 2 * FOX_HEADS * LANES), np.float32)
    for h in range(FOX_HEADS):
        base = (1 - h % 2) * FOX_HEAD_DIM
        q0 = h * LANES + base
        k0 = (FOX_HEADS + h) * LANES + base
        for part in range(3):
            src = CF_LANE + part * FOX_HEADS + h
            sel[src, q0 + part] = 1.0
            sel[0, q0 + 3 + part] = -1.0
            sel[0, k0 + part] = 1.0
            sel[src, k0 + 3 + part] = 1.0
    return jnp.asarray(sel, BF16)


def _inproj(x2, nw, w_t, splits, conv_w, prm, *, batch, seq):
    sel = _bias_selector()
    m, d = x2.shape
    tm = TM_IN
    tiles_per_seq = seq // tm
    grid = (1 + m // tm,)
    tile = lambda s: jnp.maximum(s - 1, 0)
    tok = lambda w: pl.BlockSpec((tm, w), lambda s: (tile(s), 0))
    seq_t = lambda rows: pl.BlockSpec(
        (1, rows, tm), lambda s: (tile(s) // tiles_per_seq, 0, tile(s) % tiles_per_seq))
    c0, c1, c2 = splits
    out_shape = (
        jax.ShapeDtypeStruct((m, GDN_WIDTH), F32),
        jax.ShapeDtypeStruct((m, GDN_WIDTH), F32),
        jax.ShapeDtypeStruct((m, GDN_WIDTH), F32),
        jax.ShapeDtypeStruct((m, GDN_WIDTH), F32),
        jax.ShapeDtypeStruct((m, FOX_HEADS * LANES), BF16),
        jax.ShapeDtypeStruct((m, FOX_HEADS * LANES), BF16),
        jax.ShapeDtypeStruct((batch, FOX_HEADS * FOX_VT_ROWS, seq), BF16),
        jax.ShapeDtypeStruct((m, LANES), F32),
        jax.ShapeDtypeStruct((batch, GATE_ROWS, seq), F32),
    )
    out_specs = (
        tok(GDN_WIDTH), tok(GDN_WIDTH), tok(GDN_WIDTH), tok(GDN_WIDTH),
        tok(FOX_HEADS * LANES), tok(FOX_HEADS * LANES), seq_t(FOX_HEADS * FOX_VT_ROWS), tok(LANES),
        seq_t(GATE_ROWS),
    )
    in_specs = [
        tok(d), _const_spec(nw.shape), _const_spec(w_t.shape), _const_spec(conv_w.shape),
        _const_spec(prm.shape), _const_spec(sel.shape),
    ]
    return pl.pallas_call(
        functools.partial(_inproj_kernel, splits=splits, tm=tm, ts=TS_IN, tiles_per_seq=tiles_per_seq),
        grid=grid, in_specs=in_specs, out_specs=out_specs, out_shape=out_shape,
        scratch_shapes=[pltpu.VMEM((tm + SUBLANES, 3 * GDN_WIDTH), F32), pltpu.VMEM((1, LANES), F32),
                        pltpu.VMEM((c0, d), BF16), pltpu.VMEM((c1 - c0, d), BF16),
                        pltpu.VMEM((2 * FOX_WIDTH, d), BF16), pltpu.VMEM((FOX_WIDTH, d), BF16),
                        pltpu.VMEM((LANES, d), BF16)],
        compiler_params=pltpu.CompilerParams(dimension_semantics=("arbitrary",), vmem_limit_bytes=VMEM_LIMIT),
        name="inproj",
    )(x2, nw, w_t, conv_w, prm, sel)


def _gdn_kernel(q_ref, k_ref, v_ref, z_ref, gates_ref, gates_t_ref, nw_ref, o_ref, state_ref, *, bg, npair):
    @pl.when(pl.program_id(1) == 0)
    def _():
        state_ref[...] = jnp.zeros_like(state_ref)

    ri = lax.broadcasted_iota(jnp.int32, (PAIR, PAIR), 0)
    ci = lax.broadcasted_iota(jnp.int32, (PAIR, PAIR), 1)
    same_chunk = (ri // CHUNK) == (ci // CHUNK)
    lower = same_chunk & (ri >= ci)
    strict = same_chunk & (ri > ci)
    eye = (ri == ci).astype(F32)
    first_chunk_row = lax.broadcasted_iota(jnp.int32, (PAIR, LANES), 0) < CHUNK
    nw = nw_ref[...]

    seqs = [(b, h) for b in range(bg) for h in range(GDN_HEADS)]
    chains = [(b, pr, h) for pr in range(npair) for b, h in seqs]
    rows_of = lambda pr: slice(pr * PAIR, (pr + 1) * PAIR)
    cols_of = lambda h: slice(h * GDN_HEAD_DIM, (h + 1) * GDN_HEAD_DIM)
    col_of = lambda gt, lane: gt[:, lane:lane + 1]

    gts = {(b, pr): gates_ref[b, rows_of(pr), :] for b in range(bg) for pr in range(npair)}
    gt_ts = {(b, pr): gates_t_ref[b, :, rows_of(pr)] for b in range(bg) for pr in range(npair)}
    k16s, a_mats, decays, e_gcs = [], [], [], []
    for b, pr, h in chains:
        k = k_ref[b, rows_of(pr), cols_of(h)]
        gc = col_of(gts[b, pr], GC_LANE + h)
        gc_row = gt_ts[b, pr][GC_LANE + h:GC_LANE + h + 1, :]
        decay = jnp.where(lower, jnp.exp(jnp.where(lower, gc - gc_row, 0.0)), 0.0)
        k16 = k.astype(BF16)
        a_mats.append(
            jnp.where(strict, _dot_nt((k * col_of(gts[b, pr], BETA_LANE + h)).astype(BF16), k16) * decay, 0.0))
        k16s.append(k16)
        decays.append(decay)
        e_gcs.append(jnp.exp(gc))

    invs = [eye - a for a in a_mats]
    pows = a_mats
    for _ in range(5):
        pows = [_dot_inv(a, a) for a in pows]
        invs = [_dot_inv(t, eye + a) for t, a in zip(invs, pows)]

    us, wqs, qks, kdts, decs = {}, {}, {}, {}, {}
    for n, (b, pr, h) in enumerate(chains):
        rows = rows_of(pr)
        gc = col_of(gts[b, pr], GC_LANE + h)
        beta = col_of(gts[b, pr], BETA_LANE + h)
        q = q_ref[b, rows, cols_of(h)]
        k = k_ref[b, rows, cols_of(h)]
        inv16 = invs[n].astype(BF16)
        us[b, pr, h] = _dot(inv16, (v_ref[b, rows, cols_of(h)] * beta).astype(BF16))
        w = _dot(inv16, ((k * beta) * e_gcs[n]).astype(BF16)).astype(BF16)
        qe = (q * e_gcs[n]).astype(BF16)
        wqs[b, pr, h] = [jnp.concatenate([w[c * CHUNK:(c + 1) * CHUNK], qe[c * CHUNK:(c + 1) * CHUNK]], axis=0)
                         for c in range(PAIR // CHUNK)]
        qks[b, pr, h] = jnp.where(lower, _dot_nt(q.astype(BF16), k16s[n]) * decays[n], 0.0).astype(BF16)
        g_last = jnp.where(first_chunk_row, gts[b, pr][CHUNK - 1:CHUNK, :], gts[b, pr][PAIR - 1:PAIR, :])
        g_last = col_of(g_last, GC_LANE + h)
        kdts[b, pr, h] = (k * jnp.exp(g_last - gc)).T.astype(BF16)
        decs[b, pr, h] = jnp.exp(g_last)

    states = [state_ref[b * GDN_HEADS + h] for b, h in seqs]
    zeros = jnp.zeros((CHUNK, GDN_HEAD_DIM), BF16)
    for pr in range(npair):
        for c in range(PAIR // CHUNK):
            rows = slice(c * CHUNK, (c + 1) * CHUNK)
            out_rows = slice(pr * PAIR + c * CHUNK, pr * PAIR + (c + 1) * CHUNK)
            rs = [_dot(wqs[b, pr, h][c], states[n].astype(BF16)) for n, (b, h) in enumerate(seqs)]
            v_news = [(us[b, pr, h][rows] - rs[n][0:CHUNK]).astype(BF16) for n, (b, h) in enumerate(seqs)]
            rhs = [jnp.concatenate([v, zeros] if c == 0 else [zeros, v], axis=0) for v in v_news]
            upds = [_dot(kdts[b, pr, h], rhs[n]) for n, (b, h) in enumerate(seqs)]
            states = [states[n] * decs[b, pr, h][(c + 1) * CHUNK - 1:(c + 1) * CHUNK, :] + upds[n]
                      for n, (b, h) in enumerate(seqs)]
            for n, (b, h) in enumerate(seqs):
                o = rs[n][CHUNK:2 * CHUNK] + _dot(qks[b, pr, h][rows], rhs[n])
                o = o * lax.rsqrt(jnp.mean(o * o, axis=-1, keepdims=True) + EPS) * nw
                o_ref[b, out_rows, cols_of(h)] = (o * _silu(z_ref[b, out_rows, cols_of(h)])).astype(o_ref.dtype)
    for n, (b, h) in enumerate(seqs):
        state_ref[b * GDN_HEADS + h] = states[n]


def _gdn(gq, gk, gv, gz, gates, gates_t, nw, *, batch, seq):
    m = gq.shape[0]
    bg = BG_GDN
    npair = NPAIR_GDN
    tokens = npair * PAIR
    seq3 = lambda a: a.reshape(batch, seq, a.shape[-1])
    blk = lambda w: pl.BlockSpec((bg, tokens, w), lambda bi, j: (bi, j, 0))
    o = pl.pallas_call(
        functools.partial(_gdn_kernel, bg=bg, npair=npair),
        grid=(batch // bg, seq // tokens),
        in_specs=[blk(GDN_WIDTH), blk(GDN_WIDTH), blk(GDN_WIDTH), blk(GDN_WIDTH), blk(LANES),
                  pl.BlockSpec((bg, GATE_ROWS, tokens), lambda bi, j: (bi, 0, j)), _const_spec(nw.shape)],
        out_specs=blk(GDN_WIDTH),
        out_shape=jax.ShapeDtypeStruct((batch, seq, GDN_WIDTH), BF16),
        scratch_shapes=[pltpu.VMEM((bg * GDN_HEADS, GDN_HEAD_DIM, GDN_HEAD_DIM), F32)],
        compiler_params=pltpu.CompilerParams(
            dimension_semantics=("arbitrary", "arbitrary"), vmem_limit_bytes=VMEM_LIMIT),
        name="gdn",
    )(seq3(gq), seq3(gk), seq3(gv), seq3(gz), seq3(gates), gates_t, nw)
    return o.reshape(m, GDN_WIDTH)


def _fox_kernel(qa_ref, ka_ref, vt_ref, o_ref, m_ref, acc_ref, *, tq):
    i = pl.program_id(2)
    heads = qa_ref.shape[1] // LANES

    def chunk(start, size, q0, nq, masked, first=False):
        ka = ka_ref[pl.ds(start, size), :]
        sts = [_dot_nt(ka[:, hh * LANES:(hh + 1) * LANES], qa_ref[q0:q0 + nq, hh * LANES:(hh + 1) * LANES])
               for hh in range(heads)]
        if masked:
            ri = lax.broadcasted_iota(jnp.int32, (size, nq), 0)
            ci = lax.broadcasted_iota(jnp.int32, (size, nq), 1)
            keep = ri <= ci
            sts = [jnp.where(keep, s, NEG_BIG) for s in sts]
        qs = slice(q0, q0 + nq)
        col_max = [jnp.max(s, axis=0, keepdims=True) for s in sts]
        if first:
            m_news = col_max
        else:
            m_olds = [m_ref[hh][:, qs] for hh in range(heads)]
            m_news = [jnp.maximum(m, c) for m, c in zip(m_olds, col_max)]
        ps = [jnp.exp2(s - m).astype(BF16) for s, m in zip(sts, m_news)]
        pvs = [_dot(vt_ref[0, hh * FOX_VT_ROWS:(hh + 1) * FOX_VT_ROWS, pl.ds(start, size)], ps[hh])
               for hh in range(heads)]
        for hh in range(heads):
            rows = slice(hh * FOX_VT_ROWS, (hh + 1) * FOX_VT_ROWS)
            m_ref[hh, :, qs] = m_news[hh]
            if first:
                acc_ref[rows, qs] = pvs[hh]
            else:
                acc_ref[rows, qs] = jnp.exp2(m_olds[hh] - m_news[hh]) * acc_ref[rows, qs] + pvs[hh]

    half = tq // 2
    chunk(pl.multiple_of(i * tq, tq), half, 0, tq, True, first=True)
    chunk(pl.multiple_of(i * tq + half, half), half, half, half, True)

    def body(j, carry):
        chunk(pl.multiple_of(j * tq, tq), tq, 0, tq, False)
        return carry

    lax.fori_loop(0, i, body, 0)

    out_t = jnp.concatenate(
        [acc_ref[hh * FOX_VT_ROWS:hh * FOX_VT_ROWS + FOX_HEAD_DIM, :]
         / acc_ref[hh * FOX_VT_ROWS + FOX_HEAD_DIM:hh * FOX_VT_ROWS + FOX_HEAD_DIM + 1, :]
         for hh in range(heads)], axis=0)
    o_ref[0] = out_t.astype(o_ref.dtype)


def _fox(fqa, fka, fvt, *, batch, seq):
    m = fqa.shape[0]
    tq = TQ_FOX
    steps = seq // tq
    heads = HEADS_FOX
    return pl.pallas_call(
        functools.partial(_fox_kernel, tq=tq),
        grid=(batch, FOX_HEADS // heads, steps),
        in_specs=[pl.BlockSpec((tq, heads * LANES), lambda b, p, i: (b * steps + i, p)),
                  pl.BlockSpec((seq, heads * LANES), lambda b, p, i: (b, p)),
                  pl.BlockSpec((1, heads * FOX_VT_ROWS, seq), lambda b, p, i: (b, p, 0))],
        out_specs=pl.BlockSpec((1, heads * FOX_HEAD_DIM, tq), lambda b, p, i: (b, p, i)),
        out_shape=jax.ShapeDtypeStruct((batch, FOX_WIDTH, seq), BF16),
        scratch_shapes=[pltpu.VMEM((heads, 1, tq), F32), pltpu.VMEM((heads * FOX_VT_ROWS, tq), F32)],
        compiler_params=pltpu.CompilerParams(
            dimension_semantics=("arbitrary", "arbitrary", "arbitrary"), vmem_limit_bytes=VMEM_LIMIT),
        name="fox",
    )(fqa, fka, fvt)


def _post_kernel(x_ref, og_ref, of_ref, p_ref, wo_c, wgu_c, wdown_c, wpg_c, wpp_c,
                 ffn_nw_ref, ple_nw_ref, fin_nw_ref, out_ref,
                 wo_ref, wgate_ref, wup_ref, wdown_ref, wpg_ref, wpp_ref, act_ref, *, w_steps, ts, tf):
    step = pl.program_id(0)
    d_ff = wgate_ref.shape[1]

    @pl.when(step < w_steps)
    def _():
        def put(dst_ref, chunk):
            rows = chunk.shape[0]
            dst_ref[pl.ds(pl.multiple_of(step * rows, rows), rows), :] = chunk.astype(BF16)

        put(wo_ref, wo_c[...])
        put(wgate_ref, wgu_c[:, 0:d_ff])
        put(wup_ref, wgu_c[:, d_ff:2 * d_ff])
        put(wdown_ref, wdown_c[...])
        put(wpg_ref, wpg_c[...])
        put(wpp_ref, wpp_c[...])

    @pl.when(step >= w_steps)
    def _():
        tm = x_ref.shape[0]
        groups = [slice(r, r + ts) for r in range(0, tm, ts)]
        hs = [x_ref[g, :] + _dot(og_ref[g, :], wo_ref[0:GDN_WIDTH, :])
              + _dot_tn(of_ref[0, :, g], wo_ref[GDN_WIDTH:GDN_WIDTH + FOX_WIDTH, :]) for g in groups]

        hns = [(h * _rms_scale(h) * ffn_nw_ref[...]).astype(BF16) for h in hs]
        for g, hn in zip(groups, hns):
            for c in range(d_ff // tf):
                cols = slice(c * tf, (c + 1) * tf)
                gate = _dot(hn, wgate_ref[:, cols])
                up = _dot(hn, wup_ref[:, cols])
                act_ref[g, cols] = (_silu(gate) * up).astype(BF16)
        hs = [h + _dot(act_ref[g, :], wdown_ref[...]) for g, h in zip(groups, hs)]

        hns = [(h * _rms_scale(h) * ple_nw_ref[...]).astype(BF16) for h in hs]
        ple_gates = [_sigmoid(_dot(hn, wpg_ref[...])) for hn in hns]
        hs = [h + pg * _dot(p_ref[g, :].astype(BF16), wpp_ref[...]) for g, h, pg in zip(groups, hs, ple_gates)]

        for g, h in zip(groups, hs):
            out_ref[g, :] = h * _rms_scale(h) * fin_nw_ref[...]


def _post(x2, o_g, o_f, p2, w_out, w_gate_up, w_down, w_ple_gate, w_ple_proj, ffn_nw, ple_nw, fin_nw):
    m, d = x2.shape
    tm = TM_POST
    w_steps = W_STEPS_POST
    d_ff = w_down.shape[0]
    tok = lambda w: pl.BlockSpec((tm, w), lambda s: (jnp.maximum(s - w_steps, 0), 0))
    wchunk = lambda w: pl.BlockSpec((w.shape[0] // w_steps, w.shape[1]), lambda s: (jnp.minimum(s, w_steps - 1), 0))
    weights = (w_out, w_gate_up, w_down, w_ple_gate, w_ple_proj)
    norms = (ffn_nw, ple_nw, fin_nw)
    tiles_per_seq = o_f.shape[2] // tm
    of_spec = pl.BlockSpec((1, FOX_WIDTH, tm), lambda s: (jnp.maximum(s - w_steps, 0) // tiles_per_seq, 0,
                                                           jnp.maximum(s - w_steps, 0) % tiles_per_seq))
    in_specs = ([tok(d), tok(GDN_WIDTH), of_spec, tok(p2.shape[1])] + [wchunk(w) for w in weights]
                + [_const_spec(w.shape) for w in norms])
    return pl.pallas_call(
        functools.partial(_post_kernel, w_steps=w_steps, ts=TS_POST, tf=TF_POST),
        grid=(w_steps + m // tm,), in_specs=in_specs, out_specs=tok(d),
        out_shape=jax.ShapeDtypeStruct((m, d), F32),
        scratch_shapes=[pltpu.VMEM(w_out.shape, BF16), pltpu.VMEM((d, d_ff), BF16), pltpu.VMEM((d, d_ff), BF16),
                        pltpu.VMEM(w_down.shape, BF16), pltpu.VMEM(w_ple_gate.shape, BF16),
                        pltpu.VMEM(w_ple_proj.shape, BF16), pltpu.VMEM((tm, d_ff), BF16)],
        compiler_params=pltpu.CompilerParams(dimension_semantics=("arbitrary",), vmem_limit_bytes=VMEM_LIMIT),
        name="post",
    )(x2, o_g, o_f, p2, *weights, *norms)


def _layer(x, p_i, attn_norm_w, w_in, conv_w, a_log, dt_bias, gdn_norm_w, fox_f_bias, w_out,
           ffn_norm_w, w_gate_up, w_down, ple_norm_w, w_ple_gate, w_ple_proj, out_norm_w):
    batch, seq, d = x.shape
    m = batch * seq
    assert seq % TM_IN == 0 and seq % (NPAIR_GDN * PAIR) == 0 and seq % TQ_FOX == 0 and seq % TM_POST == 0
    assert batch % BG_GDN == 0
    row = lambda w: w.reshape(1, -1).astype(F32)

    c0 = 3 * GDN_WIDTH
    c1 = c0 + GDN_WIDTH
    c2 = c1 + 2 * GDN_HEADS
    n_small = 2 * GDN_HEADS + FOX_HEADS
    zeros4 = jnp.zeros((GDN_HEADS,), F32)
    pad = jnp.zeros((LANES - n_small,), F32)
    prm = jnp.stack([
        jnp.concatenate([dt_bias.astype(F32), zeros4, fox_f_bias.astype(F32), pad]),
        jnp.concatenate([a_log.astype(F32), zeros4, jnp.zeros((FOX_HEADS,), F32), pad]),
    ])
    prm = jnp.pad(prm, ((0, SUBLANES - prm.shape[0]), (0, 0)))

    x2 = x.reshape(m, d)
    gq, gk, gv, gz, fqa, fka, fvt, gates, gates_t = _inproj(
        x2, row(attn_norm_w), w_in.astype(F32).T, (c0, c1, c2), conv_w.astype(F32), prm, batch=batch, seq=seq)
    o_g = _gdn(gq, gk, gv, gz, gates, gates_t, row(gdn_norm_w), batch=batch, seq=seq)
    o_f = _fox(fqa, fka, fvt, batch=batch, seq=seq)

    return _post(
        x2, o_g, o_f, p_i.reshape(m, -1),
        w_out.astype(F32), w_gate_up.astype(F32), w_down.astype(F32), w_ple_gate.astype(F32),
        w_ple_proj.astype(F32), row(ffn_norm_w), row(ple_norm_w), row(out_norm_w),
    ).reshape(batch, seq, d)


def kernel(x, p, attn_norm_w, w_in, conv_w, a_log, dt_bias, gdn_norm_w, fox_f_bias, w_out, ffn_norm_w,
           w_gate_up, w_down, ple_norm_w, w_ple_gate, w_ple_proj, final_norm_w):
    assert p.shape[0] == 1, "single-layer problem"
    return _layer(x, p[0], attn_norm_w[0], w_in[0], conv_w[0], a_log[0], dt_bias[0], gdn_norm_w[0],
                  fox_f_bias[0], w_out[0], ffn_norm_w[0], w_gate_up[0], w_down[0], ple_norm_w[0],
                  w_ple_gate[0], w_ple_proj[0], final_norm_w)
```

```python
import functools

import jax
import jax.numpy as jnp
import numpy as np
from jax import lax
from jax.experimental import pallas as pl
from jax.experimental.pallas import tpu as pltpu

F32 = jnp.float32
BF16 = jnp.bfloat16

EPS = 1e-6
GDN_HEADS = 4
GDN_HEAD_DIM = 128
GDN_WIDTH = GDN_HEADS * GDN_HEAD_DIM
FOX_HEADS = 8
FOX_HEAD_DIM = 64
FOX_WIDTH = FOX_HEADS * FOX_HEAD_DIM
CONV_WIDTH = 4
CHUNK = 64
PAIR = 2 * CHUNK
LANES = 128
SUBLANES = 8
BF16_ROWS = 16
FOX_VT_ROWS = FOX_HEAD_DIM + BF16_ROWS
NEG_BIG = -1e30
LOG2E = 1.4426950408889634

GC_LANE = 0
BETA_LANE = 4
CF_LANE = 8
GATE_ROWS = 16

VMEM_LIMIT = 56 * 1024 * 1024

TM_IN = 512
TS_IN = 512
BG_GDN = 4
NPAIR_GDN = 2
TQ_FOX = 512
HEADS_FOX = 8
TM_POST = 512
TS_POST = 256
TF_POST = 256


def _dot(a, b):
    return jnp.dot(a, b, preferred_element_type=F32)


def _dot_nt(a, b):
    return lax.dot_general(a, b, (((1,), (1,)), ((), ())), preferred_element_type=F32)


def _dot_tn(a, b):
    return lax.dot_general(a, b, (((0,), (0,)), ((), ())), preferred_element_type=F32)


def _dot_inv(a, b):
    return _dot(a.astype(BF16), b.astype(BF16))


def _sigmoid(x):
    return 0.5 * jnp.tanh(0.5 * x) + 0.5


def _silu(x):
    h = 0.5 * x
    return h * jnp.tanh(h) + h


def _rms_scale(x):
    return lax.rsqrt(jnp.mean(x * x, axis=-1, keepdims=True) + EPS)


def _const_spec(shape):
    nd = len(shape)
    return pl.BlockSpec(shape, lambda *_: (0,) * nd, pipeline_mode=pl.Buffered(1))


def _inproj_kernel(x_ref, nw_ref, wt_ref, conv_ref, prm_ref, sel_ref,
                   gq_ref, gk_ref, gv_ref, gz_ref, fqa_ref, fka_ref, fvt_ref, gates_ref, gates_t_ref,
                   buf_ref, carry_ref, wg_ref, wz_ref, wf_ref, wfvt_ref, ws_ref, *, splits, **tile_params):
    step = pl.program_id(0)

    @pl.when(step == 0)
    def _():
        c0, c1, c2 = splits
        c3 = c2 + wf_ref.shape[0]
        c4 = c3 + wfvt_ref.shape[0]
        wg_ref[...] = wt_ref[0:c0, :].astype(BF16)
        wz_ref[...] = wt_ref[c0:c1, :].astype(BF16)
        wf_ref[...] = wt_ref[c2:c3, :].astype(BF16)
        wfvt_ref[...] = wt_ref[c3:c4, :].astype(BF16)
        small = jnp.concatenate([wt_ref[c1:c2, :], wt_ref[c4:wt_ref.shape[0], :]], axis=0)
        ws_ref[...] = jnp.zeros(ws_ref.shape, BF16)
        ws_ref[0:small.shape[0], :] = small.astype(BF16)

    @pl.when(step > 0)
    def _():
        _inproj_tile(step - 1, x_ref, nw_ref, wg_ref, wz_ref, wf_ref, wfvt_ref, ws_ref, conv_ref, prm_ref,
                     sel_ref, gq_ref, gk_ref, gv_ref, gz_ref, fqa_ref, fka_ref, fvt_ref, gates_ref, gates_t_ref,
                     buf_ref, carry_ref, **tile_params)


def _inproj_tile(i, x_ref, nw_ref, wg_ref, wz_ref, wf_ref, wfvt_ref, ws_ref, conv_ref, prm_ref, sel_ref,
                 gq_ref, gk_ref, gv_ref, gz_ref, fqa_ref, fka_ref, fvt_ref, gates_ref, gates_t_ref,
                 buf_ref, carry_ref, *, tm, ts, tiles_per_seq):
    seq_start = (i % tiles_per_seq) == 0
    halo = SUBLANES

    @pl.when(seq_start)
    def _():
        buf_ref[0:halo, :] = jnp.zeros((halo, buf_ref.shape[1]), F32)
        carry_ref[...] = jnp.zeros_like(carry_ref)

    @pl.when(jnp.logical_not(seq_start))
    def _():
        buf_ref[0:halo, :] = buf_ref[tm:tm + halo, :]

    lane = lax.broadcasted_iota(jnp.int32, (ts, LANES), 1)
    row = lax.broadcasted_iota(jnp.int32, (ts, LANES), 0)
    is_gc = lane < BETA_LANE
    is_cf = (lane >= CF_LANE) & (lane < CF_LANE + FOX_HEADS)
    pos = jnp.where(is_gc, row % CHUNK, row)
    summed = is_gc | is_cf
    out_refs = (gq_ref, gk_ref, gv_ref)

    for r0 in range(0, tm, ts):
        rows = slice(r0, r0 + ts)
        x = x_ref[rows, :]
        xn = (x * _rms_scale(x) * nw_ref[...]).astype(BF16)

        buf_ref[halo + r0:halo + r0 + ts, :] = _dot_nt(xn, wg_ref[...])
        for j in range(3 * GDN_HEADS):
            cols = slice(j * LANES, (j + 1) * LANES)
            acc = buf_ref[halo + r0:halo + r0 + ts, cols] * conv_ref[CONV_WIDTH - 1:CONV_WIDTH, cols]
            for t in range(CONV_WIDTH - 1):
                off = halo + r0 - (CONV_WIDTH - 1) + t
                acc = acc + buf_ref[off:off + ts, cols] * conv_ref[t:t + 1, cols]
            y = _silu(acc)
            kind, head = divmod(j, GDN_HEADS)
            if kind < 2:
                y = y * lax.rsqrt(jnp.sum(y * y, axis=-1, keepdims=True) + EPS)
            if kind == 0:
                y = y * (GDN_HEAD_DIM ** -0.5)
            out_refs[kind][rows, head * LANES:(head + 1) * LANES] = y

        gz_ref[rows, :] = _dot_nt(xn, wz_ref[...])

        vt = _dot_nt(wfvt_ref[...], xn).astype(BF16)
        for h in range(FOX_HEADS):
            v0 = h * FOX_VT_ROWS
            fvt_ref[0, v0:v0 + FOX_HEAD_DIM, rows] = vt[h * FOX_HEAD_DIM:(h + 1) * FOX_HEAD_DIM]
            fvt_ref[0, v0 + FOX_HEAD_DIM:v0 + FOX_VT_ROWS, rows] = jnp.ones((FOX_VT_ROWS - FOX_HEAD_DIM, ts), BF16)

        z = _dot_nt(xn, ws_ref[...]) + prm_ref[0:1, :]
        soft = jnp.log1p(jnp.exp(-jnp.abs(z)))
        g_decay = -jnp.exp(prm_ref[1:2, :]) * (jnp.maximum(z, 0.0) + soft)
        log_f = -(jnp.maximum(-z, 0.0) + soft)
        val = jnp.where(is_gc, g_decay, jnp.where(lane < CF_LANE, _sigmoid(z), jnp.where(is_cf, log_f, 0.0)))

        d = 1
        while d < ts:
            shifted = pltpu.roll(val, d, axis=0)
            val = val + jnp.where(summed & (pos >= d), shifted, 0.0)
            d *= 2
        val = val + jnp.where(is_cf, carry_ref[...], 0.0)
        carry_ref[...] = val[ts - 1:ts, :]

        gates_ref[rows, :] = val
        gates_t_ref[0, :, rows] = val.T[0:GATE_ROWS, :]

        f_qk = _dot_nt(xn, wf_ref[...])
        c2 = jnp.where(is_cf, val * LOG2E, 0.0)
        c_hi = c2.astype(BF16).astype(F32)
        c_mid = (c2 - c_hi).astype(BF16).astype(F32)
        c_lo = c2 - c_hi - c_mid
        parts = (jnp.where(lane == 0, 1.0, c_hi) + pltpu.roll(c_mid, FOX_HEADS, axis=1)
                 + pltpu.roll(c_lo, 2 * FOX_HEADS, axis=1))
        bias = _dot(parts.astype(BF16), sel_ref[...])
        for h in range(FOX_HEADS):
            src, half = divmod(h, 2)
            in_half = (lane >= half * FOX_HEAD_DIM) & (lane < (half + 1) * FOX_HEAD_DIM)
            q_slab = f_qk[:, src * LANES:(src + 1) * LANES] * (FOX_HEAD_DIM ** -0.5 * LOG2E)
            k_slab = f_qk[:, FOX_WIDTH + src * LANES:FOX_WIDTH + (src + 1) * LANES]
            q_bias = bias[:, h * LANES:(h + 1) * LANES]
            k_bias = bias[:, (FOX_HEADS + h) * LANES:(FOX_HEADS + h + 1) * LANES]
            fqa_ref[rows, h * LANES:(h + 1) * LANES] = jnp.where(in_half, q_slab, q_bias).astype(BF16)
            fka_ref[rows, h * LANES:(h + 1) * LANES] = jnp.where(in_half, k_slab, k_bias).astype(BF16)


def _bias_selector():
    sel = np.zeros((LANES, 2 * FOX_HEADS * LANES), np.float32)
    for h in range(FOX_HEADS):
        base = (1 - h % 2) * FOX_HEAD_DIM
        q0 = h * LANES + base
        k0 = (FOX_HEADS + h) * LANES + base
        for part in range(3):
            src = CF_LANE + part * FOX_HEADS + h
            sel[src, q0 + part] = 1.0
            sel[0, q0 + 3 + part] = -1.0
            sel[0, k0 + part] = 1.0
            sel[src, k0 + 3 + part] = 1.0
    return jnp.asarray(sel, BF16)


def _inproj(x2, nw, w_t, splits, conv_w, prm, *, batch, seq):
    sel = _bias_selector()
    m, d = x2.shape
    tm = TM_IN
    tiles_per_seq = seq // tm
    grid = (1 + m // tm,)
    tile = lambda s: jnp.maximum(s - 1, 0)
    tok = lambda w: pl.BlockSpec((tm, w), lambda s: (tile(s), 0))
    seq_t = lambda rows: pl.BlockSpec(
        (1, rows, tm), lambda s: (tile(s) // tiles_per_seq, 0, tile(s) % tiles_per_seq))
    c0, c1, c2 = splits
    out_shape = (
        jax.ShapeDtypeStruct((m, GDN_WIDTH), F32),
        jax.ShapeDtypeStruct((m, GDN_WIDTH), F32),
        jax.ShapeDtypeStruct((m, GDN_WIDTH), F32),
        jax.ShapeDtypeStruct((m, GDN_WIDTH), F32),
        jax.ShapeDtypeStruct((m, FOX_HEADS * LANES), BF16),
        jax.ShapeDtypeStruct((m, FOX_HEADS * LANES), BF16),
        jax.ShapeDtypeStruct((batch, FOX_HEADS * FOX_VT_ROWS, seq), BF16),
        jax.ShapeDtypeStruct((m, LANES), F32),
        jax.ShapeDtypeStruct((batch, GATE_ROWS, seq), F32),
    )
    out_specs = (
        tok(GDN_WIDTH), tok(GDN_WIDTH), tok(GDN_WIDTH), tok(GDN_WIDTH),
        tok(FOX_HEADS * LANES), tok(FOX_HEADS * LANES), seq_t(FOX_HEADS * FOX_VT_ROWS), tok(LANES),
        seq_t(GATE_ROWS),
    )
    in_specs = [
        tok(d), _const_spec(nw.shape), _const_spec(w_t.shape), _const_spec(conv_w.shape),
        _const_spec(prm.shape), _const_spec(sel.shape),
    ]
    return pl.pallas_call(
        functools.partial(_inproj_kernel, splits=splits, tm=tm, ts=TS_IN, tiles_per_seq=tiles_per_seq),
        grid=grid, in_specs=in_specs, out_specs=out_specs, out_shape=out_shape,
        scratch_shapes=[pltpu.VMEM((tm + SUBLANES, 3 * GDN_WIDTH), F32), pltpu.VMEM((1, LANES), F32),
                        pltpu.VMEM((c0, d), BF16), pltpu.VMEM((c1 - c0, d), BF16),
                        pltpu.VMEM((2 * FOX_WIDTH, d), BF16), pltpu.VMEM((FOX_WIDTH, d), BF16),
                        pltpu.VMEM((LANES, d), BF16)],
        compiler_params=pltpu.CompilerParams(dimension_semantics=("arbitrary",), vmem_limit_bytes=VMEM_LIMIT),
        name="inproj",
    )(x2, nw, w_t, conv_w, prm, sel)


def _gdn_kernel(q_ref, k_ref, v_ref, z_ref, gates_ref, gates_t_ref, nw_ref, o_ref, state_ref, *, bg, npair):
    @pl.when(pl.program_id(1) == 0)
    def _():
        state_ref[...] = jnp.zeros_like(state_ref)

    ri = lax.broadcasted_iota(jnp.int32, (PAIR, PAIR), 0)
    ci = lax.broadcasted_iota(jnp.int32, (PAIR, PAIR), 1)
    same_chunk = (ri // CHUNK) == (ci // CHUNK)
    lower = same_chunk & (ri >= ci)
    strict = same_chunk & (ri > ci)
    eye = (ri == ci).astype(F32)
    first_chunk_row = lax.broadcasted_iota(jnp.int32, (PAIR, LANES), 0) < CHUNK
    nw = nw_ref[...]

    seqs = [(b, h) for b in range(bg) for h in range(GDN_HEADS)]
    chains = [(b, pr, h) for pr in range(npair) for b, h in seqs]
    rows_of = lambda pr: slice(pr * PAIR, (pr + 1) * PAIR)
    cols_of = lambda h: slice(h * GDN_HEAD_DIM, (h + 1) * GDN_HEAD_DIM)
    col_of = lambda gt, lane: gt[:, lane:lane + 1]

    gts = {(b, pr): gates_ref[b, rows_of(pr), :] for b in range(bg) for pr in range(npair)}
    gt_ts = {(b, pr): gates_t_ref[b, :, rows_of(pr)] for b in range(bg) for pr in range(npair)}
    k16s, a_mats, decays, e_gcs = [], [], [], []
    for b, pr, h in chains:
        k = k_ref[b, rows_of(pr), cols_of(h)]
        gc = col_of(gts[b, pr], GC_LANE + h)
        gc_row = gt_ts[b, pr][GC_LANE + h:GC_LANE + h + 1, :]
        decay = jnp.where(lower, jnp.exp(jnp.where(lower, gc - gc_row, 0.0)), 0.0)
        k16 = k.astype(BF16)
        a_mats.append(
            jnp.where(strict, _dot_nt((k * col_of(gts[b, pr], BETA_LANE + h)).astype(BF16), k16) * decay, 0.0))
        k16s.append(k16)
        decays.append(decay)
        e_gcs.append(jnp.exp(gc))

    invs = [eye - a for a in a_mats]
    pows = a_mats
    for _ in range(5):
        pows = [_dot_inv(a, a) for a in pows]
        invs = [_dot_inv(t, eye + a) for t, a in zip(invs, pows)]

    us, wqs, qks, kdts, decs = {}, {}, {}, {}, {}
    for n, (b, pr, h) in enumerate(chains):
        rows = rows_of(pr)
        gc = col_of(gts[b, pr], GC_LANE + h)
        beta = col_of(gts[b, pr], BETA_LANE + h)
        q = q_ref[b, rows, cols_of(h)]
        k = k_ref[b, rows, cols_of(h)]
        inv16 = invs[n].astype(BF16)
        us[b, pr, h] = _dot(inv16, (v_ref[b, rows, cols_of(h)] * beta).astype(BF16))
        w = _dot(inv16, ((k * beta) * e_gcs[n]).astype(BF16)).astype(BF16)
        qe = (q * e_gcs[n]).astype(BF16)
        wqs[b, pr, h] = [jnp.concatenate([w[c * CHUNK:(c + 1) * CHUNK], qe[c * CHUNK:(c + 1) * CHUNK]], axis=0)
                         for c in range(PAIR // CHUNK)]
        qks[b, pr, h] = jnp.where(lower, _dot_nt(q.astype(BF16), k16s[n]) * decays[n], 0.0).astype(BF16)
        g_last = jnp.where(first_chunk_row, gts[b, pr][CHUNK - 1:CHUNK, :], gts[b, pr][PAIR - 1:PAIR, :])
        g_last = col_of(g_last, GC_LANE + h)
        kdts[b, pr, h] = (k * jnp.exp(g_last - gc)).T.astype(BF16)
        decs[b, pr, h] = jnp.exp(g_last)

    states = [state_ref[b * GDN_HEADS + h] for b, h in seqs]
    zeros = jnp.zeros((CHUNK, GDN_HEAD_DIM), BF16)
    for pr in range(npair):
        for c in range(PAIR // CHUNK):
            rows = slice(c * CHUNK, (c + 1) * CHUNK)
            out_rows = slice(pr * PAIR + c * CHUNK, pr * PAIR + (c + 1) * CHUNK)
            rs = [_dot(wqs[b, pr, h][c], states[n].astype(BF16)) for n, (b, h) in enumerate(seqs)]
            v_news = [(us[b, pr, h][rows] - rs[n][0:CHUNK]).astype(BF16) for n, (b, h) in enumerate(seqs)]
            rhs = [jnp.concatenate([v, zeros] if c == 0 else [zeros, v], axis=0) for v in v_news]
            upds = [_dot(kdts[b, pr, h], rhs[n]) for n, (b, h) in enumerate(seqs)]
            states = [states[n] * decs[b, pr, h][(c + 1) * CHUNK - 1:(c + 1) * CHUNK, :] + upds[n]
                      for n, (b, h) in enumerate(seqs)]
            for n, (b, h) in enumerate(seqs):
                o = rs[n][CHUNK:2 * CHUNK] + _dot(qks[b, pr, h][rows], rhs[n])
                o = o * lax.rsqrt(jnp.mean(o * o, axis=-1, keepdims=True) + EPS) * nw
                o_ref[b, out_rows, cols_of(h)] = (o * _silu(z_ref[b, out_rows, cols_of(h)])).astype(o_ref.dtype)
    for n, (b, h) in enumerate(seqs):
        state_ref[b * GDN_HEADS + h] = states[n]


def _gdn(gq, gk, gv, gz, gates, gates_t, nw, *, batch, seq):
    m = gq.shape[0]
    bg = BG_GDN
    npair = NPAIR_GDN
    tokens = npair * PAIR
    seq3 = lambda a: a.reshape(batch, seq, a.shape[-1])
    blk = lambda w: pl.BlockSpec((bg, tokens, w), lambda bi, j: (bi, j, 0))
    o = pl.pallas_call(
        functools.partial(_gdn_kernel, bg=bg, npair=npair),
        grid=(batch // bg, seq // tokens),
        in_specs=[blk(GDN_WIDTH), blk(GDN_WIDTH), blk(GDN_WIDTH), blk(GDN_WIDTH), blk(LANES),
                  pl.BlockSpec((bg, GATE_ROWS, tokens), lambda bi, j: (bi, 0, j)), _const_spec(nw.shape)],
        out_specs=blk(GDN_WIDTH),
        out_shape=jax.ShapeDtypeStruct((batch, seq, GDN_WIDTH), BF16),
        scratch_shapes=[pltpu.VMEM((bg * GDN_HEADS, GDN_HEAD_DIM, GDN_HEAD_DIM), F32)],
        compiler_params=pltpu.CompilerParams(
            dimension_semantics=("arbitrary", "arbitrary"), vmem_limit_bytes=VMEM_LIMIT),
        name="gdn",
    )(seq3(gq), seq3(gk), seq3(gv), seq3(gz), seq3(gates), gates_t, nw)
    return o.reshape(m, GDN_WIDTH)


def _fox_kernel(qa_ref, ka_ref, vt_ref, *refs, tq, n_weights):
    w_src = refs[:n_weights]
    o_ref = refs[n_weights]
    w_dst = refs[n_weights + 1:2 * n_weights + 1]
    m_ref, acc_ref = refs[2 * n_weights + 1:]
    i = pl.program_id(2)
    heads = qa_ref.shape[1] // LANES

    @pl.when(i == 0)
    def _():
        for src, dst in zip(w_src, w_dst):
            dst[...] = src[...].astype(BF16)

    def chunk(start, size, q0, nq, masked, first=False):
        ka = ka_ref[pl.ds(start, size), :]
        sts = [_dot_nt(ka[:, hh * LANES:(hh + 1) * LANES], qa_ref[q0:q0 + nq, hh * LANES:(hh + 1) * LANES])
               for hh in range(heads)]
        if masked:
            ri = lax.broadcasted_iota(jnp.int32, (size, nq), 0)
            ci = lax.broadcasted_iota(jnp.int32, (size, nq), 1)
            keep = ri <= ci
            sts = [jnp.where(keep, s, NEG_BIG) for s in sts]
        qs = slice(q0, q0 + nq)
        col_max = [jnp.max(s, axis=0, keepdims=True) for s in sts]
        if first:
            m_news = col_max
        else:
            m_olds = [m_ref[hh][:, qs] for hh in range(heads)]
            m_news = [jnp.maximum(m, c) for m, c in zip(m_olds, col_max)]
        ps = [jnp.exp2(s - m).astype(BF16) for s, m in zip(sts, m_news)]
        pvs = [_dot(vt_ref[0, hh * FOX_VT_ROWS:(hh + 1) * FOX_VT_ROWS, pl.ds(start, size)], ps[hh])
               for hh in range(heads)]
        for hh in range(heads):
            rows = slice(hh * FOX_VT_ROWS, (hh + 1) * FOX_VT_ROWS)
            m_ref[hh, :, qs] = m_news[hh]
            if first:
                acc_ref[rows, qs] = pvs[hh]
            else:
                acc_ref[rows, qs] = jnp.exp2(m_olds[hh] - m_news[hh]) * acc_ref[rows, qs] + pvs[hh]

    half = tq // 2
    chunk(pl.multiple_of(i * tq, tq), half, 0, tq, True, first=True)
    chunk(pl.multiple_of(i * tq + half, half), half, half, half, True)

    def body(j, carry):
        chunk(pl.multiple_of(j * tq, tq), tq, 0, tq, False)
        return carry

    lax.fori_loop(0, i, body, 0)

    out_t = jnp.concatenate(
        [acc_ref[hh * FOX_VT_ROWS:hh * FOX_VT_ROWS + FOX_HEAD_DIM, :]
         / acc_ref[hh * FOX_VT_ROWS + FOX_HEAD_DIM:hh * FOX_VT_ROWS + FOX_HEAD_DIM + 1, :]
         for hh in range(heads)], axis=0)
    o_ref[0] = out_t.astype(o_ref.dtype)


def _fox(fqa, fka, fvt, weights, *, batch, seq):
    tq = TQ_FOX
    steps = seq // tq
    heads = HEADS_FOX
    assert heads == FOX_HEADS, "the weight chunks are indexed by the sequence only"
    wchunk = lambda w: pl.BlockSpec((w.shape[0] // batch, w.shape[1]), lambda b, p, i: (b, 0))
    outs = pl.pallas_call(
        functools.partial(_fox_kernel, tq=tq, n_weights=len(weights)),
        grid=(batch, FOX_HEADS // heads, steps),
        in_specs=[pl.BlockSpec((tq, heads * LANES), lambda b, p, i: (b * steps + i, p)),
                  pl.BlockSpec((seq, heads * LANES), lambda b, p, i: (b, p)),
                  pl.BlockSpec((1, heads * FOX_VT_ROWS, seq), lambda b, p, i: (b, p, 0))]
                 + [wchunk(w) for w in weights],
        out_specs=[pl.BlockSpec((1, heads * FOX_HEAD_DIM, tq), lambda b, p, i: (b, p, i))]
                  + [wchunk(w) for w in weights],
        out_shape=[jax.ShapeDtypeStruct((batch, FOX_WIDTH, seq), BF16)]
                  + [jax.ShapeDtypeStruct(w.shape, BF16) for w in weights],
        scratch_shapes=[pltpu.VMEM((heads, 1, tq), F32), pltpu.VMEM((heads * FOX_VT_ROWS, tq), F32)],
        compiler_params=pltpu.CompilerParams(
            dimension_semantics=("arbitrary", "arbitrary", "arbitrary"), vmem_limit_bytes=VMEM_LIMIT),
        name="fox",
    )(fqa, fka, fvt, *weights)
    return outs[0], tuple(outs[1:])


def _post_kernel(x_ref, og_ref, of_ref, p_ref, wo_ref, wgu_ref, wdown_ref, wpg_ref, wpp_ref,
                 ffn_nw_ref, ple_nw_ref, fin_nw_ref, out_ref, act_ref, *, ts, tf):
    d_ff = wdown_ref.shape[0]
    tm = x_ref.shape[0]
    groups = [slice(r, r + ts) for r in range(0, tm, ts)]
    hs = [x_ref[g, :] + _dot(og_ref[g, :], wo_ref[0:GDN_WIDTH, :])
          + _dot_tn(of_ref[0, :, g], wo_ref[GDN_WIDTH:GDN_WIDTH + FOX_WIDTH, :]) for g in groups]

    hns = [(h * _rms_scale(h) * ffn_nw_ref[...]).astype(BF16) for h in hs]
    for g, hn in zip(groups, hns):
        for c in range(d_ff // tf):
            gate = _dot(hn, wgu_ref[:, c * tf:(c + 1) * tf])
            up = _dot(hn, wgu_ref[:, d_ff + c * tf:d_ff + (c + 1) * tf])
            act_ref[g, c * tf:(c + 1) * tf] = (_silu(gate) * up).astype(BF16)
    hs = [h + _dot(act_ref[g, :], wdown_ref[...]) for g, h in zip(groups, hs)]

    hns = [(h * _rms_scale(h) * ple_nw_ref[...]).astype(BF16) for h in hs]
    ple_gates = [_sigmoid(_dot(hn, wpg_ref[...])) for hn in hns]
    hs = [h + pg * _dot(p_ref[g, :].astype(BF16), wpp_ref[...]) for g, h, pg in zip(groups, hs, ple_gates)]

    for g, h in zip(groups, hs):
        out_ref[g, :] = h * _rms_scale(h) * fin_nw_ref[...]


def _post(x2, o_g, o_f, p2, weights, ffn_nw, ple_nw, fin_nw):
    m, d = x2.shape
    tm = TM_POST
    d_ff = weights[2].shape[0]
    tok = lambda w: pl.BlockSpec((tm, w), lambda i: (i, 0))
    norms = (ffn_nw, ple_nw, fin_nw)
    tiles_per_seq = o_f.shape[2] // tm
    of_spec = pl.BlockSpec((1, FOX_WIDTH, tm), lambda i: (i // tiles_per_seq, 0, i % tiles_per_seq))
    in_specs = ([tok(d), tok(GDN_WIDTH), of_spec, tok(p2.shape[1])]
                + [_const_spec(w.shape) for w in weights] + [_const_spec(w.shape) for w in norms])
    return pl.pallas_call(
        functools.partial(_post_kernel, ts=TS_POST, tf=TF_POST),
        grid=(m // tm,), in_specs=in_specs, out_specs=tok(d),
        out_shape=jax.ShapeDtypeStruct((m, d), F32),
        scratch_shapes=[pltpu.VMEM((tm, d_ff), BF16)],
        compiler_params=pltpu.CompilerParams(dimension_semantics=("arbitrary",), vmem_limit_bytes=VMEM_LIMIT),
        name="post",
    )(x2, o_g, o_f, p2, *weights, *norms)


def _layer(x, p_i, attn_norm_w, w_in, conv_w, a_log, dt_bias, gdn_norm_w, fox_f_bias, w_out,
           ffn_norm_w, w_gate_up, w_down, ple_norm_w, w_ple_gate, w_ple_proj, out_norm_w):
    batch, seq, d = x.shape
    m = batch * seq
    assert seq % TM_IN == 0 and seq % (NPAIR_GDN * PAIR) == 0 and seq % TQ_FOX == 0 and seq % TM_POST == 0
    assert batch % BG_GDN == 0
    row = lambda w: w.reshape(1, -1).astype(F32)

    c0 = 3 * GDN_WIDTH
    c1 = c0 + GDN_WIDTH
    c2 = c1 + 2 * GDN_HEADS
    n_small = 2 * GDN_HEADS + FOX_HEADS
    zeros4 = jnp.zeros((GDN_HEADS,), F32)
    pad = jnp.zeros((LANES - n_small,), F32)
    prm = jnp.stack([
        jnp.concatenate([dt_bias.astype(F32), zeros4, fox_f_bias.astype(F32), pad]),
        jnp.concatenate([a_log.astype(F32), zeros4, jnp.zeros((FOX_HEADS,), F32), pad]),
    ])
    prm = jnp.pad(prm, ((0, SUBLANES - prm.shape[0]), (0, 0)))

    x2 = x.reshape(m, d)
    gq, gk, gv, gz, fqa, fka, fvt, gates, gates_t = _inproj(
        x2, row(attn_norm_w), w_in.astype(F32).T, (c0, c1, c2), conv_w.astype(F32), prm, batch=batch, seq=seq)
    o_g = _gdn(gq, gk, gv, gz, gates, gates_t, row(gdn_norm_w), batch=batch, seq=seq)
    post_weights = tuple(w.astype(F32) for w in (w_out, w_gate_up, w_down, w_ple_gate, w_ple_proj))
    o_f, post_weights = _fox(fqa, fka, fvt, post_weights, batch=batch, seq=seq)

    return _post(
        x2, o_g, o_f, p_i.reshape(m, -1), post_weights, row(ffn_norm_w), row(ple_norm_w), row(out_norm_w),
    ).reshape(batch, seq, d)


def kernel(x, p, attn_norm_w, w_in, conv_w, a_log, dt_bias, gdn_norm_w, fox_f_bias, w_out, ffn_norm_w,
           w_gate_up, w_down, ple_norm_w, w_ple_gate, w_ple_proj, final_norm_w):
    assert p.shape[0] == 1, "single-layer problem"
    return _layer(x, p[0], attn_norm_w[0], w_in[0], conv_w[0], a_log[0], dt_bias[0], gdn_norm_w[0],
                  fox_f_bias[0], w_out[0], ffn_norm_w[0], w_gate_up[0], w_down[0], ple_norm_w[0],
                  w_ple_gate[0], w_ple_proj[0], final_norm_w)
```

```python
import functools

import jax
import jax.numpy as jnp
import numpy as np
from jax import lax
from jax.experimental import pallas as pl
from jax.experimental.pallas import tpu as pltpu

F32 = jnp.float32
BF16 = jnp.bfloat16

EPS = 1e-6
GDN_HEADS = 4
GDN_HEAD_DIM = 128
GDN_WIDTH = GDN_HEADS * GDN_HEAD_DIM
FOX_HEADS = 8
FOX_HEAD_DIM = 64
FOX_WIDTH = FOX_HEADS * FOX_HEAD_DIM
CONV_WIDTH = 4
CHUNK = 64
PAIR = 2 * CHUNK
LANES = 128
SUBLANES = 8
BF16_ROWS = 16
FOX_VT_ROWS = FOX_HEAD_DIM + BF16_ROWS
NEG_BIG = -1e30
LOG2E = 1.4426950408889634

GC_LANE = 0
BETA_LANE = 4
CF_LANE = 8
GATE_ROWS = 16

VMEM_LIMIT = 56 * 1024 * 1024

TM_IN = 512
TS_IN = 512
BG_GDN = 4
NPAIR_GDN = 2
TQ_FOX = 512
HEADS_FOX = 8
FOX_LOOKAHEAD = 4
TM_POST = 512
TS_POST = 256
TF_POST = 256


def _dot(a, b):
    return jnp.dot(a, b, preferred_element_type=F32)


def _dot_nt(a, b):
    return lax.dot_general(a, b, (((1,), (1,)), ((), ())), preferred_element_type=F32)


def _dot_tn(a, b):
    return lax.dot_general(a, b, (((0,), (0,)), ((), ())), preferred_element_type=F32)


def _dot_inv(a, b):
    return _dot(a.astype(BF16), b.astype(BF16))


def _sigmoid(x):
    return 0.5 * jnp.tanh(0.5 * x) + 0.5


def _silu(x):
    h = 0.5 * x
    return h * jnp.tanh(h) + h


def _rms_scale(x):
    return lax.rsqrt(jnp.mean(x * x, axis=-1, keepdims=True) + EPS)


def _const_spec(shape):
    nd = len(shape)
    return pl.BlockSpec(shape, lambda *_: (0,) * nd, pipeline_mode=pl.Buffered(1))


def _inproj_kernel(x_ref, nw_ref, wt_ref, conv_ref, prm_ref, sel_ref,
                   gq_ref, gk_ref, gv_ref, gz_ref, fqa_ref, fka_ref, fvt_ref, gates_ref, gates_t_ref,
                   buf_ref, carry_ref, wg_ref, wz_ref, wf_ref, wfvt_ref, ws_ref, *, splits, **tile_params):
    step = pl.program_id(0)

    @pl.when(step == 0)
    def _():
        c0, c1, c2 = splits
        c3 = c2 + wf_ref.shape[0]
        c4 = c3 + wfvt_ref.shape[0]
        wg_ref[...] = wt_ref[0:c0, :].astype(BF16)
        wz_ref[...] = wt_ref[c0:c1, :].astype(BF16)
        wf_ref[...] = wt_ref[c2:c3, :].astype(BF16)
        wfvt_ref[...] = wt_ref[c3:c4, :].astype(BF16)
        small = jnp.concatenate([wt_ref[c1:c2, :], wt_ref[c4:wt_ref.shape[0], :]], axis=0)
        ws_ref[...] = jnp.zeros(ws_ref.shape, BF16)
        ws_ref[0:small.shape[0], :] = small.astype(BF16)

    @pl.when(step > 0)
    def _():
        _inproj_tile(step - 1, x_ref, nw_ref, wg_ref, wz_ref, wf_ref, wfvt_ref, ws_ref, conv_ref, prm_ref,
                     sel_ref, gq_ref, gk_ref, gv_ref, gz_ref, fqa_ref, fka_ref, fvt_ref, gates_ref, gates_t_ref,
                     buf_ref, carry_ref, **tile_params)


def _inproj_tile(i, x_ref, nw_ref, wg_ref, wz_ref, wf_ref, wfvt_ref, ws_ref, conv_ref, prm_ref, sel_ref,
                 gq_ref, gk_ref, gv_ref, gz_ref, fqa_ref, fka_ref, fvt_ref, gates_ref, gates_t_ref,
                 buf_ref, carry_ref, *, tm, ts, tiles_per_seq):
    seq_start = (i % tiles_per_seq) == 0
    halo = SUBLANES

    @pl.when(seq_start)
    def _():
        buf_ref[0:halo, :] = jnp.zeros((halo, buf_ref.shape[1]), F32)
        carry_ref[...] = jnp.zeros_like(carry_ref)

    @pl.when(jnp.logical_not(seq_start))
    def _():
        buf_ref[0:halo, :] = buf_ref[tm:tm + halo, :]

    lane = lax.broadcasted_iota(jnp.int32, (ts, LANES), 1)
    row = lax.broadcasted_iota(jnp.int32, (ts, LANES), 0)
    is_gc = lane < BETA_LANE
    is_cf = (lane >= CF_LANE) & (lane < CF_LANE + FOX_HEADS)
    pos = jnp.where(is_gc, row % CHUNK, row)
    summed = is_gc | is_cf
    out_refs = (gq_ref, gk_ref, gv_ref)

    for r0 in range(0, tm, ts):
        rows = slice(r0, r0 + ts)
        x = x_ref[rows, :]
        xn = (x * _rms_scale(x) * nw_ref[...]).astype(BF16)

        buf_ref[halo + r0:halo + r0 + ts, :] = _dot_nt(xn, wg_ref[...])
        for j in range(3 * GDN_HEADS):
            cols = slice(j * LANES, (j + 1) * LANES)
            acc = buf_ref[halo + r0:halo + r0 + ts, cols] * conv_ref[CONV_WIDTH - 1:CONV_WIDTH, cols]
            for t in range(CONV_WIDTH - 1):
                off = halo + r0 - (CONV_WIDTH - 1) + t
                acc = acc + buf_ref[off:off + ts, cols] * conv_ref[t:t + 1, cols]
            y = _silu(acc)
            kind, head = divmod(j, GDN_HEADS)
            if kind < 2:
                y = y * lax.rsqrt(jnp.sum(y * y, axis=-1, keepdims=True) + EPS)
            if kind == 0:
                y = y * (GDN_HEAD_DIM ** -0.5)
            out_refs[kind][rows, head * LANES:(head + 1) * LANES] = y

        gz_ref[rows, :] = _dot_nt(xn, wz_ref[...])

        vt = _dot_nt(wfvt_ref[...], xn).astype(BF16)
        for h in range(FOX_HEADS):
            v0 = h * FOX_VT_ROWS
            fvt_ref[0, v0:v0 + FOX_HEAD_DIM, rows] = vt[h * FOX_HEAD_DIM:(h + 1) * FOX_HEAD_DIM]
            fvt_ref[0, v0 + FOX_HEAD_DIM:v0 + FOX_VT_ROWS, rows] = jnp.ones((FOX_VT_ROWS - FOX_HEAD_DIM, ts), BF16)

        z = _dot_nt(xn, ws_ref[...]) + prm_ref[0:1, :]
        soft = jnp.log1p(jnp.exp(-jnp.abs(z)))
        g_decay = -jnp.exp(prm_ref[1:2, :]) * (jnp.maximum(z, 0.0) + soft)
        log_f = -(jnp.maximum(-z, 0.0) + soft)
        val = jnp.where(is_gc, g_decay, jnp.where(lane < CF_LANE, _sigmoid(z), jnp.where(is_cf, log_f, 0.0)))

        d = 1
        while d < ts:
            shifted = pltpu.roll(val, d, axis=0)
            val = val + jnp.where(summed & (pos >= d), shifted, 0.0)
            d *= 2
        val = val + jnp.where(is_cf, carry_ref[...], 0.0)
        carry_ref[...] = val[ts - 1:ts, :]

        gates_ref[rows, :] = val
        gates_t_ref[0, :, rows] = val.T[0:GATE_ROWS, :]

        f_qk = _dot_nt(xn, wf_ref[...])
        c2 = jnp.where(is_cf, val * LOG2E, 0.0)
        c_hi = c2.astype(BF16).astype(F32)
        c_mid = (c2 - c_hi).astype(BF16).astype(F32)
        c_lo = c2 - c_hi - c_mid
        parts = (jnp.where(lane == 0, 1.0, c_hi) + pltpu.roll(c_mid, FOX_HEADS, axis=1)
                 + pltpu.roll(c_lo, 2 * FOX_HEADS, axis=1))
        bias = _dot(parts.astype(BF16), sel_ref[...])
        for h in range(FOX_HEADS):
            src, half = divmod(h, 2)
            in_half = (lane >= half * FOX_HEAD_DIM) & (lane < (half + 1) * FOX_HEAD_DIM)
            q_slab = f_qk[:, src * LANES:(src + 1) * LANES] * (FOX_HEAD_DIM ** -0.5 * LOG2E)
            k_slab = f_qk[:, FOX_WIDTH + src * LANES:FOX_WIDTH + (src + 1) * LANES]
            q_bias = bias[:, h * LANES:(h + 1) * LANES]
            k_bias = bias[:, (FOX_HEADS + h) * LANES:(FOX_HEADS + h + 1) * LANES]
            fqa_ref[rows, h * LANES:(h + 1) * LANES] = jnp.where(in_half, q_slab, q_bias).astype(BF16)
            fka_ref[rows, h * LANES:(h + 1) * LANES] = jnp.where(in_half, k_slab, k_bias).astype(BF16)


def _bias_selector():
    sel = np.zeros((LANES, 2 * FOX_HEADS * LANES), np.float32)
    for h in range(FOX_HEADS):
        base = (1 - h % 2) * FOX_HEAD_DIM
        q0 = h * LANES + base
        k0 = (FOX_HEADS + h) * LANES + base
        for part in range(3):
            src = CF_LANE + part * FOX_HEADS + h
            sel[src, q0 + part] = 1.0
            sel[0, q0 + 3 + part] = -1.0
            sel[0, k0 + part] = 1.0
            sel[src, k0 + 3 + part] = 1.0
    return jnp.asarray(sel, BF16)


def _inproj(x2, nw, w_t, splits, conv_w, prm, *, batch, seq):
    sel = _bias_selector()
    m, d = x2.shape
    tm = TM_IN
    tiles_per_seq = seq // tm
    grid = (1 + m // tm,)
    tile = lambda s: jnp.maximum(s - 1, 0)
    tok = lambda w: pl.BlockSpec((tm, w), lambda s: (tile(s), 0))
    seq_t = lambda rows: pl.BlockSpec(
        (1, rows, tm), lambda s: (tile(s) // tiles_per_seq, 0, tile(s) % tiles_per_seq))
    c0, c1, c2 = splits
    out_shape = (
        jax.ShapeDtypeStruct((m, GDN_WIDTH), F32),
        jax.ShapeDtypeStruct((m, GDN_WIDTH), F32),
        jax.ShapeDtypeStruct((m, GDN_WIDTH), F32),
        jax.ShapeDtypeStruct((m, GDN_WIDTH), F32),
        jax.ShapeDtypeStruct((m, FOX_HEADS * LANES), BF16),
        jax.ShapeDtypeStruct((m, FOX_HEADS * LANES), BF16),
        jax.ShapeDtypeStruct((batch, FOX_HEADS * FOX_VT_ROWS, seq), BF16),
        jax.ShapeDtypeStruct((m, LANES), F32),
        jax.ShapeDtypeStruct((batch, GATE_ROWS, seq), F32),
    )
    out_specs = (
        tok(GDN_WIDTH), tok(GDN_WIDTH), tok(GDN_WIDTH), tok(GDN_WIDTH),
        tok(FOX_HEADS * LANES), tok(FOX_HEADS * LANES), seq_t(FOX_HEADS * FOX_VT_ROWS), tok(LANES),
        seq_t(GATE_ROWS),
    )
    in_specs = [
        tok(d), _const_spec(nw.shape), _const_spec(w_t.shape), _const_spec(conv_w.shape),
        _const_spec(prm.shape), _const_spec(sel.shape),
    ]
    return pl.pallas_call(
        functools.partial(_inproj_kernel, splits=splits, tm=tm, ts=TS_IN, tiles_per_seq=tiles_per_seq),
        grid=grid, in_specs=in_specs, out_specs=out_specs, out_shape=out_shape,
        scratch_shapes=[pltpu.VMEM((tm + SUBLANES, 3 * GDN_WIDTH), F32), pltpu.VMEM((1, LANES), F32),
                        pltpu.VMEM((c0, d), BF16), pltpu.VMEM((c1 - c0, d), BF16),
                        pltpu.VMEM((2 * FOX_WIDTH, d), BF16), pltpu.VMEM((FOX_WIDTH, d), BF16),
                        pltpu.VMEM((LANES, d), BF16)],
        compiler_params=pltpu.CompilerParams(dimension_semantics=("arbitrary",), vmem_limit_bytes=VMEM_LIMIT),
        name="inproj",
    )(x2, nw, w_t, conv_w, prm, sel)


def _gdn_kernel(q_ref, k_ref, v_ref, z_ref, gates_ref, gates_t_ref, nw_ref, o_ref, state_ref, *, bg, npair):
    @pl.when(pl.program_id(1) == 0)
    def _():
        state_ref[...] = jnp.zeros_like(state_ref)

    ri = lax.broadcasted_iota(jnp.int32, (PAIR, PAIR), 0)
    ci = lax.broadcasted_iota(jnp.int32, (PAIR, PAIR), 1)
    same_chunk = (ri // CHUNK) == (ci // CHUNK)
    lower = same_chunk & (ri >= ci)
    strict = same_chunk & (ri > ci)
    eye = (ri == ci).astype(F32)
    first_chunk_row = lax.broadcasted_iota(jnp.int32, (PAIR, LANES), 0) < CHUNK
    nw = nw_ref[...]

    seqs = [(b, h) for b in range(bg) for h in range(GDN_HEADS)]
    chains = [(b, pr, h) for pr in range(npair) for b, h in seqs]
    rows_of = lambda pr: slice(pr * PAIR, (pr + 1) * PAIR)
    cols_of = lambda h: slice(h * GDN_HEAD_DIM, (h + 1) * GDN_HEAD_DIM)
    col_of = lambda gt, lane: gt[:, lane:lane + 1]

    gts = {(b, pr): gates_ref[b, rows_of(pr), :] for b in range(bg) for pr in range(npair)}
    gt_ts = {(b, pr): gates_t_ref[b, :, rows_of(pr)] for b in range(bg) for pr in range(npair)}
    k16s, a_mats, decays, e_gcs = [], [], [], []
    for b, pr, h in chains:
        k = k_ref[b, rows_of(pr), cols_of(h)]
        gc = col_of(gts[b, pr], GC_LANE + h)
        gc_row = gt_ts[b, pr][GC_LANE + h:GC_LANE + h + 1, :]
        decay = jnp.where(lower, jnp.exp(jnp.where(lower, gc - gc_row, 0.0)), 0.0)
        k16 = k.astype(BF16)
        a_mats.append(
            jnp.where(strict, _dot_nt((k * col_of(gts[b, pr], BETA_LANE + h)).astype(BF16), k16) * decay, 0.0))
        k16s.append(k16)
        decays.append(decay)
        e_gcs.append(jnp.exp(gc))

    invs = [eye - a for a in a_mats]
    pows = a_mats
    for _ in range(5):
        pows = [_dot_inv(a, a) for a in pows]
        invs = [_dot_inv(t, eye + a) for t, a in zip(invs, pows)]

    us, wqs, qks, kdts, decs = {}, {}, {}, {}, {}
    for n, (b, pr, h) in enumerate(chains):
        rows = rows_of(pr)
        gc = col_of(gts[b, pr], GC_LANE + h)
        beta = col_of(gts[b, pr], BETA_LANE + h)
        q = q_ref[b, rows, cols_of(h)]
        k = k_ref[b, rows, cols_of(h)]
        inv16 = invs[n].astype(BF16)
        us[b, pr, h] = _dot(inv16, (v_ref[b, rows, cols_of(h)] * beta).astype(BF16))
        w = _dot(inv16, ((k * beta) * e_gcs[n]).astype(BF16)).astype(BF16)
        qe = (q * e_gcs[n]).astype(BF16)
        wqs[b, pr, h] = [jnp.concatenate([w[c * CHUNK:(c + 1) * CHUNK], qe[c * CHUNK:(c + 1) * CHUNK]], axis=0)
                         for c in range(PAIR // CHUNK)]
        qks[b, pr, h] = jnp.where(lower, _dot_nt(q.astype(BF16), k16s[n]) * decays[n], 0.0).astype(BF16)
        g_last = jnp.where(first_chunk_row, gts[b, pr][CHUNK - 1:CHUNK, :], gts[b, pr][PAIR - 1:PAIR, :])
        g_last = col_of(g_last, GC_LANE + h)
        kdts[b, pr, h] = (k * jnp.exp(g_last - gc)).T.astype(BF16)
        decs[b, pr, h] = jnp.exp(g_last)

    states = [state_ref[b * GDN_HEADS + h] for b, h in seqs]
    zeros = jnp.zeros((CHUNK, GDN_HEAD_DIM), BF16)
    for pr in range(npair):
        for c in range(PAIR // CHUNK):
            rows = slice(c * CHUNK, (c + 1) * CHUNK)
            out_rows = slice(pr * PAIR + c * CHUNK, pr * PAIR + (c + 1) * CHUNK)
            rs = [_dot(wqs[b, pr, h][c], states[n].astype(BF16)) for n, (b, h) in enumerate(seqs)]
            v_news = [(us[b, pr, h][rows] - rs[n][0:CHUNK]).astype(BF16) for n, (b, h) in enumerate(seqs)]
            rhs = [jnp.concatenate([v, zeros] if c == 0 else [zeros, v], axis=0) for v in v_news]
            upds = [_dot(kdts[b, pr, h], rhs[n]) for n, (b, h) in enumerate(seqs)]
            states = [states[n] * decs[b, pr, h][(c + 1) * CHUNK - 1:(c + 1) * CHUNK, :] + upds[n]
                      for n, (b, h) in enumerate(seqs)]
            for n, (b, h) in enumerate(seqs):
                o = rs[n][CHUNK:2 * CHUNK] + _dot(qks[b, pr, h][rows], rhs[n])
                o = o * lax.rsqrt(jnp.mean(o * o, axis=-1, keepdims=True) + EPS) * nw
                o_ref[b, out_rows, cols_of(h)] = (o * _silu(z_ref[b, out_rows, cols_of(h)])).astype(o_ref.dtype)
    for n, (b, h) in enumerate(seqs):
        state_ref[b * GDN_HEADS + h] = states[n]


def _gdn(gq, gk, gv, gz, gates, gates_t, nw, *, batch, seq):
    m = gq.shape[0]
    bg = BG_GDN
    npair = NPAIR_GDN
    tokens = npair * PAIR
    seq3 = lambda a: a.reshape(batch, seq, a.shape[-1])
    blk = lambda w: pl.BlockSpec((bg, tokens, w), lambda bi, j: (bi, j, 0))
    o = pl.pallas_call(
        functools.partial(_gdn_kernel, bg=bg, npair=npair),
        grid=(batch // bg, seq // tokens),
        in_specs=[blk(GDN_WIDTH), blk(GDN_WIDTH), blk(GDN_WIDTH), blk(GDN_WIDTH), blk(LANES),
                  pl.BlockSpec((bg, GATE_ROWS, tokens), lambda bi, j: (bi, 0, j)), _const_spec(nw.shape)],
        out_specs=blk(GDN_WIDTH),
        out_shape=jax.ShapeDtypeStruct((batch, seq, GDN_WIDTH), BF16),
        scratch_shapes=[pltpu.VMEM((bg * GDN_HEADS, GDN_HEAD_DIM, GDN_HEAD_DIM), F32)],
        compiler_params=pltpu.CompilerParams(
            dimension_semantics=("arbitrary", "arbitrary"), vmem_limit_bytes=VMEM_LIMIT),
        name="gdn",
    )(seq3(gq), seq3(gk), seq3(gv), seq3(gz), seq3(gates), gates_t, nw)
    return o.reshape(m, GDN_WIDTH)


def _fox_kernel(qa_ref, ka_ref, vt_ref, *refs, tq, n_weights):
    w_src = refs[:n_weights]
    o_ref = refs[n_weights]
    w_dst = refs[n_weights + 1:2 * n_weights + 1]
    m_ref, acc_ref = refs[2 * n_weights + 1:]
    i = pl.program_id(2)
    heads = qa_ref.shape[1] // LANES

    @pl.when(i == 0)
    def _():
        for src, dst in zip(w_src, w_dst):
            dst[...] = src[...].astype(BF16)

    def chunk(start, size, q0, nq, masked, first=False):
        ka = ka_ref[pl.ds(start, size), :]
        qs = slice(q0, q0 + nq)
        if masked:
            ri = lax.broadcasted_iota(jnp.int32, (size, nq), 0)
            ci = lax.broadcasted_iota(jnp.int32, (size, nq), 1)
            keep = ri <= ci

        def logits(hh):
            s = _dot_nt(ka[:, hh * LANES:(hh + 1) * LANES], qa_ref[q0:q0 + nq, hh * LANES:(hh + 1) * LANES])
            return jnp.where(keep, s, NEG_BIG) if masked else s

        m_olds = None if first else [m_ref[hh][:, qs] for hh in range(heads)]
        sts = {hh: logits(hh) for hh in range(min(FOX_LOOKAHEAD, heads))}
        m_news, pvs = [], []
        for hh in range(heads):
            s = sts.pop(hh)
            col_max = jnp.max(s, axis=0, keepdims=True)
            m_new = col_max if first else jnp.maximum(m_olds[hh], col_max)
            p = jnp.exp2(s - m_new).astype(BF16)
            pvs.append(_dot(vt_ref[0, hh * FOX_VT_ROWS:(hh + 1) * FOX_VT_ROWS, pl.ds(start, size)], p))
            if hh + FOX_LOOKAHEAD < heads:
                sts[hh + FOX_LOOKAHEAD] = logits(hh + FOX_LOOKAHEAD)
            m_news.append(m_new)
        for hh in range(heads):
            rows = slice(hh * FOX_VT_ROWS, (hh + 1) * FOX_VT_ROWS)
            m_ref[hh, :, qs] = m_news[hh]
            if first:
                acc_ref[rows, qs] = pvs[hh]
            else:
                acc_ref[rows, qs] = jnp.exp2(m_olds[hh] - m_news[hh]) * acc_ref[rows, qs] + pvs[hh]

    half = tq // 2
    chunk(pl.multiple_of(i * tq, tq), half, 0, tq, True, first=True)
    chunk(pl.multiple_of(i * tq + half, half), half, half, half, True)

    def body(j, carry):
        chunk(pl.multiple_of(j * tq, tq), tq, 0, tq, False)
        return carry

    lax.fori_loop(0, i, body, 0)

    out_t = jnp.concatenate(
        [acc_ref[hh * FOX_VT_ROWS:hh * FOX_VT_ROWS + FOX_HEAD_DIM, :]
         / acc_ref[hh * FOX_VT_ROWS + FOX_HEAD_DIM:hh * FOX_VT_ROWS + FOX_HEAD_DIM + 1, :]
         for hh in range(heads)], axis=0)
    o_ref[0] = out_t.astype(o_ref.dtype)


def _fox(fqa, fka, fvt, weights, *, batch, seq):
    tq = TQ_FOX
    steps = seq // tq
    heads = HEADS_FOX
    assert heads == FOX_HEADS, "the weight chunks are indexed by the sequence only"
    wchunk = lambda w: pl.BlockSpec((w.shape[0] // batch, w.shape[1]), lambda b, p, i: (b, 0))
    outs = pl.pallas_call(
        functools.partial(_fox_kernel, tq=tq, n_weights=len(weights)),
        grid=(batch, FOX_HEADS // heads, steps),
        in_specs=[pl.BlockSpec((tq, heads * LANES), lambda b, p, i: (b * steps + i, p)),
                  pl.BlockSpec((seq, heads * LANES), lambda b, p, i: (b, p)),
                  pl.BlockSpec((1, heads * FOX_VT_ROWS, seq), lambda b, p, i: (b, p, 0))]
                 + [wchunk(w) for w in weights],
        out_specs=[pl.BlockSpec((1, heads * FOX_HEAD_DIM, tq), lambda b, p, i: (b, p, i))]
                  + [wchunk(w) for w in weights],
        out_shape=[jax.ShapeDtypeStruct((batch, FOX_WIDTH, seq), BF16)]
                  + [jax.ShapeDtypeStruct(w.shape, BF16) for w in weights],
        scratch_shapes=[pltpu.VMEM((heads, 1, tq), F32), pltpu.VMEM((heads * FOX_VT_ROWS, tq), F32)],
        compiler_params=pltpu.CompilerParams(
            dimension_semantics=("arbitrary", "arbitrary", "arbitrary"), vmem_limit_bytes=VMEM_LIMIT),
        name="fox",
    )(fqa, fka, fvt, *weights)
    return outs[0], tuple(outs[1:])


def _post_kernel(x_ref, og_ref, of_ref, p_ref, wo_ref, wgu_ref, wdown_ref, wpg_ref, wpp_ref,
                 ffn_nw_ref, ple_nw_ref, fin_nw_ref, out_ref, act_ref, *, ts, tf):
    d_ff = wdown_ref.shape[0]
    tm = x_ref.shape[0]
    groups = [slice(r, r + ts) for r in range(0, tm, ts)]
    hs = [x_ref[g, :] + _dot(og_ref[g, :], wo_ref[0:GDN_WIDTH, :])
          + _dot_tn(of_ref[0, :, g], wo_ref[GDN_WIDTH:GDN_WIDTH + FOX_WIDTH, :]) for g in groups]

    hns = [(h * _rms_scale(h) * ffn_nw_ref[...]).astype(BF16) for h in hs]
    for g, hn in zip(groups, hns):
        for c in range(d_ff // tf):
            gate = _dot(hn, wgu_ref[:, c * tf:(c + 1) * tf])
            up = _dot(hn, wgu_ref[:, d_ff + c * tf:d_ff + (c + 1) * tf])
            act_ref[g, c * tf:(c + 1) * tf] = (_silu(gate) * up).astype(BF16)
    hs = [h + _dot(act_ref[g, :], wdown_ref[...]) for g, h in zip(groups, hs)]

    hns = [(h * _rms_scale(h) * ple_nw_ref[...]).astype(BF16) for h in hs]
    ple_gates = [_sigmoid(_dot(hn, wpg_ref[...])) for hn in hns]
    hs = [h + pg * _dot(p_ref[g, :].astype(BF16), wpp_ref[...]) for g, h, pg in zip(groups, hs, ple_gates)]

    for g, h in zip(groups, hs):
        out_ref[g, :] = h * _rms_scale(h) * fin_nw_ref[...]


def _post(x2, o_g, o_f, p2, weights, ffn_nw, ple_nw, fin_nw):
    m, d = x2.shape
    tm = TM_POST
    d_ff = weights[2].shape[0]
    tok = lambda w: pl.BlockSpec((tm, w), lambda i: (i, 0))
    norms = (ffn_nw, ple_nw, fin_nw)
    tiles_per_seq = o_f.shape[2] // tm
    of_spec = pl.BlockSpec((1, FOX_WIDTH, tm), lambda i: (i // tiles_per_seq, 0, i % tiles_per_seq))
    in_specs = ([tok(d), tok(GDN_WIDTH), of_spec, tok(p2.shape[1])]
                + [_const_spec(w.shape) for w in weights] + [_const_spec(w.shape) for w in norms])
    return pl.pallas_call(
        functools.partial(_post_kernel, ts=TS_POST, tf=TF_POST),
        grid=(m // tm,), in_specs=in_specs, out_specs=tok(d),
        out_shape=jax.ShapeDtypeStruct((m, d), F32),
        scratch_shapes=[pltpu.VMEM((tm, d_ff), BF16)],
        compiler_params=pltpu.CompilerParams(dimension_semantics=("arbitrary",), vmem_limit_bytes=VMEM_LIMIT),
        name="post",
    )(x2, o_g, o_f, p2, *weights, *norms)


def _layer(x, p_i, attn_norm_w, w_in, conv_w, a_log, dt_bias, gdn_norm_w, fox_f_bias, w_out,
           ffn_norm_w, w_gate_up, w_down, ple_norm_w, w_ple_gate, w_ple_proj, out_norm_w):
    batch, seq, d = x.shape
    m = batch * seq
    assert seq % TM_IN == 0 and seq % (NPAIR_GDN * PAIR) == 0 and seq % TQ_FOX == 0 and seq % TM_POST == 0
    assert batch % BG_GDN == 0
    row = lambda w: w.reshape(1, -1).astype(F32)

    c0 = 3 * GDN_WIDTH
    c1 = c0 + GDN_WIDTH
    c2 = c1 + 2 * GDN_HEADS
    n_small = 2 * GDN_HEADS + FOX_HEADS
    zeros4 = jnp.zeros((GDN_HEADS,), F32)
    pad = jnp.zeros((LANES - n_small,), F32)
    prm = jnp.stack([
        jnp.concatenate([dt_bias.astype(F32), zeros4, fox_f_bias.astype(F32), pad]),
        jnp.concatenate([a_log.astype(F32), zeros4, jnp.zeros((FOX_HEADS,), F32), pad]),
    ])
    prm = jnp.pad(prm, ((0, SUBLANES - prm.shape[0]), (0, 0)))

    x2 = x.reshape(m, d)
    gq, gk, gv, gz, fqa, fka, fvt, gates, gates_t = _inproj(
        x2, row(attn_norm_w), w_in.astype(F32).T, (c0, c1, c2), conv_w.astype(F32), prm, batch=batch, seq=seq)
    o_g = _gdn(gq, gk, gv, gz, gates, gates_t, row(gdn_norm_w), batch=batch, seq=seq)
    post_weights = tuple(w.astype(F32) for w in (w_out, w_gate_up, w_down, w_ple_gate, w_ple_proj))
    o_f, post_weights = _fox(fqa, fka, fvt, post_weights, batch=batch, seq=seq)

    return _post(
        x2, o_g, o_f, p_i.reshape(m, -1), post_weights, row(ffn_norm_w), row(ple_norm_w), row(out_norm_w),
    ).reshape(batch, seq, d)


def kernel(x, p, attn_norm_w, w_in, conv_w, a_log, dt_bias, gdn_norm_w, fox_f_bias, w_out, ffn_norm_w,
           w_gate_up, w_down, ple_norm_w, w_ple_gate, w_ple_proj, final_norm_w):
    assert p.shape[0] == 1, "single-layer problem"
    return _layer(x, p[0], attn_norm_w[0], w_in[0], conv_w[0], a_log[0], dt_bias[0], gdn_norm_w[0],
                  fox_f_bias[0], w_out[0], ffn_norm_w[0], w_gate_up[0], w_down[0], ple_norm_w[0],
                  w_ple_gate[0], w_ple_proj[0], final_norm_w)
```
